```python
import jax, jax.numpy as jnp
from jax import lax
import numpy as np

D_MODEL = 1024
BATCH = 8
SEQ = 2048
DEPTH = 1

MEM_LEN = 256
D_MIX = D_MODEL
D_SGU = D_MIX // 2
SGU_HEADS = 4
SGU_HEAD_DIM = D_SGU // SGU_HEADS
CHUNK = 128
D_RWKV = D_MIX - D_SGU
RWKV_HEAD_DIM = 64
RWKV_HEADS = D_RWKV // RWKV_HEAD_DIM
DECAY_LORA = 64
ICL_LORA = 64
C_SGU = 3 * D_SGU
C_RWKV = 4 * D_RWKV + DECAY_LORA + ICL_LORA
C_IN = C_SGU + C_RWKV
XATTN_HEADS = 4
XATTN_HEAD_DIM = D_MODEL // XATTN_HEADS
RMS_EPS = 1e-6
LN_EPS = 1e-5
GN_EPS = 64e-5

kernel_name = 'hybrid_sgu_rwkv7_memxattn'


def rms_norm(x, g):
    xf = x.astype(jnp.float32)
    y = xf * lax.rsqrt(jnp.mean(xf * xf, axis=-1, keepdims=True) + RMS_EPS)
    return (y * g.astype(jnp.float32)).astype(x.dtype)


def layer_norm(x, g, b):
    xf = x.astype(jnp.float32)
    mu = jnp.mean(xf, axis=-1, keepdims=True)
    var = jnp.mean(jnp.square(xf - mu), axis=-1, keepdims=True)
    y = (xf - mu) * lax.rsqrt(var + LN_EPS)
    return (y * g.astype(jnp.float32) + b.astype(jnp.float32)).astype(x.dtype)


def token_shift(z):
    return jnp.pad(z, ((0, 0), (1, 0), (0, 0)))[:, :-1]


def sgu_group(z, ln_g, ln_b, ws, bs, out_g):
    b, s, _ = z.shape
    u, v, gate = jnp.split(z, 3, axis=-1)
    u = jax.nn.gelu(u, approximate=False)
    v = layer_norm(jax.nn.gelu(v, approximate=False), ln_g, ln_b)
    v = v.reshape(b, s // CHUNK, CHUNK, SGU_HEADS, SGU_HEAD_DIM)
    causal = jnp.tril(jnp.ones((CHUNK, CHUNK), dtype=bool))
    ws_c = jnp.where(causal[None], ws, jnp.zeros_like(ws))
    sv = jnp.einsum('hts,bcshd->bcthd', ws_c, v) + jnp.swapaxes(bs, 0, 1)[None, None, :, :, None]
    y = rms_norm(u * sv.reshape(b, s, D_SGU), out_g)
    return y * jax.nn.silu(gate)


def rwkv7_step(state, inp):
    r, w, k, v, a, bb = inp
    sa = jnp.einsum('bhvk,bhk->bhv', state, a)
    state = state * w[:, :, None, :] + sa[..., None] * bb[:, :, None, :] + v[..., None] * k[:, :, None, :]
    y = jnp.einsum('bhvk,bhk->bhv', state, r)
    return state, y


def rwkv7_group(z, mu, w0, w2, a0, a2, k_k, k_a, r_k, gn_g, gn_b):
    b, s, _ = z.shape
    z = z + (token_shift(z) - z) * mu
    r, k, v, gate, wd, ad = jnp.split(
        z, [D_RWKV, 2 * D_RWKV, 3 * D_RWKV, 4 * D_RWKV, 4 * D_RWKV + DECAY_LORA], axis=-1)
    logw = -jax.nn.softplus(-(w0 + jnp.tanh(wd) @ w2)) - 0.5
    decay = jnp.exp(-jnp.exp(logw.astype(jnp.float32)))
    icl = jax.nn.sigmoid(a0 + ad @ a2)
    heads = lambda t: t.astype(jnp.float32).reshape(b, s, RWKV_HEADS, RWKV_HEAD_DIM)
    kk = heads(k * k_k)
    kk = kk * lax.rsqrt(jnp.maximum(jnp.sum(kk * kk, axis=-1, keepdims=True), 1e-24))
    k = k * (1 + (icl - 1) * k_a)
    rh, wh, kh, vh, ah = heads(r), heads(decay), heads(k), heads(v), heads(icl)
    seq_major = lambda t: jnp.moveaxis(t, 1, 0)
    state0 = jnp.zeros((b, RWKV_HEADS, RWKV_HEAD_DIM, RWKV_HEAD_DIM), jnp.float32)
    _, y = lax.scan(rwkv7_step, state0,
                    (seq_major(rh), seq_major(wh), seq_major(kh), seq_major(vh),
                     seq_major(-kk), seq_major(kk * ah)))
    y = jnp.moveaxis(y, 0, 1)
    mean = jnp.mean(y, axis=-1, keepdims=True)
    var = jnp.mean(jnp.square(y - mean), axis=-1, keepdims=True)
    yn = ((y - mean) * lax.rsqrt(var + GN_EPS)).reshape(b, s, D_RWKV)
    yn = yn * gn_g.astype(jnp.float32) + gn_b.astype(jnp.float32)
    bonus = jnp.sum(rh * kh * r_k.astype(jnp.float32), axis=-1, keepdims=True) * vh
    out = (yn + bonus.reshape(b, s, D_RWKV)).astype(z.dtype)
    return out * jax.nn.silu(gate)


def mem_cross_attention(h, mem, g_x, g_mem, w_q, w_kv, w_o):
    b, s, _ = h.shape
    m = mem.shape[1]
    hn = rms_norm(h, g_x)
    mn = rms_norm(mem, g_mem)
    q = (hn @ w_q).reshape(b, s, XATTN_HEADS, XATTN_HEAD_DIM)
    k, v = jnp.split(mn @ w_kv, 2, axis=-1)
    k = k.reshape(b, m, XATTN_HEADS, XATTN_HEAD_DIM)
    v = v.reshape(b, m, XATTN_HEADS, XATTN_HEAD_DIM)
    scores = jnp.einsum('bqhd,bkhd->bhqk', q, k).astype(jnp.float32) * (XATTN_HEAD_DIM ** -0.5)
    p = jax.nn.softmax(scores, axis=-1).astype(v.dtype)
    o = jnp.einsum('bhqk,bkhd->bqhd', p, v).reshape(b, s, D_MODEL)
    return o @ w_o


def setup_inputs(seed: int = 0) -> dict:
    key = jax.random.key(seed)
    ks = jax.random.split(key, 26)
    L = DEPTH
    nrm = lambda k, shape, scale: scale * jax.random.normal(k, shape, jnp.float32)
    gain = lambda k, shape: 1.0 + 0.01 * jax.random.normal(k, shape, jnp.float32)
    return {
        'x': nrm(ks[0], (BATCH, SEQ, D_MODEL), 1.0),
        'mem': nrm(ks[1], (BATCH, MEM_LEN, D_MODEL), 1.0),
        'ln_mix_g': gain(ks[2], (L, D_MODEL)),
        'w_in': nrm(ks[3], (L, D_MODEL, C_IN), D_MODEL ** -0.5),
        'sgu_ln_g': gain(ks[4], (L, D_SGU)),
        'sgu_ln_b': nrm(ks[5], (L, D_SGU), 0.01),
        'sgu_ws': nrm(ks[6], (L, SGU_HEADS, CHUNK, CHUNK), CHUNK ** -0.5),
        'sgu_bs': gain(ks[7], (L, SGU_HEADS, CHUNK)),
        'sgu_out_g': gain(ks[8], (L, D_SGU)),
        'rw_mu': jax.random.uniform(ks[9], (L, C_RWKV), jnp.float32),
        'rw_w0': jax.random.uniform(ks[10], (L, D_RWKV), jnp.float32, -4.0, 0.0),
        'rw_w2': nrm(ks[11], (L, DECAY_LORA, D_RWKV), 0.1),
        'rw_a0': nrm(ks[12], (L, D_RWKV), 0.1),
        'rw_a2': nrm(ks[13], (L, ICL_LORA, D_RWKV), 0.1),
        'rw_k_k': 0.85 + 0.01 * jax.random.normal(ks[14], (L, D_RWKV), jnp.float32),
        'rw_k_a': gain(ks[15], (L, D_RWKV)),
        'rw_r_k': nrm(ks[16], (L, RWKV_HEADS, RWKV_HEAD_DIM), 0.1),
        'rw_gn_g': gain(ks[17], (L, D_RWKV)),
        'rw_gn_b': nrm(ks[18], (L, D_RWKV), 0.01),
        'w_out': nrm(ks[19], (L, D_MIX, D_MODEL), D_MIX ** -0.5),
        'ln_x_g': gain(ks[20], (L, D_MODEL)),
        'ln_mem_g': gain(ks[21], (L, D_MODEL)),
        'w_q': nrm(ks[22], (L, D_MODEL, D_MODEL), D_MODEL ** -0.5),
        'w_kv': nrm(ks[23], (L, D_MODEL, 2 * D_MODEL), D_MODEL ** -0.5),
        'w_o': nrm(ks[24], (L, D_MODEL, D_MODEL), D_MODEL ** -0.5),
        'ln_f_g': gain(ks[25], (D_MODEL,)),
    }


def reference(x, mem, ln_mix_g, w_in, sgu_ln_g, sgu_ln_b, sgu_ws, sgu_bs, sgu_out_g,
              rw_mu, rw_w0, rw_w2, rw_a0, rw_a2, rw_k_k, rw_k_a, rw_r_k, rw_gn_g, rw_gn_b,
              w_out, ln_x_g, ln_mem_g, w_q, w_kv, w_o, ln_f_g):
    h = x
    for l in range(DEPTH):
        z = rms_norm(h, ln_mix_g[l]) @ w_in[l]
        y_sgu = sgu_group(z[..., :C_SGU], sgu_ln_g[l], sgu_ln_b[l], sgu_ws[l], sgu_bs[l], sgu_out_g[l])
        y_rwkv = rwkv7_group(z[..., C_SGU:], rw_mu[l], rw_w0[l], rw_w2[l], rw_a0[l], rw_a2[l],
                             rw_k_k[l], rw_k_a[l], rw_r_k[l], rw_gn_g[l], rw_gn_b[l])
        h = h + jnp.concatenate([y_sgu, y_rwkv], axis=-1) @ w_out[l]
        h = h + mem_cross_attention(h, mem, ln_x_g[l], ln_mem_g[l], w_q[l], w_kv[l], w_o[l])
    return rms_norm(h, ln_f_g)
```

```python
import functools

import jax
import jax.numpy as jnp
from jax import lax
from jax.experimental import pallas as pl
from jax.experimental.pallas import tpu as pltpu

D_MODEL = 1024
MEM_LEN = 256
D_SGU = 512
SGU_HEADS = 4
SGU_HEAD_DIM = D_SGU // SGU_HEADS
SGU_CHUNK = 128
D_RWKV = 512
RWKV_HEAD_DIM = 64
RWKV_HEADS = D_RWKV // RWKV_HEAD_DIM
LORA = 64
C_SGU = 3 * D_SGU
C_RWKV = 4 * D_RWKV + 2 * LORA
XATTN_HEADS = 4
XATTN_HEAD_DIM = D_MODEL // XATTN_HEADS
RMS_EPS = 1e-6
LN_EPS = 1e-5
GN_EPS = 64e-5

RWKV_CHUNK = 64
IN_PROJ_ROWS = 512
SGU_ROWS = 256
ATTN_ROWS = 256
VMEM_LIMIT_BYTES = 48 * 1024 * 1024

F32 = jnp.float32
BF16 = jnp.bfloat16


def _dot(a, b):
    return jnp.dot(a.astype(BF16), b.astype(BF16), preferred_element_type=F32)


def _dot_nt(a, b):
    return lax.dot_general(a.astype(BF16), b.astype(BF16), (((1,), (1,)), ((), ())),
                           preferred_element_type=F32)


def _dot_tn(a, b):
    return lax.dot_general(a.astype(BF16), b.astype(BF16), (((0,), (0,)), ((), ())),
                           preferred_element_type=F32)


def _rms_norm(x, g):
    return x * lax.rsqrt(jnp.mean(x * x, axis=-1, keepdims=True) + RMS_EPS) * g


def _gelu(x):
    return 0.5 * x * (1.0 + lax.erf(x * (2.0 ** -0.5)))


def _silu(x):
    return x * jax.nn.sigmoid(x)


def _in_proj_kernel(x_ref, g_ref, w_ref, zs_ref, zr_ref):
    xn = _rms_norm(x_ref[...], g_ref[...]).astype(BF16)
    zs_ref[...] = jnp.dot(xn, w_ref[:, :C_SGU], preferred_element_type=F32)
    zr_ref[...] = jnp.dot(xn, w_ref[:, C_SGU:], preferred_element_type=F32)


def _in_proj(x2, g, w_in):
    m = x2.shape[0]
    tm = IN_PROJ_ROWS
    return pl.pallas_call(
        _in_proj_kernel,
        grid=(m // tm,),
        in_specs=[
            pl.BlockSpec((tm, D_MODEL), lambda i: (i, 0)),
            pl.BlockSpec((1, D_MODEL), lambda i: (0, 0)),
            pl.BlockSpec((D_MODEL, C_SGU + C_RWKV), lambda i: (0, 0)),
        ],
        out_specs=[
            pl.BlockSpec((tm, C_SGU), lambda i: (i, 0)),
            pl.BlockSpec((tm, C_RWKV), lambda i: (i, 0)),
        ],
        out_shape=[
            jax.ShapeDtypeStruct((m, C_SGU), F32),
            jax.ShapeDtypeStruct((m, C_RWKV), F32),
        ],
        compiler_params=pltpu.CompilerParams(
            dimension_semantics=("arbitrary",), vmem_limit_bytes=VMEM_LIMIT_BYTES),
        name="in_proj",
    )(x2, g, w_in)


def _sgu_kernel(zu_ref, zv_ref, zg_ref, lng_ref, lnb_ref, ws_ref, bs_ref, og_ref, o_ref, sv_ref):
    u = _gelu(zu_ref[...])
    v = _gelu(zv_ref[...])
    mu = jnp.mean(v, axis=-1, keepdims=True)
    vc = v - mu
    var = jnp.mean(vc * vc, axis=-1, keepdims=True)
    vn = (vc * lax.rsqrt(var + LN_EPS) * lng_ref[...] + lnb_ref[...]).astype(BF16)
    row = lax.broadcasted_iota(jnp.int32, (SGU_CHUNK, SGU_CHUNK), 0)
    col = lax.broadcasted_iota(jnp.int32, (SGU_CHUNK, SGU_CHUNK), 1)
    causal = col <= row
    for h in range(SGU_HEADS):
        cols = slice(h * SGU_HEAD_DIM, (h + 1) * SGU_HEAD_DIM)
        ws_c = jnp.where(causal, ws_ref[h], 0.0).astype(BF16)
        for c in range(SGU_ROWS // SGU_CHUNK):
            rows = slice(c * SGU_CHUNK, (c + 1) * SGU_CHUNK)
            sv_ref[rows, cols] = (
                jnp.dot(ws_c, vn[rows, cols], preferred_element_type=F32) + bs_ref[:, cols])
    y = _rms_norm(u * sv_ref[...], og_ref[...])
    o_ref[...] = (y * _silu(zg_ref[...])).astype(o_ref.dtype)


def _sgu(z_sgu, ln_g, ln_b, ws, bs_full, out_g):
    m = z_sgu.shape[0]
    tb = SGU_ROWS
    vec = pl.BlockSpec((1, D_SGU), lambda i: (0, 0))
    return pl.pallas_call(
        _sgu_kernel,
        grid=(m // tb,),
        in_specs=[
            pl.BlockSpec((tb, D_SGU), lambda i: (i, 0)),
            pl.BlockSpec((tb, D_SGU), lambda i: (i, 1)),
            pl.BlockSpec((tb, D_SGU), lambda i: (i, 2)),
            vec, vec,
            pl.BlockSpec((SGU_HEADS, SGU_CHUNK, SGU_CHUNK), lambda i: (0, 0, 0)),
            pl.BlockSpec((SGU_CHUNK, D_SGU), lambda i: (0, 0)),
            vec,
        ],
        out_specs=pl.BlockSpec((tb, D_SGU), lambda i: (i, 0)),
        out_shape=jax.ShapeDtypeStruct((m, D_SGU), BF16),
        scratch_shapes=[pltpu.VMEM((tb, D_SGU), F32)],
        compiler_params=pltpu.CompilerParams(
            dimension_semantics=("arbitrary",), vmem_limit_bytes=VMEM_LIMIT_BYTES),
        name="sgu",
    )(z_sgu, z_sgu, z_sgu, ln_g, ln_b, ws, bs_full, out_g)


def _tri_inverse(n_mat, row, col):
    eye = (row == col).astype(F32)
    level0 = jnp.where((row >> 1) == (col >> 1), jnp.where((row & 1) == 1, n_mat, 0.0), 0.0)
    x = eye + jnp.where((col & 1) == 0, level0, 0.0)
    s = 2
    while s < RWKV_CHUNK:
        shift = s.bit_length()
        same = (row >> shift) == (col >> shift)
        low_left = jnp.where((row & s) != 0, jnp.where((col & s) == 0, n_mat, 0.0), 0.0)
        n_s = jnp.where(same, low_left, 0.0)
        x = x + _dot(_dot(x, n_s), x)
        s *= 2
    return x


def _rwkv_kernel(z_ref, mu_ref, w0_ref, w2_ref, a0_ref, a2_ref, kk_ref, ka_ref, rk_ref,
                 gng_ref, gnb_ref, o_ref, state_ref, prev_ref):
    C = RWKV_CHUNK
    N = RWKV_HEAD_DIM

    @pl.when(pl.program_id(1) == 0)
    def _():
        state_ref[...] = jnp.zeros_like(state_ref)
        prev_ref[...] = jnp.zeros_like(prev_ref)

    z = z_ref[...]
    row1 = lax.broadcasted_iota(jnp.int32, (C, 1), 0)
    z_prev = jnp.where(row1 == 0, prev_ref[...], pltpu.roll(z, 1, axis=0))
    prev_ref[...] = z[C - 1:C, :]
    zs = z + (z_prev - z) * mu_ref[...]

    r = zs[:, 0:D_RWKV]
    k = zs[:, D_RWKV:2 * D_RWKV]
    v = zs[:, 2 * D_RWKV:3 * D_RWKV]
    gate = zs[:, 3 * D_RWKV:4 * D_RWKV]
    wd = zs[:, 4 * D_RWKV:4 * D_RWKV + LORA]
    ad = zs[:, 4 * D_RWKV + LORA:4 * D_RWKV + 2 * LORA]

    lw = w0_ref[...] + _dot(jnp.tanh(wd), w2_ref[...])
    neg = -lw
    softplus = jnp.maximum(neg, 0.0) + jnp.log1p(jnp.exp(-jnp.abs(neg)))
    ld = -jnp.exp(-softplus - 0.5)
    icl = jax.nn.sigmoid(a0_ref[...] + _dot(ad, a2_ref[...]))
    kk_full = k * kk_ref[...]
    k2 = k * (1.0 + (icl - 1.0) * ka_ref[...])

    row = lax.broadcasted_iota(jnp.int32, (C, C), 0)
    col = lax.broadcasted_iota(jnp.int32, (C, C), 1)
    incl = col <= row
    strict = col < row

    tril = incl.astype(F32).astype(BF16)
    p1 = ld.astype(BF16)
    r1 = ld - p1.astype(F32)
    p2 = r1.astype(BF16)
    p3 = (r1 - p2.astype(F32)).astype(BF16)
    cum = (jnp.dot(tril, p1, preferred_element_type=F32)
           + jnp.dot(tril, p2, preferred_element_type=F32)
           + jnp.dot(tril, p3, preferred_element_type=F32))
    cum_last = cum[C - 1:C, :]
    e_incl = jnp.exp(cum)
    e_excl = jnp.exp(cum - ld)
    e_neg = jnp.exp(-cum)
    e_rest = jnp.exp(cum_last - cum)
    g_last = jnp.exp(cum_last)

    r_t = r * e_incl
    k_t = k2 * e_neg
    k_h = k2 * e_rest
    rk_prod = r * k2 * rk_ref[...]
    gate_act = _silu(gate)
    eye = row == col

    for h in range(RWKV_HEADS):
        sl = slice(h * N, (h + 1) * N)
        kk = kk_full[:, sl]
        kk = kk * lax.rsqrt(jnp.maximum(jnp.sum(kk * kk, axis=-1, keepdims=True), 1e-24))
        bvec = kk * icl[:, sl]
        a_t = -kk * e_excl[:, sl]
        b_t = bvec * e_neg[:, sl]
        b_h = bvec * e_rest[:, sl]
        v_h = v[:, sl]
        ar = jnp.concatenate([a_t, r_t[:, sl]], axis=0)
        s_b = _dot_nt(ar, b_t)
        s_k = _dot_nt(ar, k_t[:, sl])
        n_mat = jnp.where(strict, s_b[:C], 0.0)
        a_ak = jnp.where(strict, s_k[:C], 0.0)
        a_rb = jnp.where(incl, s_b[C:], 0.0)
        a_rk = jnp.where(incl, s_k[C:], 0.0)
        m_inv = _tri_inverse(n_mat, row, col)
        p_mat = _dot(m_inv, a_t)
        w_mat = _dot(m_inv, _dot(a_ak, v_h))
        phi = jnp.where(eye, g_last[:, sl], 0.0) + _dot_tn(b_h, p_mat)
        psi = _dot_tn(jnp.concatenate([b_h, k_h[:, sl]], axis=0),
                      jnp.concatenate([w_mat, v_h], axis=0))
        q_mat = r_t[:, sl] + _dot(a_rb, p_mat)
        z_mat = _dot(a_rb, w_mat) + _dot(a_rk, v_h)
        t_state = state_ref[h]
        y = _dot(q_mat, t_state) + z_mat
        state_ref[h] = _dot(phi, t_state) + psi

        mean = jnp.mean(y, axis=-1, keepdims=True)
        yc = y - mean
        var = jnp.mean(yc * yc, axis=-1, keepdims=True)
        yn = yc * lax.rsqrt(var + GN_EPS) * gng_ref[:, sl] + gnb_ref[:, sl]
        bonus = jnp.sum(rk_prod[:, sl], axis=-1, keepdims=True) * v_h
        o_ref[:, sl] = ((yn + bonus) * gate_act[:, sl]).astype(o_ref.dtype)


def _rwkv(z_rwkv, mu, w0, w2, a0, a2, k_k, k_a, r_k, gn_g, gn_b, batch, seq):
    C = RWKV_CHUNK
    z3 = z_rwkv.reshape(batch, seq, C_RWKV)
    vec = lambda n: pl.BlockSpec((1, n), lambda b, c: (0, 0))
    lora = pl.BlockSpec((LORA, D_RWKV), lambda b, c: (0, 0))
    out = pl.pallas_call(
        _rwkv_kernel,
        grid=(batch, seq // C),
        in_specs=[
            pl.BlockSpec((None, C, C_RWKV), lambda b, c: (b, c, 0)),
            vec(C_RWKV), vec(D_RWKV), lora, vec(D_RWKV), lora,
            vec(D_RWKV), vec(D_RWKV), vec(D_RWKV), vec(D_RWKV), vec(D_RWKV),
        ],
        out_specs=pl.BlockSpec((None, C, D_RWKV), lambda b, c: (b, c, 0)),
        out_shape=jax.ShapeDtypeStruct((batch, seq, D_RWKV), BF16),
        scratch_shapes=[
            pltpu.VMEM((RWKV_HEADS, RWKV_HEAD_DIM, RWKV_HEAD_DIM), F32),
            pltpu.VMEM((1, C_RWKV), F32),
        ],
        compiler_params=pltpu.CompilerParams(
            dimension_semantics=("arbitrary", "arbitrary"), vmem_limit_bytes=VMEM_LIMIT_BYTES),
        name="rwkv",
    )(z3, mu, w0, w2, a0, a2, k_k, k_a, r_k, gn_g, gn_b)
    return out.reshape(batch * seq, D_RWKV)


def _kv_proj_kernel(m_ref, g_ref, w_ref, k_ref, v_ref):
    mn = _rms_norm(m_ref[...], g_ref[...]).astype(BF16)
    k_ref[...] = jnp.dot(mn, w_ref[:, :D_MODEL], preferred_element_type=F32).astype(k_ref.dtype)
    v_ref[...] = jnp.dot(mn, w_ref[:, D_MODEL:], preferred_element_type=F32).astype(v_ref.dtype)


def _kv_proj(mem2, g, w_kv):
    m = mem2.shape[0]
    tm = MEM_LEN
    return pl.pallas_call(
        _kv_proj_kernel,
        grid=(m // tm,),
        in_specs=[
            pl.BlockSpec((tm, D_MODEL), lambda i: (i, 0)),
            pl.BlockSpec((1, D_MODEL), lambda i: (0, 0)),
            pl.BlockSpec((D_MODEL, 2 * D_MODEL), lambda i: (0, 0)),
        ],
        out_specs=[
            pl.BlockSpec((tm, D_MODEL), lambda i: (i, 0)),
            pl.BlockSpec((tm, D_MODEL), lambda i: (i, 0)),
        ],
        out_shape=[
            jax.ShapeDtypeStruct((m, D_MODEL), BF16),
            jax.ShapeDtypeStruct((m, D_MODEL), BF16),
        ],
        compiler_params=pltpu.CompilerParams(
            dimension_semantics=("arbitrary",), vmem_limit_bytes=VMEM_LIMIT_BYTES),
        name="kv_proj",
    )(mem2, g, w_kv)


def _out_attn_kernel(x_ref, ys_ref, yr_ref, wo1_ref, wo2_ref, gx_ref, wq_ref, k_ref, v_ref,
                     wo_ref, gf_ref, o_ref):
    h1 = (x_ref[...]
          + jnp.dot(ys_ref[...], wo1_ref[...], preferred_element_type=F32)
          + jnp.dot(yr_ref[...], wo2_ref[...], preferred_element_type=F32))
    hn = _rms_norm(h1, gx_ref[...]).astype(BF16)
    q = jnp.dot(hn, wq_ref[...], preferred_element_type=F32).astype(BF16)
    heads = []
    for h in range(XATTN_HEADS):
        sl = slice(h * XATTN_HEAD_DIM, (h + 1) * XATTN_HEAD_DIM)
        s = _dot_nt(q[:, sl], k_ref[:, sl]) * (XATTN_HEAD_DIM ** -0.5)
        s = s - jnp.max(s, axis=-1, keepdims=True)
        e = jnp.exp(s)
        p = e / jnp.sum(e, axis=-1, keepdims=True)
        heads.append(jnp.dot(p.astype(BF16), v_ref[:, sl], preferred_element_type=F32).astype(BF16))
    o = jnp.concatenate(heads, axis=-1)
    h2 = h1 + jnp.dot(o, wo_ref[...], preferred_element_type=F32)
    o_ref[...] = _rms_norm(h2, gf_ref[...])


def _out_attn(x2, y_sgu, y_rwkv, wo1, wo2, g_x, w_q, k_mem, v_mem, w_o, g_f, seq):
    m = x2.shape[0]
    tq = ATTN_ROWS
    per_batch = seq // tq
    row_blk = lambda n: pl.BlockSpec((tq, n), lambda i: (i, 0))
    full = lambda a, b: pl.BlockSpec((a, b), lambda i: (0, 0))
    mem_blk = pl.BlockSpec((MEM_LEN, D_MODEL), lambda i: (i // per_batch, 0))
    return pl.pallas_call(
        _out_attn_kernel,
        grid=(m // tq,),
        in_specs=[
            row_blk(D_MODEL), row_blk(D_SGU), row_blk(D_RWKV),
            full(D_SGU, D_MODEL), full(D_RWKV, D_MODEL), full(1, D_MODEL),
            full(D_MODEL, D_MODEL), mem_blk, mem_blk, full(D_MODEL, D_MODEL), full(1, D_MODEL),
        ],
        out_specs=row_blk(D_MODEL),
        out_shape=jax.ShapeDtypeStruct((m, D_MODEL), F32),
        compiler_params=pltpu.CompilerParams(
            dimension_semantics=("arbitrary",), vmem_limit_bytes=VMEM_LIMIT_BYTES),
        name="out_attn",
    )(x2, y_sgu, y_rwkv, wo1, wo2, g_x, w_q, k_mem, v_mem, w_o, g_f)


def kernel(x, mem, ln_mix_g, w_in, sgu_ln_g, sgu_ln_b, sgu_ws, sgu_bs, sgu_out_g, rw_mu, rw_w0, rw_w2, rw_a0, rw_a2, rw_k_k, rw_k_a, rw_r_k, rw_gn_g, rw_gn_b, w_out, ln_x_g, ln_mem_g, w_q, w_kv, w_o, ln_f_g):
    batch, seq, _ = x.shape
    assert w_in.shape[0] == 1, "the final norm is fused into the only layer's last call"
    row = lambda a: a.reshape(1, -1)
    h = x.reshape(batch * seq, D_MODEL)
    mem2 = mem.reshape(batch * MEM_LEN, D_MODEL)
    for l in range(1):
        z_sgu, z_rwkv = _in_proj(h, row(ln_mix_g[l]), w_in[l].astype(BF16))
        bs_full = jnp.repeat(sgu_bs[l].T, SGU_HEAD_DIM, axis=1)
        y_sgu = _sgu(z_sgu, row(sgu_ln_g[l]), row(sgu_ln_b[l]), sgu_ws[l], bs_full,
                     row(sgu_out_g[l]))
        y_rwkv = _rwkv(z_rwkv, row(rw_mu[l]), row(rw_w0[l]), rw_w2[l].astype(BF16),
                       row(rw_a0[l]), rw_a2[l].astype(BF16), row(rw_k_k[l]), row(rw_k_a[l]),
                       row(rw_r_k[l]), row(rw_gn_g[l]), row(rw_gn_b[l]), batch, seq)
        k_mem, v_mem = _kv_proj(mem2, row(ln_mem_g[l]), w_kv[l].astype(BF16))
        w_out_b = w_out[l].astype(BF16)
        h = _out_attn(h, y_sgu, y_rwkv, w_out_b[:D_SGU], w_out_b[D_SGU:], row(ln_x_g[l]),
                      w_q[l].astype(BF16), k_mem, v_mem, w_o[l].astype(BF16), row(ln_f_g), seq)
    return h.reshape(batch, seq, D_MODEL)
```

```python
import functools

import jax
import jax.numpy as jnp
from jax import lax
from jax.experimental import pallas as pl
from jax.experimental.pallas import tpu as pltpu

D_MODEL = 1024
MEM_LEN = 256
D_SGU = 512
SGU_HEADS = 4
SGU_HEAD_DIM = D_SGU // SGU_HEADS
SGU_CHUNK = 128
D_RWKV = 512
RWKV_HEAD_DIM = 64
RWKV_HEADS = D_RWKV // RWKV_HEAD_DIM
LORA = 64
C_SGU = 3 * D_SGU
C_RWKV = 4 * D_RWKV + 2 * LORA
XATTN_HEADS = 4
XATTN_HEAD_DIM = D_MODEL // XATTN_HEADS
RMS_EPS = 1e-6
LN_EPS = 1e-5
GN_EPS = 64e-5

RWKV_CHUNK = 64
RWKV_STEP_CHUNKS = 4
PAIR = 2 * RWKV_HEAD_DIM
RWKV_PAIRS = RWKV_HEADS // 2
IN_PROJ_ROWS = 512
SGU_ROWS = 256
ATTN_ROWS = 256
VMEM_LIMIT_BYTES = 48 * 1024 * 1024

F32 = jnp.float32
BF16 = jnp.bfloat16


def _dot(a, b):
    return jnp.dot(a.astype(BF16), b.astype(BF16), preferred_element_type=F32)


def _dot_nt(a, b):
    return lax.dot_general(a.astype(BF16), b.astype(BF16), (((1,), (1,)), ((), ())),
                           preferred_element_type=F32)


def _dot_tn(a, b):
    return lax.dot_general(a.astype(BF16), b.astype(BF16), (((0,), (0,)), ((), ())),
                           preferred_element_type=F32)


def _rms_norm(x, g):
    return x * lax.rsqrt(jnp.mean(x * x, axis=-1, keepdims=True) + RMS_EPS) * g


def _gelu(x):
    return 0.5 * x * (1.0 + lax.erf(x * (2.0 ** -0.5)))


def _silu(x):
    return x * jax.nn.sigmoid(x)


def _in_proj_kernel(x_ref, g_ref, w_ref, zs_ref, zr_ref):
    xn = _rms_norm(x_ref[...], g_ref[...]).astype(BF16)
    zs_ref[...] = jnp.dot(xn, w_ref[:, :C_SGU], preferred_element_type=F32)
    zr_ref[...] = jnp.dot(xn, w_ref[:, C_SGU:], preferred_element_type=F32)


def _in_proj(x2, g, w_in):
    m = x2.shape[0]
    tm = IN_PROJ_ROWS
    return pl.pallas_call(
        _in_proj_kernel,
        grid=(m // tm,),
        in_specs=[
            pl.BlockSpec((tm, D_MODEL), lambda i: (i, 0)),
            pl.BlockSpec((1, D_MODEL), lambda i: (0, 0)),
            pl.BlockSpec((D_MODEL, C_SGU + C_RWKV), lambda i: (0, 0)),
        ],
        out_specs=[
            pl.BlockSpec((tm, C_SGU), lambda i: (i, 0)),
            pl.BlockSpec((tm, C_RWKV), lambda i: (i, 0)),
        ],
        out_shape=[
            jax.ShapeDtypeStruct((m, C_SGU), F32),
            jax.ShapeDtypeStruct((m, C_RWKV), F32),
        ],
        compiler_params=pltpu.CompilerParams(
            dimension_semantics=("arbitrary",), vmem_limit_bytes=VMEM_LIMIT_BYTES),
        name="in_proj",
    )(x2, g, w_in)


def _sgu_kernel(zu_ref, zv_ref, zg_ref, lng_ref, lnb_ref, ws_ref, bs_ref, og_ref, o_ref, sv_ref):
    u = _gelu(zu_ref[...])
    v = _gelu(zv_ref[...])
    mu = jnp.mean(v, axis=-1, keepdims=True)
    vc = v - mu
    var = jnp.mean(vc * vc, axis=-1, keepdims=True)
    vn = (vc * lax.rsqrt(var + LN_EPS) * lng_ref[...] + lnb_ref[...]).astype(BF16)
    row = lax.broadcasted_iota(jnp.int32, (SGU_CHUNK, SGU_CHUNK), 0)
    col = lax.broadcasted_iota(jnp.int32, (SGU_CHUNK, SGU_CHUNK), 1)
    causal = col <= row
    for h in range(SGU_HEADS):
        cols = slice(h * SGU_HEAD_DIM, (h + 1) * SGU_HEAD_DIM)
        ws_c = jnp.where(causal, ws_ref[h], 0.0).astype(BF16)
        for c in range(SGU_ROWS // SGU_CHUNK):
            rows = slice(c * SGU_CHUNK, (c + 1) * SGU_CHUNK)
            sv_ref[rows, cols] = (
                jnp.dot(ws_c, vn[rows, cols], preferred_element_type=F32) + bs_ref[:, cols])
    y = _rms_norm(u * sv_ref[...], og_ref[...])
    o_ref[...] = (y * _silu(zg_ref[...])).astype(o_ref.dtype)


def _sgu(z_sgu, ln_g, ln_b, ws, bs_full, out_g):
    m = z_sgu.shape[0]
    tb = SGU_ROWS
    vec = pl.BlockSpec((1, D_SGU), lambda i: (0, 0))
    return pl.pallas_call(
        _sgu_kernel,
        grid=(m // tb,),
        in_specs=[
            pl.BlockSpec((tb, D_SGU), lambda i: (i, 0)),
            pl.BlockSpec((tb, D_SGU), lambda i: (i, 1)),
            pl.BlockSpec((tb, D_SGU), lambda i: (i, 2)),
            vec, vec,
            pl.BlockSpec((SGU_HEADS, SGU_CHUNK, SGU_CHUNK), lambda i: (0, 0, 0)),
            pl.BlockSpec((SGU_CHUNK, D_SGU), lambda i: (0, 0)),
            vec,
        ],
        out_specs=pl.BlockSpec((tb, D_SGU), lambda i: (i, 0)),
        out_shape=jax.ShapeDtypeStruct((m, D_SGU), BF16),
        scratch_shapes=[pltpu.VMEM((tb, D_SGU), F32)],
        compiler_params=pltpu.CompilerParams(
            dimension_semantics=("arbitrary",), vmem_limit_bytes=VMEM_LIMIT_BYTES),
        name="sgu",
    )(z_sgu, z_sgu, z_sgu, ln_g, ln_b, ws, bs_full, out_g)


def _block_diag(x, lane_a):
    top = jnp.where(lane_a, x, 0.0).astype(BF16)
    bot = jnp.where(lane_a, 0.0, x).astype(BF16)
    return jnp.concatenate([top, bot], axis=0)


def _head_sum(x, lane_a):
    outs = []
    for p in range(RWKV_PAIRS):
        xp = x[:, p * PAIR:(p + 1) * PAIR]
        sum_a = jnp.sum(jnp.where(lane_a, xp, 0.0), axis=-1, keepdims=True)
        sum_b = jnp.sum(jnp.where(lane_a, 0.0, xp), axis=-1, keepdims=True)
        outs.append(jnp.where(lane_a, sum_a, sum_b))
    return jnp.concatenate(outs, axis=-1)


def _rwkv_kernel(z_ref, mu_ref, w0_ref, w2_ref, a0_ref, a2_ref, kk_ref, ka_ref, rk_ref,
                 gng_ref, gnb_ref, o_ref, state_ref, prev_ref):
    C = RWKV_CHUNK
    G = RWKV_STEP_CHUNKS
    R = C * G
    N = RWKV_HEAD_DIM

    @pl.when(pl.program_id(1) == 0)
    def _():
        state_ref[...] = jnp.zeros_like(state_ref)
        prev_ref[...] = jnp.zeros_like(prev_ref)

    z = z_ref[...]
    row_r = lax.broadcasted_iota(jnp.int32, (R, 1), 0)
    z_prev = jnp.where(row_r == 0, prev_ref[...], pltpu.roll(z, 1, axis=0))
    prev_ref[...] = z[R - 1:R, :]
    zs = z + (z_prev - z) * mu_ref[...]

    r = zs[:, 0:D_RWKV]
    k = zs[:, D_RWKV:2 * D_RWKV]
    v = zs[:, 2 * D_RWKV:3 * D_RWKV]
    gate = zs[:, 3 * D_RWKV:4 * D_RWKV]
    wd = zs[:, 4 * D_RWKV:4 * D_RWKV + LORA]
    ad = zs[:, 4 * D_RWKV + LORA:4 * D_RWKV + 2 * LORA]

    lw = w0_ref[...] + _dot(jnp.tanh(wd), w2_ref[...])
    neg = -lw
    softplus = jnp.maximum(neg, 0.0) + jnp.log1p(jnp.exp(-jnp.abs(neg)))
    ld = -jnp.exp(-softplus - 0.5)
    icl = jax.nn.sigmoid(a0_ref[...] + _dot(ad, a2_ref[...]))
    k2 = k * (1.0 + (icl - 1.0) * ka_ref[...])

    lane_r = lax.broadcasted_iota(jnp.int32, (R, PAIR), 1) < N
    kk = k * kk_ref[...]
    kk = kk * lax.rsqrt(jnp.maximum(_head_sum(kk * kk, lane_r), 1e-24))
    bvec = kk * icl

    row = lax.broadcasted_iota(jnp.int32, (C, PAIR), 0)
    lane = lax.broadcasted_iota(jnp.int32, (C, PAIR), 1)
    col = lane & (N - 1)
    lane_a = lane < N
    incl = col <= row
    strict = col < row
    eye = col == row

    tril = incl[:, :C].astype(F32).astype(BF16)
    p1 = ld.astype(BF16)
    r1 = ld - p1.astype(F32)
    p2 = r1.astype(BF16)
    p3 = (r1 - p2.astype(F32)).astype(BF16)
    cums, lasts = [], []
    for c in range(G):
        rows = slice(c * C, (c + 1) * C)
        cum_c = (jnp.dot(tril, p1[rows], preferred_element_type=F32)
                 + jnp.dot(tril, p2[rows], preferred_element_type=F32)
                 + jnp.dot(tril, p3[rows], preferred_element_type=F32))
        cums.append(cum_c)
        lasts.append(cum_c[C - 1:C, :])
    cum = jnp.concatenate(cums, axis=0)
    cum_last = jnp.concatenate([jnp.broadcast_to(l, (C, D_RWKV)) for l in lasts], axis=0)
    e_neg = jnp.exp(-cum)
    e_rest = jnp.exp(cum_last - cum)
    a_t = -kk * jnp.exp(cum - ld)
    r_t = r * jnp.exp(cum)
    b_t = bvec * e_neg
    k_t = k2 * e_neg
    b_h = bvec * e_rest
    k_h = k2 * e_rest

    units = [(c, p) for c in range(G) for p in range(RWKV_PAIRS)]
    sl = lambda x, u: x[u[0] * C:(u[0] + 1) * C, u[1] * PAIR:(u[1] + 1) * PAIR]
    dot_bd = lambda lhs, rhs: jnp.dot(lhs.astype(BF16), _block_diag(rhs, lane_a),
                                      preferred_element_type=F32)

    lhs_ar = [jnp.concatenate([sl(a_t, u), sl(r_t, u)], axis=0) for u in units]
    s_b = [_dot_nt(lhs_ar[i], _block_diag(sl(b_t, u), lane_a)) for i, u in enumerate(units)]
    s_k = [_dot_nt(lhs_ar[i], _block_diag(sl(k_t, u), lane_a)) for i, u in enumerate(units)]
    n_mat = [jnp.where(strict, s[:C], 0.0) for s in s_b]
    a_ak = [jnp.where(strict, s[:C], 0.0) for s in s_k]
    a_rb = [jnp.where(incl, s[C:], 0.0) for s in s_b]
    a_rk = [jnp.where(incl, s[C:], 0.0) for s in s_k]

    lvl = ((row >> 1) == (col >> 1)) & ((row & 1) == 1) & ((col & 1) == 0)
    eye_f = eye.astype(F32)
    x = [eye_f + jnp.where(lvl, n, 0.0) for n in n_mat]
    s = 2
    while s < C:
        shift = s.bit_length()
        lvl = ((row >> shift) == (col >> shift)) & ((row & s) != 0) & ((col & s) == 0)
        xn = [dot_bd(x[i], jnp.where(lvl, n_mat[i], 0.0)) for i in range(len(units))]
        x = [x[i] + dot_bd(xn[i], x[i]) for i in range(len(units))]
        s *= 2

    av = [dot_bd(a_ak[i], sl(v, u)) for i, u in enumerate(units)]
    p_mat = [dot_bd(x[i], sl(a_t, u)) for i, u in enumerate(units)]

    ones_bd = (lax.broadcasted_iota(jnp.int32, (PAIR, PAIR), 0) < N) == (
        lax.broadcasted_iota(jnp.int32, (PAIR, PAIR), 1) < N)
    ones_bd = ones_bd.astype(F32).astype(BF16)
    g_col = []
    for u in units:
        g_last = jnp.exp(lasts[u[0]][:, u[1] * PAIR:(u[1] + 1) * PAIR])
        d0 = jnp.where(eye, g_last, 0.0)
        d1 = d0.astype(BF16)
        e1 = d0 - d1.astype(F32)
        d2 = e1.astype(BF16)
        d3 = (e1 - d2.astype(F32)).astype(BF16)
        g3 = _dot_nt(jnp.concatenate([d1, d2, d3], axis=0), ones_bd)
        g_col.append(g3[:C] + g3[C:2 * C] + g3[2 * C:])

    t_state = [state_ref[:, p * PAIR:(p + 1) * PAIR] for p in range(RWKV_PAIRS)]
    y_rows = []
    for c in range(G):
        idx = [c * RWKV_PAIRS + p for p in range(RWKV_PAIRS)]
        t_bd = [_block_diag(t_state[p], lane_a) for p in range(RWKV_PAIRS)]
        u_mat = [jnp.dot(jnp.concatenate([p_mat[i], x[i]], axis=1).astype(BF16),
                         jnp.concatenate([t_bd[p], _block_diag(av[i], lane_a)], axis=0),
                         preferred_element_type=F32)
                 for p, i in enumerate(idx)]
        y_rows.append(jnp.concatenate(
            [jnp.dot(jnp.concatenate([sl(r_t, units[i]), a_rb[i], a_rk[i]], axis=1).astype(BF16),
                     jnp.concatenate([t_bd[p], _block_diag(u_mat[p], lane_a),
                                      _block_diag(sl(v, units[i]), lane_a)], axis=0),
                     preferred_element_type=F32)
             for p, i in enumerate(idx)], axis=1))
        full = [_dot_tn(jnp.concatenate([sl(b_h, units[i]), sl(k_h, units[i])], axis=0),
                        jnp.concatenate([u_mat[p], sl(v, units[i])], axis=0))
                for p, i in enumerate(idx)]
        t_state = [g_col[i] * t_state[p] + jnp.where(lane_a, full[p][:N], full[p][N:])
                   for p, i in enumerate(idx)]
    for p in range(RWKV_PAIRS):
        state_ref[:, p * PAIR:(p + 1) * PAIR] = t_state[p]

    y = jnp.concatenate(y_rows, axis=0)
    yc = y - _head_sum(y, lane_r) * (1.0 / N)
    var = _head_sum(yc * yc, lane_r) * (1.0 / N)
    yn = yc * lax.rsqrt(var + GN_EPS) * gng_ref[...] + gnb_ref[...]
    bonus = _head_sum(r * k2 * rk_ref[...], lane_r) * v
    o_ref[...] = ((yn + bonus) * _silu(gate)).astype(o_ref.dtype)


def _rwkv(z_rwkv, mu, w0, w2, a0, a2, k_k, k_a, r_k, gn_g, gn_b, batch, seq):
    rows = RWKV_CHUNK * RWKV_STEP_CHUNKS
    z3 = z_rwkv.reshape(batch, seq, C_RWKV)
    vec = lambda n: pl.BlockSpec((1, n), lambda b, c: (0, 0))
    lora = pl.BlockSpec((LORA, D_RWKV), lambda b, c: (0, 0))
    out = pl.pallas_call(
        _rwkv_kernel,
        grid=(batch, seq // rows),
        in_specs=[
            pl.BlockSpec((None, rows, C_RWKV), lambda b, c: (b, c, 0)),
            vec(C_RWKV), vec(D_RWKV), lora, vec(D_RWKV), lora,
            vec(D_RWKV), vec(D_RWKV), vec(D_RWKV), vec(D_RWKV), vec(D_RWKV),
        ],
        out_specs=pl.BlockSpec((None, rows, D_RWKV), lambda b, c: (b, c, 0)),
        out_shape=jax.ShapeDtypeStruct((batch, seq, D_RWKV), BF16),
        scratch_shapes=[
            pltpu.VMEM((RWKV_HEAD_DIM, D_RWKV), F32),
            pltpu.VMEM((1, C_RWKV), F32),
        ],
        compiler_params=pltpu.CompilerParams(
            dimension_semantics=("arbitrary", "arbitrary"), vmem_limit_bytes=VMEM_LIMIT_BYTES),
        name="rwkv",
    )(z3, mu, w0, w2, a0, a2, k_k, k_a, r_k, gn_g, gn_b)
    return out.reshape(batch * seq, D_RWKV)


def _kv_proj_kernel(m_ref, g_ref, w_ref, k_ref, v_ref):
    mn = _rms_norm(m_ref[...], g_ref[...]).astype(BF16)
    k_ref[...] = jnp.dot(mn, w_ref[:, :D_MODEL], preferred_element_type=F32).astype(k_ref.dtype)
    v_ref[...] = jnp.dot(mn, w_ref[:, D_MODEL:], preferred_element_type=F32).astype(v_ref.dtype)


def _kv_proj(mem2, g, w_kv):
    m = mem2.shape[0]
    tm = MEM_LEN
    return pl.pallas_call(
        _kv_proj_kernel,
        grid=(m // tm,),
        in_specs=[
            pl.BlockSpec((tm, D_MODEL), lambda i: (i, 0)),
            pl.BlockSpec((1, D_MODEL), lambda i: (0, 0)),
            pl.BlockSpec((D_MODEL, 2 * D_MODEL), lambda i: (0, 0)),
        ],
        out_specs=[
            pl.BlockSpec((tm, D_MODEL), lambda i: (i, 0)),
            pl.BlockSpec((tm, D_MODEL), lambda i: (i, 0)),
        ],
        out_shape=[
            jax.ShapeDtypeStruct((m, D_MODEL), BF16),
            jax.ShapeDtypeStruct((m, D_MODEL), BF16),
        ],
        compiler_params=pltpu.CompilerParams(
            dimension_semantics=("arbitrary",), vmem_limit_bytes=VMEM_LIMIT_BYTES),
        name="kv_proj",
    )(mem2, g, w_kv)


def _out_attn_kernel(x_ref, ys_ref, yr_ref, wo1_ref, wo2_ref, gx_ref, wq_ref, k_ref, v_ref,
                     wo_ref, gf_ref, o_ref):
    h1 = (x_ref[...]
          + jnp.dot(ys_ref[...], wo1_ref[...], preferred_element_type=F32)
          + jnp.dot(yr_ref[...], wo2_ref[...], preferred_element_type=F32))
    hn = _rms_norm(h1, gx_ref[...]).astype(BF16)
    q = jnp.dot(hn, wq_ref[...], preferred_element_type=F32).astype(BF16)
    heads = []
    for h in range(XATTN_HEADS):
        sl = slice(h * XATTN_HEAD_DIM, (h + 1) * XATTN_HEAD_DIM)
        s = _dot_nt(q[:, sl], k_ref[:, sl]) * (XATTN_HEAD_DIM ** -0.5)
        s = s - jnp.max(s, axis=-1, keepdims=True)
        e = jnp.exp(s)
        p = e / jnp.sum(e, axis=-1, keepdims=True)
        heads.append(jnp.dot(p.astype(BF16), v_ref[:, sl], preferred_element_type=F32).astype(BF16))
    o = jnp.concatenate(heads, axis=-1)
    h2 = h1 + jnp.dot(o, wo_ref[...], preferred_element_type=F32)
    o_ref[...] = _rms_norm(h2, gf_ref[...])


def _out_attn(x2, y_sgu, y_rwkv, wo1, wo2, g_x, w_q, k_mem, v_mem, w_o, g_f, seq):
    m = x2.shape[0]
    tq = ATTN_ROWS
    per_batch = seq // tq
    row_blk = lambda n: pl.BlockSpec((tq, n), lambda i: (i, 0))
    full = lambda a, b: pl.BlockSpec((a, b), lambda i: (0, 0))
    mem_blk = pl.BlockSpec((MEM_LEN, D_MODEL), lambda i: (i // per_batch, 0))
    return pl.pallas_call(
        _out_attn_kernel,
        grid=(m // tq,),
        in_specs=[
            row_blk(D_MODEL), row_blk(D_SGU), row_blk(D_RWKV),
            full(D_SGU, D_MODEL), full(D_RWKV, D_MODEL), full(1, D_MODEL),
            full(D_MODEL, D_MODEL), mem_blk, mem_blk, full(D_MODEL, D_MODEL), full(1, D_MODEL),
        ],
        out_specs=row_blk(D_MODEL),
        out_shape=jax.ShapeDtypeStruct((m, D_MODEL), F32),
        compiler_params=pltpu.CompilerParams(
            dimension_semantics=("arbitrary",), vmem_limit_bytes=VMEM_LIMIT_BYTES),
        name="out_attn",
    )(x2, y_sgu, y_rwkv, wo1, wo2, g_x, w_q, k_mem, v_mem, w_o, g_f)


def kernel(x, mem, ln_mix_g, w_in, sgu_ln_g, sgu_ln_b, sgu_ws, sgu_bs, sgu_out_g, rw_mu, rw_w0, rw_w2, rw_a0, rw_a2, rw_k_k, rw_k_a, rw_r_k, rw_gn_g, rw_gn_b, w_out, ln_x_g, ln_mem_g, w_q, w_kv, w_o, ln_f_g):
    batch, seq, _ = x.shape
    assert w_in.shape[0] == 1, "the final norm is fused into the only layer's last call"
    row = lambda a: a.reshape(1, -1)
    h = x.reshape(batch * seq, D_MODEL)
    mem2 = mem.reshape(batch * MEM_LEN, D_MODEL)
    for l in range(1):
        z_sgu, z_rwkv = _in_proj(h, row(ln_mix_g[l]), w_in[l].astype(BF16))
        bs_full = jnp.repeat(sgu_bs[l].T, SGU_HEAD_DIM, axis=1)
        y_sgu = _sgu(z_sgu, row(sgu_ln_g[l]), row(sgu_ln_b[l]), sgu_ws[l], bs_full,
                     row(sgu_out_g[l]))
        y_rwkv = _rwkv(z_rwkv, row(rw_mu[l]), row(rw_w0[l]), rw_w2[l].astype(BF16),
                       row(rw_a0[l]), rw_a2[l].astype(BF16), row(rw_k_k[l]), row(rw_k_a[l]),
                       row(rw_r_k[l]), row(rw_gn_g[l]), row(rw_gn_b[l]), batch, seq)
        k_mem, v_mem = _kv_proj(mem2, row(ln_mem_g[l]), w_kv[l].astype(BF16))
        w_out_b = w_out[l].astype(BF16)
        h = _out_attn(h, y_sgu, y_rwkv, w_out_b[:D_SGU], w_out_b[D_SGU:], row(ln_x_g[l]),
                      w_q[l].astype(BF16), k_mem, v_mem, w_o[l].astype(BF16), row(ln_f_g), seq)
    return h.reshape(batch, seq, D_MODEL)
```

```python
import math

import jax
import jax.numpy as jnp
from jax import lax
from jax.experimental import pallas as pl
from jax.experimental.pallas import tpu as pltpu

D_MODEL = 1024
MEM_LEN = 256
D_SGU = 512
SGU_HEADS = 4
SGU_HEAD_DIM = D_SGU // SGU_HEADS
SGU_CHUNK = 128
D_RWKV = 512
RWKV_HEAD_DIM = 64
RWKV_HEADS = D_RWKV // RWKV_HEAD_DIM
LORA = 64
C_SGU = 3 * D_SGU
C_RWKV = 4 * D_RWKV + 2 * LORA
XATTN_HEADS = 4
XATTN_HEAD_DIM = D_MODEL // XATTN_HEADS
RMS_EPS = 1e-6
LN_EPS = 1e-5
GN_EPS = 64e-5

RWKV_CHUNK = 64
RWKV_STEP_CHUNKS = 4
PAIR = 2 * RWKV_HEAD_DIM
RWKV_PAIRS = RWKV_HEADS // 2
IN_PROJ_ROWS = 512
SGU_ROWS = 512
ATTN_ROWS = 512
VMEM_LIMIT_BYTES = 48 * 1024 * 1024

F32 = jnp.float32
BF16 = jnp.bfloat16


def _dot(a, b):
    return jnp.dot(a.astype(BF16), b.astype(BF16), preferred_element_type=F32)


def _dot_nt(a, b):
    return lax.dot_general(a.astype(BF16), b.astype(BF16), (((1,), (1,)), ((), ())),
                           preferred_element_type=F32)


def _dot_tn(a, b):
    return lax.dot_general(a.astype(BF16), b.astype(BF16), (((0,), (0,)), ((), ())),
                           preferred_element_type=F32)


def _rms_norm(x, g):
    return x * lax.rsqrt(jnp.mean(x * x, axis=-1, keepdims=True) + RMS_EPS) * g


def _gelu(x):
    return 0.5 * x * (1.0 + lax.erf(x * (2.0 ** -0.5)))


def _silu(x):
    return x * jax.nn.sigmoid(x)


def _in_proj_kernel(x_ref, g_ref, w_ref, zs_ref, zr_ref):
    xn = _rms_norm(x_ref[...], g_ref[...]).astype(BF16)
    zs_ref[...] = jnp.dot(xn, w_ref[:, :C_SGU], preferred_element_type=F32)
    zr_ref[...] = jnp.dot(xn, w_ref[:, C_SGU:], preferred_element_type=F32)


def _in_proj(x2, g, w_in):
    m = x2.shape[0]
    tm = IN_PROJ_ROWS
    return pl.pallas_call(
        _in_proj_kernel,
        grid=(m // tm,),
        in_specs=[
            pl.BlockSpec((tm, D_MODEL), lambda i: (i, 0)),
            pl.BlockSpec((1, D_MODEL), lambda i: (0, 0)),
            pl.BlockSpec((D_MODEL, C_SGU + C_RWKV), lambda i: (0, 0)),
        ],
        out_specs=[
            pl.BlockSpec((tm, C_SGU), lambda i: (i, 0)),
            pl.BlockSpec((tm, C_RWKV), lambda i: (i, 0)),
        ],
        out_shape=[
            jax.ShapeDtypeStruct((m, C_SGU), F32),
            jax.ShapeDtypeStruct((m, C_RWKV), F32),
        ],
        compiler_params=pltpu.CompilerParams(
            dimension_semantics=("arbitrary",), vmem_limit_bytes=VMEM_LIMIT_BYTES),
        name="in_proj",
    )(x2, g, w_in)


def _sgu_kernel(zu_ref, zv_ref, zg_ref, lng_ref, lnb_ref, ws_ref, bs_ref, og_ref, o_ref, sv_ref):
    u = _gelu(zu_ref[...])
    v = _gelu(zv_ref[...])
    mu = jnp.mean(v, axis=-1, keepdims=True)
    vc = v - mu
    var = jnp.mean(vc * vc, axis=-1, keepdims=True)
    vn = (vc * lax.rsqrt(var + LN_EPS) * lng_ref[...] + lnb_ref[...]).astype(BF16)
    row = lax.broadcasted_iota(jnp.int32, (SGU_CHUNK, SGU_CHUNK), 0)
    col = lax.broadcasted_iota(jnp.int32, (SGU_CHUNK, SGU_CHUNK), 1)
    causal = col <= row
    for h in range(SGU_HEADS):
        cols = slice(h * SGU_HEAD_DIM, (h + 1) * SGU_HEAD_DIM)
        ws_c = jnp.where(causal, ws_ref[h], 0.0).astype(BF16)
        for c in range(SGU_ROWS // SGU_CHUNK):
            rows = slice(c * SGU_CHUNK, (c + 1) * SGU_CHUNK)
            sv_ref[rows, cols] = (
                jnp.dot(ws_c, vn[rows, cols], preferred_element_type=F32) + bs_ref[:, cols])
    y = _rms_norm(u * sv_ref[...], og_ref[...])
    o_ref[...] = (y * _silu(zg_ref[...])).astype(o_ref.dtype)


def _sgu(z_sgu, ln_g, ln_b, ws, bs_full, out_g):
    m = z_sgu.shape[0]
    tb = SGU_ROWS
    vec = pl.BlockSpec((1, D_SGU), lambda i: (0, 0))
    return pl.pallas_call(
        _sgu_kernel,
        grid=(m // tb,),
        in_specs=[
            pl.BlockSpec((tb, D_SGU), lambda i: (i, 0)),
            pl.BlockSpec((tb, D_SGU), lambda i: (i, 1)),
            pl.BlockSpec((tb, D_SGU), lambda i: (i, 2)),
            vec, vec,
            pl.BlockSpec((SGU_HEADS, SGU_CHUNK, SGU_CHUNK), lambda i: (0, 0, 0)),
            pl.BlockSpec((SGU_CHUNK, D_SGU), lambda i: (0, 0)),
            vec,
        ],
        out_specs=pl.BlockSpec((tb, D_SGU), lambda i: (i, 0)),
        out_shape=jax.ShapeDtypeStruct((m, D_SGU), BF16),
        scratch_shapes=[pltpu.VMEM((tb, D_SGU), F32)],
        compiler_params=pltpu.CompilerParams(
            dimension_semantics=("arbitrary",), vmem_limit_bytes=VMEM_LIMIT_BYTES),
        name="sgu",
    )(z_sgu, z_sgu, z_sgu, ln_g, ln_b, ws, bs_full, out_g)


def _block_diag(x, lane_a):
    top = jnp.where(lane_a, x, 0.0).astype(BF16)
    bot = jnp.where(lane_a, 0.0, x).astype(BF16)
    return jnp.concatenate([top, bot], axis=0)


def _head_sum(x, lane_a):
    outs = []
    for p in range(RWKV_PAIRS):
        xp = x[:, p * PAIR:(p + 1) * PAIR]
        sum_a = jnp.sum(jnp.where(lane_a, xp, 0.0), axis=-1, keepdims=True)
        sum_b = jnp.sum(jnp.where(lane_a, 0.0, xp), axis=-1, keepdims=True)
        outs.append(jnp.where(lane_a, sum_a, sum_b))
    return jnp.concatenate(outs, axis=-1)


def _rwkv_kernel(z_ref, mu_ref, w0_ref, w2_ref, a0_ref, a2_ref, kk_ref, ka_ref, rk_ref,
                 gng_ref, gnb_ref, o_ref, state_ref, prev_ref):
    C = RWKV_CHUNK
    G = RWKV_STEP_CHUNKS
    R = C * G
    N = RWKV_HEAD_DIM

    @pl.when(pl.program_id(1) == 0)
    def _():
        state_ref[...] = jnp.zeros_like(state_ref)
        prev_ref[...] = jnp.zeros_like(prev_ref)

    z = z_ref[...]
    row_r = lax.broadcasted_iota(jnp.int32, (R, 1), 0)
    z_prev = jnp.where(row_r == 0, prev_ref[...], pltpu.roll(z, 1, axis=0))
    prev_ref[...] = z[R - 1:R, :]
    zs = z + (z_prev - z) * mu_ref[...]

    r = zs[:, 0:D_RWKV]
    k = zs[:, D_RWKV:2 * D_RWKV]
    v = zs[:, 2 * D_RWKV:3 * D_RWKV]
    gate = zs[:, 3 * D_RWKV:4 * D_RWKV]
    wd = zs[:, 4 * D_RWKV:4 * D_RWKV + LORA]
    ad = zs[:, 4 * D_RWKV + LORA:4 * D_RWKV + 2 * LORA]

    lw = w0_ref[...] + _dot(jnp.tanh(wd), w2_ref[...])
    ld = -math.exp(-0.5) * jax.nn.sigmoid(lw)
    icl = jax.nn.sigmoid(a0_ref[...] + _dot(ad, a2_ref[...]))
    k2 = k * (1.0 + (icl - 1.0) * ka_ref[...])

    lane_r = lax.broadcasted_iota(jnp.int32, (R, PAIR), 1) < N
    kk = k * kk_ref[...]
    kk = kk * lax.rsqrt(jnp.maximum(_head_sum(kk * kk, lane_r), 1e-24))
    bvec = kk * icl

    row = lax.broadcasted_iota(jnp.int32, (C, PAIR), 0)
    lane = lax.broadcasted_iota(jnp.int32, (C, PAIR), 1)
    col = lane & (N - 1)
    lane_a = lane < N
    incl = col <= row
    strict = col < row
    eye = col == row

    tril = incl[:, :C].astype(F32).astype(BF16)
    tril3 = jnp.concatenate([tril, tril, tril], axis=1)
    p1 = ld.astype(BF16)
    r1 = ld - p1.astype(F32)
    p2 = r1.astype(BF16)
    p3 = (r1 - p2.astype(F32)).astype(BF16)
    cums, lasts = [], []
    for c in range(G):
        rows = slice(c * C, (c + 1) * C)
        cum_c = jnp.dot(tril3, jnp.concatenate([p1[rows], p2[rows], p3[rows]], axis=0),
                        preferred_element_type=F32)
        cums.append(cum_c)
        lasts.append(cum_c[C - 1:C, :])
    cum = jnp.concatenate(cums, axis=0)
    cum_last = jnp.concatenate([jnp.broadcast_to(l, (C, D_RWKV)) for l in lasts], axis=0)
    e_neg = jnp.exp(-cum)
    e_rest = jnp.exp(cum_last - cum)
    a_t = -kk * jnp.exp(cum - ld)
    r_t = r * jnp.exp(cum)
    b_t = bvec * e_neg
    k_t = k2 * e_neg
    b_h = bvec * e_rest
    k_h = k2 * e_rest

    units = [(c, p) for c in range(G) for p in range(RWKV_PAIRS)]
    sl = lambda x, u: x[u[0] * C:(u[0] + 1) * C, u[1] * PAIR:(u[1] + 1) * PAIR]
    dot_bd = lambda lhs, rhs: jnp.dot(lhs.astype(BF16), _block_diag(rhs, lane_a),
                                      preferred_element_type=F32)

    lhs_ar = [jnp.concatenate([sl(a_t, u), sl(r_t, u)], axis=0) for u in units]
    s_b = [_dot_nt(lhs_ar[i], _block_diag(sl(b_t, u), lane_a)) for i, u in enumerate(units)]
    s_k = [_dot_nt(lhs_ar[i], _block_diag(sl(k_t, u), lane_a)) for i, u in enumerate(units)]
    n_mat = [jnp.where(strict, s[:C], 0.0) for s in s_b]
    a_ak = [jnp.where(strict, s[:C], 0.0) for s in s_k]
    a_rb = [jnp.where(incl, s[C:], 0.0) for s in s_b]
    a_rk = [jnp.where(incl, s[C:], 0.0) for s in s_k]

    lvl = ((row >> 1) == (col >> 1)) & ((row & 1) == 1) & ((col & 1) == 0)
    eye_f = eye.astype(F32)
    x = [eye_f + jnp.where(lvl, n, 0.0) for n in n_mat]
    s = 2
    while s < C:
        shift = s.bit_length()
        lvl = ((row >> shift) == (col >> shift)) & ((row & s) != 0) & ((col & s) == 0)
        xn = [dot_bd(x[i], jnp.where(lvl, n_mat[i], 0.0)) for i in range(len(units))]
        x = [x[i] + dot_bd(xn[i], x[i]) for i in range(len(units))]
        s *= 2

    av = [dot_bd(a_ak[i], sl(v, u)) for i, u in enumerate(units)]
    p_mat = [dot_bd(x[i], sl(a_t, u)) for i, u in enumerate(units)]

    eye_full = jnp.concatenate([eye] * RWKV_PAIRS, axis=1)
    g_cols = [_head_sum(jnp.where(eye_full, jnp.exp(l), 0.0), lane_a) for l in lasts]
    g_col = [g_cols[c][:, p * PAIR:(p + 1) * PAIR] for c, p in units]

    t_state = [state_ref[:, p * PAIR:(p + 1) * PAIR] for p in range(RWKV_PAIRS)]
    y_rows = []
    for c in range(G):
        idx = [c * RWKV_PAIRS + p for p in range(RWKV_PAIRS)]
        t_bd = [_block_diag(t_state[p], lane_a) for p in range(RWKV_PAIRS)]
        u_mat = [jnp.dot(jnp.concatenate([p_mat[i], x[i]], axis=1).astype(BF16),
                         jnp.concatenate([t_bd[p], _block_diag(av[i], lane_a)], axis=0),
                         preferred_element_type=F32)
                 for p, i in enumerate(idx)]
        y_rows.append(jnp.concatenate(
            [jnp.dot(jnp.concatenate([sl(r_t, units[i]), a_rb[i], a_rk[i]], axis=1).astype(BF16),
                     jnp.concatenate([t_bd[p], _block_diag(u_mat[p], lane_a),
                                      _block_diag(sl(v, units[i]), lane_a)], axis=0),
                     preferred_element_type=F32)
             for p, i in enumerate(idx)], axis=1))
        full = [_dot_tn(jnp.concatenate([sl(b_h, units[i]), sl(k_h, units[i])], axis=0),
                        jnp.concatenate([u_mat[p], sl(v, units[i])], axis=0))
                for p, i in enumerate(idx)]
        t_state = [g_col[i] * t_state[p] + jnp.where(lane_a, full[p][:N], full[p][N:])
                   for p, i in enumerate(idx)]
    for p in range(RWKV_PAIRS):
        state_ref[:, p * PAIR:(p + 1) * PAIR] = t_state[p]

    y = jnp.concatenate(y_rows, axis=0)
    yc = y - _head_sum(y, lane_r) * (1.0 / N)
    var = _head_sum(yc * yc, lane_r) * (1.0 / N)
    yn = yc * lax.rsqrt(var + GN_EPS) * gng_ref[...] + gnb_ref[...]
    bonus = _head_sum(r * k2 * rk_ref[...], lane_r) * v
    o_ref[...] = ((yn + bonus) * _silu(gate)).astype(o_ref.dtype)


def _rwkv(z_rwkv, mu, w0, w2, a0, a2, k_k, k_a, r_k, gn_g, gn_b, batch, seq):
    rows = RWKV_CHUNK * RWKV_STEP_CHUNKS
    z3 = z_rwkv.reshape(batch, seq, C_RWKV)
    vec = lambda n: pl.BlockSpec((1, n), lambda b, c: (0, 0))
    lora = pl.BlockSpec((LORA, D_RWKV), lambda b, c: (0, 0))
    out = pl.pallas_call(
        _rwkv_kernel,
        grid=(batch, seq // rows),
        in_specs=[
            pl.BlockSpec((None, rows, C_RWKV), lambda b, c: (b, c, 0)),
            vec(C_RWKV), vec(D_RWKV), lora, vec(D_RWKV), lora,
            vec(D_RWKV), vec(D_RWKV), vec(D_RWKV), vec(D_RWKV), vec(D_RWKV),
        ],
        out_specs=pl.BlockSpec((None, rows, D_RWKV), lambda b, c: (b, c, 0)),
        out_shape=jax.ShapeDtypeStruct((batch, seq, D_RWKV), BF16),
        scratch_shapes=[
            pltpu.VMEM((RWKV_HEAD_DIM, D_RWKV), F32),
            pltpu.VMEM((1, C_RWKV), F32),
        ],
        compiler_params=pltpu.CompilerParams(
            dimension_semantics=("arbitrary", "arbitrary"), vmem_limit_bytes=VMEM_LIMIT_BYTES),
        name="rwkv",
    )(z3, mu, w0, w2, a0, a2, k_k, k_a, r_k, gn_g, gn_b)
    return out.reshape(batch * seq, D_RWKV)


def _kv_proj_kernel(m_ref, g_ref, w_ref, k_ref, v_ref):
    mn = _rms_norm(m_ref[...], g_ref[...]).astype(BF16)
    k_ref[...] = jnp.dot(mn, w_ref[:, :D_MODEL], preferred_element_type=F32).astype(k_ref.dtype)
    v_ref[...] = jnp.dot(mn, w_ref[:, D_MODEL:], preferred_element_type=F32).astype(v_ref.dtype)


def _kv_proj(mem2, g, w_kv):
    m = mem2.shape[0]
    tm = MEM_LEN
    return pl.pallas_call(
        _kv_proj_kernel,
        grid=(m // tm,),
        in_specs=[
            pl.BlockSpec((tm, D_MODEL), lambda i: (i, 0)),
            pl.BlockSpec((1, D_MODEL), lambda i: (0, 0)),
            pl.BlockSpec((D_MODEL, 2 * D_MODEL), lambda i: (0, 0)),
        ],
        out_specs=[
            pl.BlockSpec((tm, D_MODEL), lambda i: (i, 0)),
            pl.BlockSpec((tm, D_MODEL), lambda i: (i, 0)),
        ],
        out_shape=[
            jax.ShapeDtypeStruct((m, D_MODEL), BF16),
            jax.ShapeDtypeStruct((m, D_MODEL), BF16),
        ],
        compiler_params=pltpu.CompilerParams(
            dimension_semantics=("arbitrary",), vmem_limit_bytes=VMEM_LIMIT_BYTES),
        name="kv_proj",
    )(mem2, g, w_kv)


def _out_attn_kernel(x_ref, ys_ref, yr_ref, wo1_ref, wo2_ref, gx_ref, wq_ref, k_ref, v_ref,
                     wo_ref, gf_ref, o_ref):
    h1 = (x_ref[...]
          + jnp.dot(ys_ref[...], wo1_ref[...], preferred_element_type=F32)
          + jnp.dot(yr_ref[...], wo2_ref[...], preferred_element_type=F32))
    hn = _rms_norm(h1, gx_ref[...]).astype(BF16)
    q = jnp.dot(hn, wq_ref[...], preferred_element_type=F32).astype(BF16)
    heads = []
    for h in range(XATTN_HEADS):
        sl = slice(h * XATTN_HEAD_DIM, (h + 1) * XATTN_HEAD_DIM)
        s = _dot_nt(q[:, sl], k_ref[:, sl]) * (XATTN_HEAD_DIM ** -0.5)
        s = s - jnp.max(s, axis=-1, keepdims=True)
        e = jnp.exp(s)
        p = e / jnp.sum(e, axis=-1, keepdims=True)
        heads.append(jnp.dot(p.astype(BF16), v_ref[:, sl], preferred_element_type=F32).astype(BF16))
    o = jnp.concatenate(heads, axis=-1)
    h2 = h1 + jnp.dot(o, wo_ref[...], preferred_element_type=F32)
    o_ref[...] = _rms_norm(h2, gf_ref[...])


def _out_attn(x2, y_sgu, y_rwkv, wo1, wo2, g_x, w_q, k_mem, v_mem, w_o, g_f, seq):
    m = x2.shape[0]
    tq = ATTN_ROWS
    per_batch = seq // tq
    row_blk = lambda n: pl.BlockSpec((tq, n), lambda i: (i, 0))
    full = lambda a, b: pl.BlockSpec((a, b), lambda i: (0, 0))
    mem_blk = pl.BlockSpec((MEM_LEN, D_MODEL), lambda i: (i // per_batch, 0))
    return pl.pallas_call(
        _out_attn_kernel,
        grid=(m // tq,),
        in_specs=[
            row_blk(D_MODEL), row_blk(D_SGU), row_blk(D_RWKV),
            full(D_SGU, D_MODEL), full(D_RWKV, D_MODEL), full(1, D_MODEL),
            full(D_MODEL, D_MODEL), mem_blk, mem_blk, full(D_MODEL, D_MODEL), full(1, D_MODEL),
        ],
        out_specs=row_blk(D_MODEL),
        out_shape=jax.ShapeDtypeStruct((m, D_MODEL), F32),
        compiler_params=pltpu.CompilerParams(
            dimension_semantics=("arbitrary",), vmem_limit_bytes=VMEM_LIMIT_BYTES),
        name="out_attn",
    )(x2, y_sgu, y_rwkv, wo1, wo2, g_x, w_q, k_mem, v_mem, w_o, g_f)


def kernel(x, mem, ln_mix_g, w_in, sgu_ln_g, sgu_ln_b, sgu_ws, sgu_bs, sgu_out_g, rw_mu, rw_w0, rw_w2, rw_a0, rw_a2, rw_k_k, rw_k_a, rw_r_k, rw_gn_g, rw_gn_b, w_out, ln_x_g, ln_mem_g, w_q, w_kv, w_o, ln_f_g):
    batch, seq, _ = x.shape
    assert w_in.shape[0] == 1, "the final norm is fused into the only layer's last call"
    row = lambda a: a.reshape(1, -1)
    h = x.reshape(batch * seq, D_MODEL)
    mem2 = mem.reshape(batch * MEM_LEN, D_MODEL)
    for l in range(1):
        z_sgu, z_rwkv = _in_proj(h, row(ln_mix_g[l]), w_in[l].astype(BF16))
        bs_full = jnp.repeat(sgu_bs[l].T, SGU_HEAD_DIM, axis=1)
        y_sgu = _sgu(z_sgu, row(sgu_ln_g[l]), row(sgu_ln_b[l]), sgu_ws[l], bs_full,
                     row(sgu_out_g[l]))
        y_rwkv = _rwkv(z_rwkv, row(rw_mu[l]), row(rw_w0[l]), rw_w2[l].astype(BF16),
                       row(rw_a0[l]), rw_a2[l].astype(BF16), row(rw_k_k[l]), row(rw_k_a[l]),
                       row(rw_r_k[l]), row(rw_gn_g[l]), row(rw_gn_b[l]), batch, seq)
        k_mem, v_mem = _kv_proj(mem2, row(ln_mem_g[l]), w_kv[l].astype(BF16))
        w_out_b = w_out[l].astype(BF16)
        h = _out_attn(h, y_sgu, y_rwkv, w_out_b[:D_SGU], w_out_b[D_SGU:], row(ln_x_g[l]),
                      w_q[l].astype(BF16), k_mem, v_mem, w_o[l].astype(BF16), row(ln_f_g), seq)
    return h.reshape(batch, seq, D_MODEL)
```

```python
import functools
import math

import jax
import jax.numpy as jnp
from jax import lax
from jax.experimental import pallas as pl
from jax.experimental.pallas import tpu as pltpu

D_MODEL = 1024
MEM_LEN = 256
D_SGU = 512
SGU_HEADS = 4
SGU_HEAD_DIM = D_SGU // SGU_HEADS
SGU_CHUNK = 128
D_RWKV = 512
RWKV_HEAD_DIM = 64
RWKV_HEADS = D_RWKV // RWKV_HEAD_DIM
LORA = 64
C_SGU = 3 * D_SGU
C_RWKV = 4 * D_RWKV + 2 * LORA
XATTN_HEADS = 4
XATTN_HEAD_DIM = D_MODEL // XATTN_HEADS
RMS_EPS = 1e-6
LN_EPS = 1e-5
GN_EPS = 64e-5

RWKV_CHUNK = 64
RWKV_BATCHES = 4
PAIR = 2 * RWKV_HEAD_DIM
RWKV_PAIRS = RWKV_HEADS // 2
IN_PROJ_ROWS = 512
SGU_ROWS = 512
ATTN_ROWS = 512
VMEM_LIMIT_BYTES = 48 * 1024 * 1024

F32 = jnp.float32
BF16 = jnp.bfloat16


def _dot(a, b):
    return jnp.dot(a.astype(BF16), b.astype(BF16), preferred_element_type=F32)


def _dot_nt(a, b):
    return lax.dot_general(a.astype(BF16), b.astype(BF16), (((1,), (1,)), ((), ())),
                           preferred_element_type=F32)


def _dot_tn(a, b):
    return lax.dot_general(a.astype(BF16), b.astype(BF16), (((0,), (0,)), ((), ())),
                           preferred_element_type=F32)


def _rms_norm(x, g):
    return x * lax.rsqrt(jnp.mean(x * x, axis=-1, keepdims=True) + RMS_EPS) * g


def _gelu(x):
    return 0.5 * x * (1.0 + lax.erf(x * (2.0 ** -0.5)))


def _silu(x):
    return x * jax.nn.sigmoid(x)


def _in_proj_kernel(x_ref, g_ref, w_ref, zs_ref, zr_ref):
    xn = _rms_norm(x_ref[...], g_ref[...]).astype(BF16)
    zs_ref[...] = jnp.dot(xn, w_ref[:, :C_SGU], preferred_element_type=F32)
    zr_ref[...] = jnp.dot(xn, w_ref[:, C_SGU:], preferred_element_type=F32)


def _in_proj(x2, g, w_in):
    m = x2.shape[0]
    tm = IN_PROJ_ROWS
    return pl.pallas_call(
        _in_proj_kernel,
        grid=(m // tm,),
        in_specs=[
            pl.BlockSpec((tm, D_MODEL), lambda i: (i, 0)),
            pl.BlockSpec((1, D_MODEL), lambda i: (0, 0)),
            pl.BlockSpec((D_MODEL, C_SGU + C_RWKV), lambda i: (0, 0)),
        ],
        out_specs=[
            pl.BlockSpec((tm, C_SGU), lambda i: (i, 0)),
            pl.BlockSpec((tm, C_RWKV), lambda i: (i, 0)),
        ],
        out_shape=[
            jax.ShapeDtypeStruct((m, C_SGU), F32),
            jax.ShapeDtypeStruct((m, C_RWKV), F32),
        ],
        compiler_params=pltpu.CompilerParams(
            dimension_semantics=("arbitrary",), vmem_limit_bytes=VMEM_LIMIT_BYTES),
        name="in_proj",
    )(x2, g, w_in)


def _sgu_kernel(zu_ref, zv_ref, zg_ref, lng_ref, lnb_ref, ws_ref, bs_ref, og_ref, o_ref, sv_ref):
    u = _gelu(zu_ref[...])
    v = _gelu(zv_ref[...])
    mu = jnp.mean(v, axis=-1, keepdims=True)
    vc = v - mu
    var = jnp.mean(vc * vc, axis=-1, keepdims=True)
    vn = (vc * lax.rsqrt(var + LN_EPS) * lng_ref[...] + lnb_ref[...]).astype(BF16)
    row = lax.broadcasted_iota(jnp.int32, (SGU_CHUNK, SGU_CHUNK), 0)
    col = lax.broadcasted_iota(jnp.int32, (SGU_CHUNK, SGU_CHUNK), 1)
    causal = col <= row
    for h in range(SGU_HEADS):
        cols = slice(h * SGU_HEAD_DIM, (h + 1) * SGU_HEAD_DIM)
        ws_c = jnp.where(causal, ws_ref[h], 0.0).astype(BF16)
        for c in range(SGU_ROWS // SGU_CHUNK):
            rows = slice(c * SGU_CHUNK, (c + 1) * SGU_CHUNK)
            sv_ref[rows, cols] = (
                jnp.dot(ws_c, vn[rows, cols], preferred_element_type=F32) + bs_ref[:, cols])
    y = _rms_norm(u * sv_ref[...], og_ref[...])
    o_ref[...] = (y * _silu(zg_ref[...])).astype(o_ref.dtype)


def _sgu(z_sgu, ln_g, ln_b, ws, bs_full, out_g):
    m = z_sgu.shape[0]
    tb = SGU_ROWS
    vec = pl.BlockSpec((1, D_SGU), lambda i: (0, 0))
    return pl.pallas_call(
        _sgu_kernel,
        grid=(m // tb,),
        in_specs=[
            pl.BlockSpec((tb, D_SGU), lambda i: (i, 0)),
            pl.BlockSpec((tb, D_SGU), lambda i: (i, 1)),
            pl.BlockSpec((tb, D_SGU), lambda i: (i, 2)),
            vec, vec,
            pl.BlockSpec((SGU_HEADS, SGU_CHUNK, SGU_CHUNK), lambda i: (0, 0, 0)),
            pl.BlockSpec((SGU_CHUNK, D_SGU), lambda i: (0, 0)),
            vec,
        ],
        out_specs=pl.BlockSpec((tb, D_SGU), lambda i: (i, 0)),
        out_shape=jax.ShapeDtypeStruct((m, D_SGU), BF16),
        scratch_shapes=[pltpu.VMEM((tb, D_SGU), F32)],
        compiler_params=pltpu.CompilerParams(
            dimension_semantics=("arbitrary",), vmem_limit_bytes=VMEM_LIMIT_BYTES),
        name="sgu",
    )(z_sgu, z_sgu, z_sgu, ln_g, ln_b, ws, bs_full, out_g)


RWKV_ROWS = RWKV_CHUNK * RWKV_BATCHES
RWKV_BUFFERS = (
    ("ar", 2 * RWKV_ROWS, BF16),
    ("bd_b", 2 * RWKV_ROWS, BF16),
    ("bd_k", 2 * RWKV_ROWS, BF16),
    ("bd_v", 2 * RWKV_ROWS, BF16),
    ("bd_a", 2 * RWKV_ROWS, BF16),
    ("bk_h", 2 * RWKV_ROWS, BF16),
    ("v", RWKV_ROWS, BF16),
    ("g_col", RWKV_ROWS, F32),
    ("bonus", RWKV_ROWS, F32),
    ("gate", RWKV_ROWS, F32),
)


def _block_diag(x, lane_a):
    top = jnp.where(lane_a, x, 0.0).astype(BF16)
    bot = jnp.where(lane_a, 0.0, x).astype(BF16)
    return jnp.concatenate([top, bot], axis=0)


def _head_sum(x, lane_a):
    outs = []
    for p in range(RWKV_PAIRS):
        xp = x[:, p * PAIR:(p + 1) * PAIR]
        sum_a = jnp.sum(jnp.where(lane_a, xp, 0.0), axis=-1, keepdims=True)
        sum_b = jnp.sum(jnp.where(lane_a, 0.0, xp), axis=-1, keepdims=True)
        outs.append(jnp.where(lane_a, sum_a, sum_b))
    return jnp.concatenate(outs, axis=-1)


def _rwkv_prepare(z_ref, par, prev_ref, first_chunk, out):
    C, N = RWKV_CHUNK, RWKV_HEAD_DIM
    mu_ref, w0_ref, w2_ref, a0_ref, a2_ref, kk_ref, ka_ref, rk_ref = par
    row1 = lax.broadcasted_iota(jnp.int32, (C, 1), 0)
    lane_c = lax.broadcasted_iota(jnp.int32, (C, PAIR), 1)
    lane_a = lane_c < N
    eye = (lane_c & (N - 1)) == lax.broadcasted_iota(jnp.int32, (C, PAIR), 0)
    eye_full = jnp.concatenate([eye] * RWKV_PAIRS, axis=1)
    lane_full = (lax.broadcasted_iota(jnp.int32, (C, D_RWKV), 1) & N) == 0
    row_c = lax.broadcasted_iota(jnp.int32, (C, C), 0)
    col_c = lax.broadcasted_iota(jnp.int32, (C, C), 1)
    tril = (col_c <= row_c).astype(F32).astype(BF16)
    tril3 = jnp.concatenate([tril, tril, tril], axis=1)

    def halves(x):
        return (jnp.where(lane_full, x, 0.0).astype(BF16),
                jnp.where(lane_full, 0.0, x).astype(BF16))

    def one_batch(b):
        z = z_ref[b]
        prev = jnp.where(first_chunk, 0.0, prev_ref[b:b + 1, :])
        z_prev = jnp.where(row1 == 0, prev, pltpu.roll(z, 1, axis=0))
        prev_ref[b:b + 1, :] = z[C - 1:C, :]
        zs = z + (z_prev - z) * mu_ref[...]

        r = zs[:, 0:D_RWKV]
        k = zs[:, D_RWKV:2 * D_RWKV]
        v = zs[:, 2 * D_RWKV:3 * D_RWKV]
        gate = zs[:, 3 * D_RWKV:4 * D_RWKV]
        wd = zs[:, 4 * D_RWKV:4 * D_RWKV + LORA]
        ad = zs[:, 4 * D_RWKV + LORA:4 * D_RWKV + 2 * LORA]

        lw = w0_ref[...] + _dot(jnp.tanh(wd), w2_ref[...])
        ld = -math.exp(-0.5) * jax.nn.sigmoid(lw)
        icl = jax.nn.sigmoid(a0_ref[...] + _dot(ad, a2_ref[...]))
        yield
        k2 = k * (1.0 + (icl - 1.0) * ka_ref[...])
        kk = k * kk_ref[...]
        kk = kk * lax.rsqrt(jnp.maximum(_head_sum(kk * kk, lane_a), 1e-24))
        bvec = kk * icl

        p1 = ld.astype(BF16)
        r1 = ld - p1.astype(F32)
        p2 = r1.astype(BF16)
        p3 = (r1 - p2.astype(F32)).astype(BF16)
        cum = jnp.dot(tril3, jnp.concatenate([p1, p2, p3], axis=0), preferred_element_type=F32)
        last = cum[C - 1:C, :]
        yield
        e_neg = jnp.exp(-cum)
        e_rest = jnp.exp(last - cum)
        a_t = -kk * jnp.exp(cum - ld)
        r_t = r * jnp.exp(cum)

        stacked = {
            "ar": (a_t.astype(BF16), r_t.astype(BF16)),
            "bd_b": halves(bvec * e_neg), "bd_k": halves(k2 * e_neg),
            "bd_v": halves(v), "bd_a": halves(a_t),
            "bk_h": ((bvec * e_rest).astype(BF16), (k2 * e_rest).astype(BF16)),
        }
        for name, (top, bot) in stacked.items():
            out[name][2 * b * C:(2 * b + 1) * C, :] = top
            out[name][(2 * b + 1) * C:(2 * b + 2) * C, :] = bot
        rows = slice(b * C, (b + 1) * C)
        out["v"][rows, :] = v.astype(BF16)
        out["g_col"][rows, :] = _head_sum(jnp.where(eye_full, jnp.exp(last), 0.0), lane_a)
        out["bonus"][rows, :] = _head_sum(r * k2 * rk_ref[...], lane_a) * v
        out["gate"][rows, :] = _silu(gate)
        yield

    batches = [one_batch(b) for b in range(RWKV_BATCHES)]
    for _ in range(3):
        for gen in batches:
            next(gen)
            yield


def _rwkv_chunks(buf, state_ref, first_chunk, gng_ref, gnb_ref, o_ref):
    C, N = RWKV_CHUNK, RWKV_HEAD_DIM
    row = lax.broadcasted_iota(jnp.int32, (C, PAIR), 0)
    lane = lax.broadcasted_iota(jnp.int32, (C, PAIR), 1)
    col = lane & (N - 1)
    lane_a = lane < N
    incl = col <= row
    strict = col < row

    units = [(b, p) for b in range(RWKV_BATCHES) for p in range(RWKV_PAIRS)]
    nu = len(units)
    lanes = lambda u: slice(u[1] * PAIR, (u[1] + 1) * PAIR)
    two = lambda name, u: buf[name][2 * u[0] * C:(2 * u[0] + 2) * C, lanes(u)]
    one = lambda name, u: buf[name][u[0] * C:(u[0] + 1) * C, lanes(u)]
    dot_f = lambda a, b: jnp.dot(a, b, preferred_element_type=F32)
    dot_bd = lambda lhs, rhs: dot_f(lhs.astype(BF16), _block_diag(rhs, lane_a))

    s_b = [_dot_nt(two("ar", u), two("bd_b", u)) for u in units]
    yield
    s_k = [_dot_nt(two("ar", u), two("bd_k", u)) for u in units]
    n_mat = [jnp.where(strict, s[:C], 0.0) for s in s_b]
    a_ak = [jnp.where(strict, s[:C], 0.0) for s in s_k]
    a_rb = [jnp.where(incl, s[C:], 0.0).astype(BF16) for s in s_b]
    a_rk = [jnp.where(incl, s[C:], 0.0).astype(BF16) for s in s_k]
    yield

    lvl = ((row >> 1) == (col >> 1)) & ((row & 1) == 1) & ((col & 1) == 0)
    eye_f = (col == row).astype(F32)
    x = [eye_f + jnp.where(lvl, n, 0.0) for n in n_mat]
    s = 2
    while s < C:
        shift = s.bit_length()
        lvl = ((row >> shift) == (col >> shift)) & ((row & s) != 0) & ((col & s) == 0)
        xn = [dot_bd(x[i], jnp.where(lvl, n_mat[i], 0.0)) for i in range(nu)]
        yield
        x = [x[i] + dot_bd(xn[i], x[i]) for i in range(nu)]
        yield
        s *= 2

    av = [dot_f(a_ak[i].astype(BF16), two("bd_v", u)) for i, u in enumerate(units)]
    yield
    p_mat = [dot_f(x[i].astype(BF16), two("bd_a", u)) for i, u in enumerate(units)]
    yield

    t_old = [jnp.where(first_chunk, 0.0, state_ref[u[0], :, lanes(u)]) for u in units]
    t_bd = [_block_diag(t, lane_a) for t in t_old]
    u_mat = [dot_f(jnp.concatenate([p_mat[i], x[i]], axis=1).astype(BF16),
                   jnp.concatenate([t_bd[i], _block_diag(av[i], lane_a)], axis=0))
             for i in range(nu)]
    yield
    y = [dot_f(jnp.concatenate([two("ar", u)[C:], a_rb[i], a_rk[i]], axis=1),
               jnp.concatenate([t_bd[i], _block_diag(u_mat[i], lane_a), two("bd_v", u)], axis=0))
         for i, u in enumerate(units)]
    yield
    full = [_dot_tn(two("bk_h", u),
                    jnp.concatenate([u_mat[i].astype(BF16), one("v", u)], axis=0))
            for i, u in enumerate(units)]
    for i, u in enumerate(units):
        state_ref[u[0], :, lanes(u)] = (one("g_col", u) * t_old[i]
                                        + jnp.where(lane_a, full[i][:N], full[i][N:]))
    yield

    for b in range(RWKV_BATCHES):
        rows = slice(b * C, (b + 1) * C)
        yb = jnp.concatenate(y[b * RWKV_PAIRS:(b + 1) * RWKV_PAIRS], axis=1)
        yc = yb - _head_sum(yb, lane_a) * (1.0 / N)
        var = _head_sum(yc * yc, lane_a) * (1.0 / N)
        yn = yc * lax.rsqrt(var + GN_EPS) * gng_ref[...] + gnb_ref[...]
        o_ref[b] = ((yn + buf["bonus"][rows, :]) * buf["gate"][rows, :]).astype(o_ref.dtype)
        yield


def _rwkv_kernel(chunks, z_ref, mu_ref, w0_ref, w2_ref, a0_ref, a2_ref, kk_ref, ka_ref, rk_ref,
                 gng_ref, gnb_ref, o_ref, state_ref, prev_ref, *bufs):
    names = [name for name, _, _ in RWKV_BUFFERS]
    sets = (dict(zip(names, bufs[:len(names)])), dict(zip(names, bufs[len(names):])))
    par = (mu_ref, w0_ref, w2_ref, a0_ref, a2_ref, kk_ref, ka_ref, rk_ref)
    step = pl.program_id(0)

    @pl.when(step == 0)
    def _():
        for ref in sets[1].values():
            ref[...] = jnp.zeros_like(ref)
        state_ref[...] = jnp.zeros_like(state_ref)
        prev_ref[...] = jnp.zeros_like(prev_ref)

    first_prepared = step % chunks == 0
    first_consumed = step % chunks == 1

    def run(nxt, cur):
        prepare = _rwkv_prepare(z_ref, par, prev_ref, first_prepared, nxt)
        for _ in _rwkv_chunks(cur, state_ref, first_consumed, gng_ref, gnb_ref, o_ref):
            next(prepare, None)
        for _ in prepare:
            pass

    @pl.when(step % 2 == 0)
    def _():
        run(sets[0], sets[1])

    @pl.when(step % 2 == 1)
    def _():
        run(sets[1], sets[0])


def _rwkv(z_rwkv, mu, w0, w2, a0, a2, k_k, k_a, r_k, gn_g, gn_b, batch, seq):
    C, NB = RWKV_CHUNK, RWKV_BATCHES
    chunks = seq // C
    n_blocks = (batch // NB) * chunks
    z3 = z_rwkv.reshape(batch, seq, C_RWKV)
    vec = lambda n: pl.BlockSpec((1, n), lambda s: (0, 0))
    lora = pl.BlockSpec((LORA, D_RWKV), lambda s: (0, 0))

    def block_of(s):
        return (s // chunks, s % chunks, 0)

    out = pl.pallas_call(
        functools.partial(_rwkv_kernel, chunks),
        grid=(n_blocks + 1,),
        in_specs=[
            pl.BlockSpec((NB, C, C_RWKV), lambda s: block_of(jnp.minimum(s, n_blocks - 1))),
            vec(C_RWKV), vec(D_RWKV), lora, vec(D_RWKV), lora,
            vec(D_RWKV), vec(D_RWKV), vec(D_RWKV), vec(D_RWKV), vec(D_RWKV),
        ],
        out_specs=pl.BlockSpec((NB, C, D_RWKV), lambda s: block_of(jnp.maximum(s - 1, 0))),
        out_shape=jax.ShapeDtypeStruct((batch, seq, D_RWKV), BF16),
        scratch_shapes=[
            pltpu.VMEM((NB, RWKV_HEAD_DIM, D_RWKV), F32),
            pltpu.VMEM((NB, C_RWKV), F32),
        ] + 2 * [pltpu.VMEM((n, D_RWKV), dt) for _, n, dt in RWKV_BUFFERS],
        compiler_params=pltpu.CompilerParams(
            dimension_semantics=("arbitrary",), vmem_limit_bytes=VMEM_LIMIT_BYTES),
        name="rwkv",
    )(z3, mu, w0, w2, a0, a2, k_k, k_a, r_k, gn_g, gn_b)
    return out.reshape(batch * seq, D_RWKV)


def _kv_proj_kernel(m_ref, g_ref, w_ref, k_ref, v_ref):
    mn = _rms_norm(m_ref[...], g_ref[...]).astype(BF16)
    k_ref[...] = jnp.dot(mn, w_ref[:, :D_MODEL], preferred_element_type=F32).astype(k_ref.dtype)
    v_ref[...] = jnp.dot(mn, w_ref[:, D_MODEL:], preferred_element_type=F32).astype(v_ref.dtype)


def _kv_proj(mem2, g, w_kv):
    m = mem2.shape[0]
    tm = MEM_LEN
    return pl.pallas_call(
        _kv_proj_kernel,
        grid=(m // tm,),
        in_specs=[
            pl.BlockSpec((tm, D_MODEL), lambda i: (i, 0)),
            pl.BlockSpec((1, D_MODEL), lambda i: (0, 0)),
            pl.BlockSpec((D_MODEL, 2 * D_MODEL), lambda i: (0, 0)),
        ],
        out_specs=[
            pl.BlockSpec((tm, D_MODEL), lambda i: (i, 0)),
            pl.BlockSpec((tm, D_MODEL), lambda i: (i, 0)),
        ],
        out_shape=[
            jax.ShapeDtypeStruct((m, D_MODEL), BF16),
            jax.ShapeDtypeStruct((m, D_MODEL), BF16),
        ],
        compiler_params=pltpu.CompilerParams(
            dimension_semantics=("arbitrary",), vmem_limit_bytes=VMEM_LIMIT_BYTES),
        name="kv_proj",
    )(mem2, g, w_kv)


def _out_attn_kernel(x_ref, ys_ref, yr_ref, wo1_ref, wo2_ref, gx_ref, wq_ref, k_ref, v_ref,
                     wo_ref, gf_ref, o_ref):
    h1 = (x_ref[...]
          + jnp.dot(ys_ref[...], wo1_ref[...], preferred_element_type=F32)
          + jnp.dot(yr_ref[...], wo2_ref[...], preferred_element_type=F32))
    hn = _rms_norm(h1, gx_ref[...]).astype(BF16)
    q = jnp.dot(hn, wq_ref[...], preferred_element_type=F32).astype(BF16)
    heads = []
    for h in range(XATTN_HEADS):
        sl = slice(h * XATTN_HEAD_DIM, (h + 1) * XATTN_HEAD_DIM)
        s = _dot_nt(q[:, sl], k_ref[:, sl]) * (XATTN_HEAD_DIM ** -0.5)
        s = s - jnp.max(s, axis=-1, keepdims=True)
        e = jnp.exp(s)
        p = e / jnp.sum(e, axis=-1, keepdims=True)
        heads.append(jnp.dot(p.astype(BF16), v_ref[:, sl], preferred_element_type=F32).astype(BF16))
    o = jnp.concatenate(heads, axis=-1)
    h2 = h1 + jnp.dot(o, wo_ref[...], preferred_element_type=F32)
    o_ref[...] = _rms_norm(h2, gf_ref[...])


def _out_attn(x2, y_sgu, y_rwkv, wo1, wo2, g_x, w_q, k_mem, v_mem, w_o, g_f, seq):
    m = x2.shape[0]
    tq = ATTN_ROWS
    per_batch = seq // tq
    row_blk = lambda n: pl.BlockSpec((tq, n), lambda i: (i, 0))
    full = lambda a, b: pl.BlockSpec((a, b), lambda i: (0, 0))
    mem_blk = pl.BlockSpec((MEM_LEN, D_MODEL), lambda i: (i // per_batch, 0))
    return pl.pallas_call(
        _out_attn_kernel,
        grid=(m // tq,),
        in_specs=[
            row_blk(D_MODEL), row_blk(D_SGU), row_blk(D_RWKV),
            full(D_SGU, D_MODEL), full(D_RWKV, D_MODEL), full(1, D_MODEL),
            full(D_MODEL, D_MODEL), mem_blk, mem_blk, full(D_MODEL, D_MODEL), full(1, D_MODEL),
        ],
        out_specs=row_blk(D_MODEL),
        out_shape=jax.ShapeDtypeStruct((m, D_MODEL), F32),
        compiler_params=pltpu.CompilerParams(
            dimension_semantics=("arbitrary",), vmem_limit_bytes=VMEM_LIMIT_BYTES),
        name="out_attn",
    )(x2, y_sgu, y_rwkv, wo1, wo2, g_x, w_q, k_mem, v_mem, w_o, g_f)


def kernel(x, mem, ln_mix_g, w_in, sgu_ln_g, sgu_ln_b, sgu_ws, sgu_bs, sgu_out_g, rw_mu, rw_w0, rw_w2, rw_a0, rw_a2, rw_k_k, rw_k_a, rw_r_k, rw_gn_g, rw_gn_b, w_out, ln_x_g, ln_mem_g, w_q, w_kv, w_o, ln_f_g):
    batch, seq, _ = x.shape
    assert w_in.shape[0] == 1, "the final norm is fused into the only layer's last call"
    row = lambda a: a.reshape(1, -1)
    h = x.reshape(batch * seq, D_MODEL)
    mem2 = mem.reshape(batch * MEM_LEN, D_MODEL)
    for l in range(1):
        z_sgu, z_rwkv = _in_proj(h, row(ln_mix_g[l]), w_in[l].astype(BF16))
        bs_full = jnp.repeat(sgu_bs[l].T, SGU_HEAD_DIM, axis=1)
        y_sgu = _sgu(z_sgu, row(sgu_ln_g[l]), row(sgu_ln_b[l]), sgu_ws[l], bs_full,
                     row(sgu_out_g[l]))
        y_rwkv = _rwkv(z_rwkv, row(rw_mu[l]), row(rw_w0[l]), rw_w2[l].astype(BF16),
                       row(rw_a0[l]), rw_a2[l].astype(BF16), row(rw_k_k[l]), row(rw_k_a[l]),
                       row(rw_r_k[l]), row(rw_gn_g[l]), row(rw_gn_b[l]), batch, seq)
        k_mem, v_mem = _kv_proj(mem2, row(ln_mem_g[l]), w_kv[l].astype(BF16))
        w_out_b = w_out[l].astype(BF16)
        h = _out_attn(h, y_sgu, y_rwkv, w_out_b[:D_SGU], w_out_b[D_SGU:], row(ln_x_g[l]),
                      w_q[l].astype(BF16), k_mem, v_mem, w_o[l].astype(BF16), row(ln_f_g), seq)
    return h.reshape(batch, seq, D_MODEL)
```

```python
import functools
import math

import jax
import jax.numpy as jnp
from jax import lax
from jax.experimental import pallas as pl
from jax.experimental.pallas import tpu as pltpu

D_MODEL = 1024
MEM_LEN = 256
D_SGU = 512
SGU_HEADS = 4
SGU_HEAD_DIM = D_SGU // SGU_HEADS
SGU_CHUNK = 128
D_RWKV = 512
RWKV_HEAD_DIM = 64
RWKV_HEADS = D_RWKV // RWKV_HEAD_DIM
LORA = 64
C_SGU = 3 * D_SGU
C_RWKV = 4 * D_RWKV + 2 * LORA
XATTN_HEADS = 4
XATTN_HEAD_DIM = D_MODEL // XATTN_HEADS
RMS_EPS = 1e-6
LN_EPS = 1e-5
GN_EPS = 64e-5

RWKV_CHUNK = 64
RWKV_BATCHES = 4
PAIR = 2 * RWKV_HEAD_DIM
RWKV_PAIRS = RWKV_HEADS // 2
IN_PROJ_ROWS = 512
SGU_ROWS = 512
ATTN_ROWS = 512
VMEM_LIMIT_BYTES = 48 * 1024 * 1024

SUBLANES = 8
LOG2_E = math.log2(math.e)

F32 = jnp.float32
BF16 = jnp.bfloat16


def _dot(a, b):
    return jnp.dot(a.astype(BF16), b.astype(BF16), preferred_element_type=F32)


def _dot_nt(a, b):
    return lax.dot_general(a.astype(BF16), b.astype(BF16), (((1,), (1,)), ((), ())),
                           preferred_element_type=F32)


def _dot_tn(a, b):
    return lax.dot_general(a.astype(BF16), b.astype(BF16), (((0,), (0,)), ((), ())),
                           preferred_element_type=F32)


def _rms_norm(x, g):
    return x * lax.rsqrt(jnp.mean(x * x, axis=-1, keepdims=True) + RMS_EPS) * g


def _gelu(x):
    return 0.5 * x * (1.0 + lax.erf(x * (2.0 ** -0.5)))


def _silu(x):
    return x * jax.nn.sigmoid(x)


def _in_proj_kernel(x_ref, g_ref, w_ref, zs_ref, zr_ref):
    xn = _rms_norm(x_ref[...], g_ref[...]).astype(BF16)
    zs_ref[...] = jnp.dot(xn, w_ref[:, :C_SGU], preferred_element_type=F32)
    zr_ref[...] = jnp.dot(xn, w_ref[:, C_SGU:], preferred_element_type=F32)


def _in_proj(x2, g, w_in):
    m = x2.shape[0]
    tm = IN_PROJ_ROWS
    return pl.pallas_call(
        _in_proj_kernel,
        grid=(m // tm,),
        in_specs=[
            pl.BlockSpec((tm, D_MODEL), lambda i: (i, 0)),
            pl.BlockSpec((1, D_MODEL), lambda i: (0, 0)),
            pl.BlockSpec((D_MODEL, C_SGU + C_RWKV), lambda i: (0, 0)),
        ],
        out_specs=[
            pl.BlockSpec((tm, C_SGU), lambda i: (i, 0)),
            pl.BlockSpec((tm, C_RWKV), lambda i: (i, 0)),
        ],
        out_shape=[
            jax.ShapeDtypeStruct((m, C_SGU), F32),
            jax.ShapeDtypeStruct((m, C_RWKV), F32),
        ],
        compiler_params=pltpu.CompilerParams(
            dimension_semantics=("arbitrary",), vmem_limit_bytes=VMEM_LIMIT_BYTES),
        name="in_proj",
    )(x2, g, w_in)


def _sgu_kernel(zu_ref, zv_ref, zg_ref, lng_ref, lnb_ref, ws_ref, bs_ref, og_ref, o_ref, sv_ref):
    u = _gelu(zu_ref[...])
    v = _gelu(zv_ref[...])
    mu = jnp.mean(v, axis=-1, keepdims=True)
    vc = v - mu
    var = jnp.mean(vc * vc, axis=-1, keepdims=True)
    vn = (vc * lax.rsqrt(var + LN_EPS) * lng_ref[...] + lnb_ref[...]).astype(BF16)
    row = lax.broadcasted_iota(jnp.int32, (SGU_CHUNK, SGU_CHUNK), 0)
    col = lax.broadcasted_iota(jnp.int32, (SGU_CHUNK, SGU_CHUNK), 1)
    causal = col <= row
    for h in range(SGU_HEADS):
        cols = slice(h * SGU_HEAD_DIM, (h + 1) * SGU_HEAD_DIM)
        ws_c = jnp.where(causal, ws_ref[h], 0.0).astype(BF16)
        for c in range(SGU_ROWS // SGU_CHUNK):
            rows = slice(c * SGU_CHUNK, (c + 1) * SGU_CHUNK)
            sv_ref[rows, cols] = (
                jnp.dot(ws_c, vn[rows, cols], preferred_element_type=F32) + bs_ref[:, cols])
    y = _rms_norm(u * sv_ref[...], og_ref[...])
    o_ref[...] = (y * _silu(zg_ref[...])).astype(o_ref.dtype)


def _sgu(z_sgu, ln_g, ln_b, ws, bs_full, out_g):
    m = z_sgu.shape[0]
    tb = SGU_ROWS
    vec = pl.BlockSpec((1, D_SGU), lambda i: (0, 0))
    return pl.pallas_call(
        _sgu_kernel,
        grid=(m // tb,),
        in_specs=[
            pl.BlockSpec((tb, D_SGU), lambda i: (i, 0)),
            pl.BlockSpec((tb, D_SGU), lambda i: (i, 1)),
            pl.BlockSpec((tb, D_SGU), lambda i: (i, 2)),
            vec, vec,
            pl.BlockSpec((SGU_HEADS, SGU_CHUNK, SGU_CHUNK), lambda i: (0, 0, 0)),
            pl.BlockSpec((SGU_CHUNK, D_SGU), lambda i: (0, 0)),
            vec,
        ],
        out_specs=pl.BlockSpec((tb, D_SGU), lambda i: (i, 0)),
        out_shape=jax.ShapeDtypeStruct((m, D_SGU), BF16),
        scratch_shapes=[pltpu.VMEM((tb, D_SGU), F32)],
        compiler_params=pltpu.CompilerParams(
            dimension_semantics=("arbitrary",), vmem_limit_bytes=VMEM_LIMIT_BYTES),
        name="sgu",
    )(z_sgu, z_sgu, z_sgu, ln_g, ln_b, ws, bs_full, out_g)


RWKV_ROWS = RWKV_CHUNK * RWKV_BATCHES
RWKV_PREPARE_AFTER_STAGE = (1, 3, 5, 7, 8, 9, 10, 11, 12, 13, 14, 15)
RWKV_BUFFERS = (
    ("ar", 2 * RWKV_ROWS, BF16),
    ("bk_h", 2 * RWKV_ROWS, BF16),
    ("b_t", RWKV_ROWS, BF16),
    ("k_t", RWKV_ROWS, BF16),
    ("v", RWKV_ROWS, BF16),
    ("g_col", RWKV_ROWS, F32),
    ("bonus", RWKV_ROWS, F32),
    ("gate", RWKV_ROWS, F32),
)


def _block_diag(x, keep_a, keep_b):
    return jnp.concatenate([x * keep_a, x * keep_b], axis=0)


def _head_sum(x, lane_a):
    outs = []
    for p in range(RWKV_PAIRS):
        xp = x[:, p * PAIR:(p + 1) * PAIR]
        sum_a = jnp.sum(jnp.where(lane_a, xp, 0.0), axis=-1, keepdims=True)
        sum_b = jnp.sum(jnp.where(lane_a, 0.0, xp), axis=-1, keepdims=True)
        outs.append(jnp.where(lane_a, sum_a, sum_b))
    return jnp.concatenate(outs, axis=-1)


def _rwkv_prepare(z_ref, par, prev_ref, first_chunk, out):
    C, N = RWKV_CHUNK, RWKV_HEAD_DIM
    mu_ref, w0_ref, w2_ref, a0_ref, a2_ref, kk_ref, ka_ref, rk_ref = par
    row1 = lax.broadcasted_iota(jnp.int32, (C, 1), 0)
    lane_c = lax.broadcasted_iota(jnp.int32, (C, PAIR), 1)
    lane_a = lane_c < N
    eye = (lane_c & (N - 1)) == lax.broadcasted_iota(jnp.int32, (C, PAIR), 0)
    eye_full = jnp.concatenate([eye] * RWKV_PAIRS, axis=1)
    row_c = lax.broadcasted_iota(jnp.int32, (C, C), 0)
    col_c = lax.broadcasted_iota(jnp.int32, (C, C), 1)
    tril = (col_c <= row_c).astype(F32).astype(BF16)
    tril3 = jnp.concatenate([tril, tril, tril], axis=1)

    def one_batch(b):
        z = z_ref[b]
        prev = jnp.where(first_chunk, 0.0, prev_ref[b:b + 1, :])
        rolled = pltpu.roll(z, 1, axis=0)
        z_prev = jnp.concatenate(
            [jnp.where(row1[:SUBLANES] == 0, prev, rolled[:SUBLANES]), rolled[SUBLANES:]], axis=0)
        prev_ref[b:b + 1, :] = z[C - 1:C, :]
        zs = z + (z_prev - z) * mu_ref[...]

        r = zs[:, 0:D_RWKV]
        k = zs[:, D_RWKV:2 * D_RWKV]
        v = zs[:, 2 * D_RWKV:3 * D_RWKV]
        gate = zs[:, 3 * D_RWKV:4 * D_RWKV]
        wd = zs[:, 4 * D_RWKV:4 * D_RWKV + LORA]
        ad = zs[:, 4 * D_RWKV + LORA:4 * D_RWKV + 2 * LORA]

        lw = w0_ref[...] + _dot(jnp.tanh(wd), w2_ref[...])
        ld = -(math.exp(-0.5) * LOG2_E) * jax.nn.sigmoid(lw)
        icl = jax.nn.sigmoid(a0_ref[...] + _dot(ad, a2_ref[...]))
        yield
        k2 = k * ((1.0 - ka_ref[...]) + ka_ref[...] * icl)
        kk = k * kk_ref[...]
        kk = kk * lax.rsqrt(jnp.maximum(_head_sum(kk * kk, lane_a), 1e-24))
        bvec = kk * icl

        p1 = ld.astype(BF16)
        r1 = ld - p1.astype(F32)
        p2 = r1.astype(BF16)
        p3 = (r1 - p2.astype(F32)).astype(BF16)
        cum = jnp.dot(tril3, jnp.concatenate([p1, p2, p3], axis=0), preferred_element_type=F32)
        last = cum[C - 1:C, :]
        yield
        e_neg = jnp.exp2(-cum)
        g_last = jnp.exp2(last)
        a_t = -kk * jnp.exp2(cum - ld)
        r_t = r * jnp.exp2(cum)
        b_t = bvec * e_neg
        k_t = k2 * e_neg

        stacked = {
            "ar": (a_t, r_t),
            "bk_h": (b_t * g_last, k_t * g_last),
        }
        for name, (top, bot) in stacked.items():
            out[name][2 * b * C:(2 * b + 1) * C, :] = top.astype(BF16)
            out[name][(2 * b + 1) * C:(2 * b + 2) * C, :] = bot.astype(BF16)
        rows = slice(b * C, (b + 1) * C)
        out["b_t"][rows, :] = b_t.astype(BF16)
        out["k_t"][rows, :] = k_t.astype(BF16)
        out["v"][rows, :] = v.astype(BF16)
        out["g_col"][rows, :] = _head_sum(jnp.where(eye_full, g_last, 0.0), lane_a)
        out["bonus"][rows, :] = _head_sum(r * k2 * rk_ref[...], lane_a) * v
        out["gate"][rows, :] = _silu(gate)
        yield

    batches = [one_batch(b) for b in range(RWKV_BATCHES)]
    for _ in range(3):
        for gen in batches:
            next(gen)
            yield


def _rwkv_chunks(buf, state_ref, first_chunk, gng_ref, gnb_ref, o_ref):
    C, N = RWKV_CHUNK, RWKV_HEAD_DIM
    row = lax.broadcasted_iota(jnp.int32, (C, PAIR), 0)
    lane = lax.broadcasted_iota(jnp.int32, (C, PAIR), 1)
    col = lane & (N - 1)
    lane_a = lane < N
    incl = col <= row
    strict = col < row

    units = [(b, p) for b in range(RWKV_BATCHES) for p in range(RWKV_PAIRS)]
    nu = len(units)
    lanes = lambda u: slice(u[1] * PAIR, (u[1] + 1) * PAIR)
    two = lambda name, u: buf[name][2 * u[0] * C:(2 * u[0] + 2) * C, lanes(u)]
    one = lambda name, u: buf[name][u[0] * C:(u[0] + 1) * C, lanes(u)]
    dot_f = lambda a, b: jnp.dot(a, b, preferred_element_type=F32)
    as_keep = lambda mask: jnp.where(mask, 1.0, 0.0).astype(BF16)
    keep_a, keep_b = as_keep(lane_a), as_keep(jnp.logical_not(lane_a))
    bd = lambda x: _block_diag(x, keep_a, keep_b)

    s_b = [_dot_nt(two("ar", u), bd(one("b_t", u))) for u in units]
    yield
    s_k = [_dot_nt(two("ar", u), bd(one("k_t", u))) for u in units]
    n_mat = [jnp.where(strict, s[:C], 0.0) for s in s_b]
    n_bf = [n.astype(BF16) for n in n_mat]
    a_ak = [jnp.where(strict, s[:C], 0.0).astype(BF16) for s in s_k]
    a_rb = [jnp.where(incl, s[C:], 0.0).astype(BF16) for s in s_b]
    a_rk = [jnp.where(incl, s[C:], 0.0).astype(BF16) for s in s_k]
    yield

    lvl = ((row >> 1) == (col >> 1)) & ((row & 1) == 1) & ((col & 1) == 0)
    eye_f = (col == row).astype(F32)
    x = [eye_f + jnp.where(lvl, n, 0.0) for n in n_mat]
    s = 2
    while s < C:
        shift = s.bit_length()
        lvl = ((row >> shift) == (col >> shift)) & ((row & s) != 0) & ((col & s) == 0)
        lvl_a, lvl_b = as_keep(lvl & lane_a), as_keep(lvl & jnp.logical_not(lane_a))
        x_bf = [xi.astype(BF16) for xi in x]
        xn = [dot_f(x_bf[i], _block_diag(n_bf[i], lvl_a, lvl_b)) for i in range(nu)]
        yield
        x = [x[i] + dot_f(xn[i].astype(BF16), bd(x_bf[i])) for i in range(nu)]
        yield
        s *= 2

    x_bf = [xi.astype(BF16) for xi in x]
    v_bd = [bd(one("v", u)) for u in units]
    av = [dot_f(a_ak[i], v_bd[i]).astype(BF16) for i in range(nu)]
    yield
    p_mat = [dot_f(x_bf[i], bd(two("ar", u)[:C])).astype(BF16) for i, u in enumerate(units)]
    yield

    t_old = [jnp.where(first_chunk, 0.0, state_ref[u[0], :, lanes(u)]) for u in units]
    t_bd = [bd(t.astype(BF16)) for t in t_old]
    u_mat = [dot_f(jnp.concatenate([p_mat[i], x_bf[i]], axis=1),
                   jnp.concatenate([t_bd[i], bd(av[i])], axis=0))
             for i in range(nu)]
    u_bf = [um.astype(BF16) for um in u_mat]
    yield
    y = [dot_f(jnp.concatenate([two("ar", u)[C:], a_rb[i], a_rk[i]], axis=1),
               jnp.concatenate([t_bd[i], bd(u_bf[i]), v_bd[i]], axis=0))
         for i, u in enumerate(units)]
    yield
    full = [_dot_tn(two("bk_h", u), jnp.concatenate([u_bf[i], one("v", u)], axis=0))
            for i, u in enumerate(units)]
    for i, u in enumerate(units):
        state_ref[u[0], :, lanes(u)] = (one("g_col", u) * t_old[i]
                                        + jnp.where(lane_a, full[i][:N], full[i][N:]))
    yield

    for b in range(RWKV_BATCHES):
        rows = slice(b * C, (b + 1) * C)
        yb = jnp.concatenate(y[b * RWKV_PAIRS:(b + 1) * RWKV_PAIRS], axis=1)
        yc = yb - _head_sum(yb, lane_a) * (1.0 / N)
        var = _head_sum(yc * yc, lane_a) * (1.0 / N)
        yn = yc * lax.rsqrt(var + GN_EPS) * gng_ref[...] + gnb_ref[...]
        o_ref[b] = ((yn + buf["bonus"][rows, :]) * buf["gate"][rows, :]).astype(o_ref.dtype)
        yield


def _rwkv_kernel(chunks, z_ref, mu_ref, w0_ref, w2_ref, a0_ref, a2_ref, kk_ref, ka_ref, rk_ref,
                 gng_ref, gnb_ref, o_ref, state_ref, prev_ref, *bufs):
    names = [name for name, _, _ in RWKV_BUFFERS]
    sets = (dict(zip(names, bufs[:len(names)])), dict(zip(names, bufs[len(names):])))
    par = (mu_ref, w0_ref, w2_ref, a0_ref, a2_ref, kk_ref, ka_ref, rk_ref)
    step = pl.program_id(0)

    @pl.when(step == 0)
    def _():
        for ref in sets[1].values():
            ref[...] = jnp.zeros_like(ref)
        state_ref[...] = jnp.zeros_like(state_ref)
        prev_ref[...] = jnp.zeros_like(prev_ref)

    first_prepared = step % chunks == 0
    first_consumed = step % chunks == 1

    def run(nxt, cur):
        prepare = _rwkv_prepare(z_ref, par, prev_ref, first_prepared, nxt)
        for stage, _ in enumerate(
                _rwkv_chunks(cur, state_ref, first_consumed, gng_ref, gnb_ref, o_ref)):
            if stage in RWKV_PREPARE_AFTER_STAGE:
                next(prepare, None)
        for _ in prepare:
            pass

    @pl.when(step % 2 == 0)
    def _():
        run(sets[0], sets[1])

    @pl.when(step % 2 == 1)
    def _():
        run(sets[1], sets[0])


def _rwkv(z_rwkv, mu, w0, w2, a0, a2, k_k, k_a, r_k, gn_g, gn_b, batch, seq):
    C, NB = RWKV_CHUNK, RWKV_BATCHES
    chunks = seq // C
    n_blocks = (batch // NB) * chunks
    z3 = z_rwkv.reshape(batch, seq, C_RWKV)
    vec = lambda n: pl.BlockSpec((1, n), lambda s: (0, 0))
    lora = pl.BlockSpec((LORA, D_RWKV), lambda s: (0, 0))

    def block_of(s):
        return (s // chunks, s % chunks, 0)

    out = pl.pallas_call(
        functools.partial(_rwkv_kernel, chunks),
        grid=(n_blocks + 1,),
        in_specs=[
            pl.BlockSpec((NB, C, C_RWKV), lambda s: block_of(jnp.minimum(s, n_blocks - 1))),
            vec(C_RWKV), vec(D_RWKV), lora, vec(D_RWKV), lora,
            vec(D_RWKV), vec(D_RWKV), vec(D_RWKV), vec(D_RWKV), vec(D_RWKV),
        ],
        out_specs=pl.BlockSpec((NB, C, D_RWKV), lambda s: block_of(jnp.maximum(s - 1, 0))),
        out_shape=jax.ShapeDtypeStruct((batch, seq, D_RWKV), BF16),
        scratch_shapes=[
            pltpu.VMEM((NB, RWKV_HEAD_DIM, D_RWKV), F32),
            pltpu.VMEM((NB, C_RWKV), F32),
        ] + 2 * [pltpu.VMEM((n, D_RWKV), dt) for _, n, dt in RWKV_BUFFERS],
        compiler_params=pltpu.CompilerParams(
            dimension_semantics=("arbitrary",), vmem_limit_bytes=VMEM_LIMIT_BYTES),
        name="rwkv",
    )(z3, mu, w0, w2, a0, a2, k_k, k_a, r_k, gn_g, gn_b)
    return out.reshape(batch * seq, D_RWKV)


def _kv_proj_kernel(m_ref, g_ref, w_ref, k_ref, v_ref):
    mn = _rms_norm(m_ref[...], g_ref[...]).astype(BF16)
    k_ref[...] = jnp.dot(mn, w_ref[:, :D_MODEL], preferred_element_type=F32).astype(k_ref.dtype)
    v_ref[...] = jnp.dot(mn, w_ref[:, D_MODEL:], preferred_element_type=F32).astype(v_ref.dtype)


def _kv_proj(mem2, g, w_kv):
    m = mem2.shape[0]
    tm = MEM_LEN
    return pl.pallas_call(
        _kv_proj_kernel,
        grid=(m // tm,),
        in_specs=[
            pl.BlockSpec((tm, D_MODEL), lambda i: (i, 0)),
            pl.BlockSpec((1, D_MODEL), lambda i: (0, 0)),
            pl.BlockSpec((D_MODEL, 2 * D_MODEL), lambda i: (0, 0)),
        ],
        out_specs=[
            pl.BlockSpec((tm, D_MODEL), lambda i: (i, 0)),
            pl.BlockSpec((tm, D_MODEL), lambda i: (i, 0)),
        ],
        out_shape=[
            jax.ShapeDtypeStruct((m, D_MODEL), BF16),
            jax.ShapeDtypeStruct((m, D_MODEL), BF16),
        ],
        compiler_params=pltpu.CompilerParams(
            dimension_semantics=("arbitrary",), vmem_limit_bytes=VMEM_LIMIT_BYTES),
        name="kv_proj",
    )(mem2, g, w_kv)


def _out_attn_kernel(x_ref, ys_ref, yr_ref, wo1_ref, wo2_ref, gx_ref, wq_ref, k_ref, v_ref,
                     wo_ref, gf_ref, o_ref):
    h1 = (x_ref[...]
          + jnp.dot(ys_ref[...], wo1_ref[...], preferred_element_type=F32)
          + jnp.dot(yr_ref[...], wo2_ref[...], preferred_element_type=F32))
    hn = _rms_norm(h1, gx_ref[...]).astype(BF16)
    q = jnp.dot(hn, wq_ref[...], preferred_element_type=F32).astype(BF16)
    heads = []
    for h in range(XATTN_HEADS):
        sl = slice(h * XATTN_HEAD_DIM, (h + 1) * XATTN_HEAD_DIM)
        s = _dot_nt(q[:, sl], k_ref[:, sl]) * (XATTN_HEAD_DIM ** -0.5)
        s = s - jnp.max(s, axis=-1, keepdims=True)
        e = jnp.exp(s)
        p = e / jnp.sum(e, axis=-1, keepdims=True)
        heads.append(jnp.dot(p.astype(BF16), v_ref[:, sl], preferred_element_type=F32).astype(BF16))
    o = jnp.concatenate(heads, axis=-1)
    h2 = h1 + jnp.dot(o, wo_ref[...], preferred_element_type=F32)
    o_ref[...] = _rms_norm(h2, gf_ref[...])


def _out_attn(x2, y_sgu, y_rwkv, wo1, wo2, g_x, w_q, k_mem, v_mem, w_o, g_f, seq):
    m = x2.shape[0]
    tq = ATTN_ROWS
    per_batch = seq // tq
    row_blk = lambda n: pl.BlockSpec((tq, n), lambda i: (i, 0))
    full = lambda a, b: pl.BlockSpec((a, b), lambda i: (0, 0))
    mem_blk = pl.BlockSpec((MEM_LEN, D_MODEL), lambda i: (i // per_batch, 0))
    return pl.pallas_call(
        _out_attn_kernel,
        grid=(m // tq,),
        in_specs=[
            row_blk(D_MODEL), row_blk(D_SGU), row_blk(D_RWKV),
            full(D_SGU, D_MODEL), full(D_RWKV, D_MODEL), full(1, D_MODEL),
            full(D_MODEL, D_MODEL), mem_blk, mem_blk, full(D_MODEL, D_MODEL), full(1, D_MODEL),
        ],
        out_specs=row_blk(D_MODEL),
        out_shape=jax.ShapeDtypeStruct((m, D_MODEL), F32),
        compiler_params=pltpu.CompilerParams(
            dimension_semantics=("arbitrary",), vmem_limit_bytes=VMEM_LIMIT_BYTES),
        name="out_attn",
    )(x2, y_sgu, y_rwkv, wo1, wo2, g_x, w_q, k_mem, v_mem, w_o, g_f)


def kernel(x, mem, ln_mix_g, w_in, sgu_ln_g, sgu_ln_b, sgu_ws, sgu_bs, sgu_out_g, rw_mu, rw_w0, rw_w2, rw_a0, rw_a2, rw_k_k, rw_k_a, rw_r_k, rw_gn_g, rw_gn_b, w_out, ln_x_g, ln_mem_g, w_q, w_kv, w_o, ln_f_g):
    batch, seq, _ = x.shape
    assert w_in.shape[0] == 1, "the final norm is fused into the only layer's last call"
    row = lambda a: a.reshape(1, -1)
    h = x.reshape(batch * seq, D_MODEL)
    mem2 = mem.reshape(batch * MEM_LEN, D_MODEL)
    for l in range(1):
        z_sgu, z_rwkv = _in_proj(h, row(ln_mix_g[l]), w_in[l].astype(BF16))
        bs_full = jnp.repeat(sgu_bs[l].T, SGU_HEAD_DIM, axis=1)
        y_sgu = _sgu(z_sgu, row(sgu_ln_g[l]), row(sgu_ln_b[l]), sgu_ws[l], bs_full,
                     row(sgu_out_g[l]))
        y_rwkv = _rwkv(z_rwkv, row(rw_mu[l]), row(rw_w0[l]), rw_w2[l].astype(BF16),
                       row(rw_a0[l]), rw_a2[l].astype(BF16), row(rw_k_k[l]), row(rw_k_a[l]),
                       row(rw_r_k[l]), row(rw_gn_g[l]), row(rw_gn_b[l]), batch, seq)
        k_mem, v_mem = _kv_proj(mem2, row(ln_mem_g[l]), w_kv[l].astype(BF16))
        w_out_b = w_out[l].astype(BF16)
        h = _out_attn(h, y_sgu, y_rwkv, w_out_b[:D_SGU], w_out_b[D_SGU:], row(ln_x_g[l]),
                      w_q[l].astype(BF16), k_mem, v_mem, w_o[l].astype(BF16), row(ln_f_g), seq)
    return h.reshape(batch, seq, D_MODEL)
```

```python
import functools
import math

import jax
import jax.numpy as jnp
from jax import lax
from jax.experimental import pallas as pl
from jax.experimental.pallas import tpu as pltpu

D_MODEL = 1024
MEM_LEN = 256
D_SGU = 512
SGU_HEADS = 4
SGU_HEAD_DIM = D_SGU // SGU_HEADS
SGU_CHUNK = 128
D_RWKV = 512
RWKV_HEAD_DIM = 64
RWKV_HEADS = D_RWKV // RWKV_HEAD_DIM
LORA = 64
C_SGU = 3 * D_SGU
C_RWKV = 4 * D_RWKV + 2 * LORA
XATTN_HEADS = 4
XATTN_HEAD_DIM = D_MODEL // XATTN_HEADS
RMS_EPS = 1e-6
LN_EPS = 1e-5
GN_EPS = 64e-5

RWKV_CHUNK = 64
RWKV_BATCHES = 4
PAIR = 2 * RWKV_HEAD_DIM
RWKV_PAIRS = RWKV_HEADS // 2
IN_PROJ_ROWS = 512
ATTN_ROWS = 512
VMEM_LIMIT_BYTES = 48 * 1024 * 1024

SUBLANES = 8
LOG2_E = math.log2(math.e)

F32 = jnp.float32
BF16 = jnp.bfloat16


def _dot(a, b):
    return jnp.dot(a.astype(BF16), b.astype(BF16), preferred_element_type=F32)


def _dot_nt(a, b):
    return lax.dot_general(a.astype(BF16), b.astype(BF16), (((1,), (1,)), ((), ())),
                           preferred_element_type=F32)


def _dot_tn(a, b):
    return lax.dot_general(a.astype(BF16), b.astype(BF16), (((0,), (0,)), ((), ())),
                           preferred_element_type=F32)


def _rms_norm(x, g):
    return x * lax.rsqrt(jnp.mean(x * x, axis=-1, keepdims=True) + RMS_EPS) * g


def _gelu(x):
    return 0.5 * x * (1.0 + lax.erf(x * (2.0 ** -0.5)))


def _silu(x):
    return x * jax.nn.sigmoid(x)


def _sgu_chunk(z, lng_ref, lnb_ref, ws_c, bs_ref, og_ref):
    u = _gelu(z[:, 0:D_SGU])
    v = _gelu(z[:, D_SGU:2 * D_SGU])
    gate = z[:, 2 * D_SGU:3 * D_SGU]
    vc = v - jnp.mean(v, axis=-1, keepdims=True)
    var = jnp.mean(vc * vc, axis=-1, keepdims=True)
    vn = (vc * lax.rsqrt(var + LN_EPS) * lng_ref[...] + lnb_ref[...]).astype(BF16)
    sv = jnp.concatenate(
        [jnp.dot(ws_c[h], vn[:, h * SGU_HEAD_DIM:(h + 1) * SGU_HEAD_DIM],
                 preferred_element_type=F32) for h in range(SGU_HEADS)], axis=1) + bs_ref[...]
    return _rms_norm(u * sv, og_ref[...]) * _silu(gate)


def _in_proj_kernel(x_ref, g_ref, w_ref, lng_ref, lnb_ref, ws_ref, bs_ref, og_ref,
                    ys_ref, zr_ref, zs_ref):
    xn = _rms_norm(x_ref[...], g_ref[...]).astype(BF16)
    zs_ref[...] = jnp.dot(xn, w_ref[:, :C_SGU], preferred_element_type=F32)
    row = lax.broadcasted_iota(jnp.int32, (SGU_CHUNK, SGU_CHUNK), 0)
    col = lax.broadcasted_iota(jnp.int32, (SGU_CHUNK, SGU_CHUNK), 1)
    ws_c = [jnp.where(col <= row, ws_ref[h], 0.0).astype(BF16) for h in range(SGU_HEADS)]
    n_chunks = IN_PROJ_ROWS // SGU_CHUNK
    lane_tiles = C_RWKV // PAIR
    bounds = [C_SGU + PAIR * (lane_tiles * c // n_chunks) for c in range(n_chunks + 1)]
    for c in range(n_chunks):
        cols = slice(bounds[c], bounds[c + 1])
        zr_ref[:, bounds[c] - C_SGU:bounds[c + 1] - C_SGU] = jnp.dot(
            xn, w_ref[:, cols], preferred_element_type=F32)
        rows = slice(c * SGU_CHUNK, (c + 1) * SGU_CHUNK)
        ys_ref[rows, :] = _sgu_chunk(zs_ref[rows, :], lng_ref, lnb_ref, ws_c, bs_ref,
                                     og_ref).astype(ys_ref.dtype)


def _in_proj(x2, g, w_in, ln_g, ln_b, ws, bs_full, out_g):
    m = x2.shape[0]
    tm = IN_PROJ_ROWS
    vec = pl.BlockSpec((1, D_SGU), lambda i: (0, 0))
    return pl.pallas_call(
        _in_proj_kernel,
        grid=(m // tm,),
        in_specs=[
            pl.BlockSpec((tm, D_MODEL), lambda i: (i, 0)),
            pl.BlockSpec((1, D_MODEL), lambda i: (0, 0)),
            pl.BlockSpec((D_MODEL, C_SGU + C_RWKV), lambda i: (0, 0)),
            vec, vec,
            pl.BlockSpec((SGU_HEADS, SGU_CHUNK, SGU_CHUNK), lambda i: (0, 0, 0)),
            pl.BlockSpec((SGU_CHUNK, D_SGU), lambda i: (0, 0)),
            vec,
        ],
        out_specs=[
            pl.BlockSpec((tm, D_SGU), lambda i: (i, 0)),
            pl.BlockSpec((tm, C_RWKV), lambda i: (i, 0)),
        ],
        out_shape=[
            jax.ShapeDtypeStruct((m, D_SGU), BF16),
            jax.ShapeDtypeStruct((m, C_RWKV), F32),
        ],
        scratch_shapes=[pltpu.VMEM((tm, C_SGU), F32)],
        compiler_params=pltpu.CompilerParams(
            dimension_semantics=("arbitrary",), vmem_limit_bytes=VMEM_LIMIT_BYTES),
        name="in_proj",
    )(x2, g, w_in, ln_g, ln_b, ws, bs_full, out_g)


RWKV_ROWS = RWKV_CHUNK * RWKV_BATCHES
RWKV_PREPARE_AFTER_STAGE = (1, 3, 5, 7, 8, 9, 10, 11, 12, 13, 14, 15)
RWKV_BUFFERS = (
    ("ar", 2 * RWKV_ROWS, BF16),
    ("bk_h", 2 * RWKV_ROWS, BF16),
    ("b_t", RWKV_ROWS, BF16),
    ("k_t", RWKV_ROWS, BF16),
    ("v", RWKV_ROWS, BF16),
    ("g_col", RWKV_ROWS, F32),
    ("bonus", RWKV_ROWS, F32),
    ("gate", RWKV_ROWS, F32),
)


def _block_diag(x, keep_a, keep_b):
    return jnp.concatenate([x * keep_a, x * keep_b], axis=0)


def _head_sum(x, lane_a):
    outs = []
    for p in range(RWKV_PAIRS):
        xp = x[:, p * PAIR:(p + 1) * PAIR]
        sum_a = jnp.sum(jnp.where(lane_a, xp, 0.0), axis=-1, keepdims=True)
        sum_b = jnp.sum(jnp.where(lane_a, 0.0, xp), axis=-1, keepdims=True)
        outs.append(jnp.where(lane_a, sum_a, sum_b))
    return jnp.concatenate(outs, axis=-1)


def _rwkv_prepare(z_ref, par, prev_ref, first_chunk, out):
    C, N = RWKV_CHUNK, RWKV_HEAD_DIM
    mu_ref, w0_ref, w2_ref, a0_ref, a2_ref, kk_ref, ka_ref, rk_ref = par
    row1 = lax.broadcasted_iota(jnp.int32, (C, 1), 0)
    lane_c = lax.broadcasted_iota(jnp.int32, (C, PAIR), 1)
    lane_a = lane_c < N
    eye = (lane_c & (N - 1)) == lax.broadcasted_iota(jnp.int32, (C, PAIR), 0)
    eye_full = jnp.concatenate([eye] * RWKV_PAIRS, axis=1)
    row_c = lax.broadcasted_iota(jnp.int32, (C, C), 0)
    col_c = lax.broadcasted_iota(jnp.int32, (C, C), 1)
    tril = (col_c <= row_c).astype(F32).astype(BF16)
    tril3 = jnp.concatenate([tril, tril, tril], axis=1)

    def one_batch(b):
        z = z_ref[b]
        prev = jnp.where(first_chunk, 0.0, prev_ref[b:b + 1, :])
        rolled = pltpu.roll(z, 1, axis=0)
        z_prev = jnp.concatenate(
            [jnp.where(row1[:SUBLANES] == 0, prev, rolled[:SUBLANES]), rolled[SUBLANES:]], axis=0)
        prev_ref[b:b + 1, :] = z[C - 1:C, :]
        zs = z + (z_prev - z) * mu_ref[...]

        r = zs[:, 0:D_RWKV]
        k = zs[:, D_RWKV:2 * D_RWKV]
        v = zs[:, 2 * D_RWKV:3 * D_RWKV]
        gate = zs[:, 3 * D_RWKV:4 * D_RWKV]
        wd = zs[:, 4 * D_RWKV:4 * D_RWKV + LORA]
        ad = zs[:, 4 * D_RWKV + LORA:4 * D_RWKV + 2 * LORA]

        lw = w0_ref[...] + _dot(jnp.tanh(wd), w2_ref[...])
        ld = -(math.exp(-0.5) * LOG2_E) * jax.nn.sigmoid(lw)
        icl = jax.nn.sigmoid(a0_ref[...] + _dot(ad, a2_ref[...]))
        yield
        k2 = k * ((1.0 - ka_ref[...]) + ka_ref[...] * icl)
        kk = k * kk_ref[...]
        kk = kk * lax.rsqrt(jnp.maximum(_head_sum(kk * kk, lane_a), 1e-24))
        bvec = kk * icl

        p1 = ld.astype(BF16)
        r1 = ld - p1.astype(F32)
        p2 = r1.astype(BF16)
        p3 = (r1 - p2.astype(F32)).astype(BF16)
        cum = jnp.dot(tril3, jnp.concatenate([p1, p2, p3], axis=0), preferred_element_type=F32)
        last = cum[C - 1:C, :]
        yield
        e_neg = jnp.exp2(-cum)
        g_last = jnp.exp2(last)
        a_t = -kk * jnp.exp2(cum - ld)
        r_t = r * jnp.exp2(cum)
        b_t = bvec * e_neg
        k_t = k2 * e_neg

        stacked = {
            "ar": (a_t, r_t),
            "bk_h": (b_t * g_last, k_t * g_last),
        }
        for name, (top, bot) in stacked.items():
            out[name][2 * b * C:(2 * b + 1) * C, :] = top.astype(BF16)
            out[name][(2 * b + 1) * C:(2 * b + 2) * C, :] = bot.astype(BF16)
        rows = slice(b * C, (b + 1) * C)
        out["b_t"][rows, :] = b_t.astype(BF16)
        out["k_t"][rows, :] = k_t.astype(BF16)
        out["v"][rows, :] = v.astype(BF16)
        out["g_col"][rows, :] = _head_sum(jnp.where(eye_full, g_last, 0.0), lane_a)
        out["bonus"][rows, :] = _head_sum(r * k2 * rk_ref[...], lane_a) * v
        out["gate"][rows, :] = _silu(gate)
        yield

    batches = [one_batch(b) for b in range(RWKV_BATCHES)]
    for _ in range(3):
        for gen in batches:
            next(gen)
            yield


def _rwkv_chunks(buf, state_ref, first_chunk, gng_ref, gnb_ref, o_ref):
    C, N = RWKV_CHUNK, RWKV_HEAD_DIM
    row = lax.broadcasted_iota(jnp.int32, (C, PAIR), 0)
    lane = lax.broadcasted_iota(jnp.int32, (C, PAIR), 1)
    col = lane & (N - 1)
    lane_a = lane < N
    incl = col <= row
    strict = col < row

    units = [(b, p) for b in range(RWKV_BATCHES) for p in range(RWKV_PAIRS)]
    nu = len(units)
    lanes = lambda u: slice(u[1] * PAIR, (u[1] + 1) * PAIR)
    two = lambda name, u: buf[name][2 * u[0] * C:(2 * u[0] + 2) * C, lanes(u)]
    one = lambda name, u: buf[name][u[0] * C:(u[0] + 1) * C, lanes(u)]
    dot_f = lambda a, b: jnp.dot(a, b, preferred_element_type=F32)
    as_keep = lambda mask: jnp.where(mask, 1.0, 0.0).astype(BF16)
    keep_a, keep_b = as_keep(lane_a), as_keep(jnp.logical_not(lane_a))
    bd = lambda x: _block_diag(x, keep_a, keep_b)

    s_b = [_dot_nt(two("ar", u), bd(one("b_t", u))) for u in units]
    yield
    s_k = [_dot_nt(two("ar", u), bd(one("k_t", u))) for u in units]
    n_mat = [jnp.where(strict, s[:C], 0.0) for s in s_b]
    n_bf = [n.astype(BF16) for n in n_mat]
    a_ak = [jnp.where(strict, s[:C], 0.0).astype(BF16) for s in s_k]
    a_rb = [jnp.where(incl, s[C:], 0.0).astype(BF16) for s in s_b]
    a_rk = [jnp.where(incl, s[C:], 0.0).astype(BF16) for s in s_k]
    yield

    lvl = ((row >> 1) == (col >> 1)) & ((row & 1) == 1) & ((col & 1) == 0)
    eye_f = (col == row).astype(F32)
    x = [eye_f + jnp.where(lvl, n, 0.0) for n in n_mat]
    s = 2
    while s < C:
        shift = s.bit_length()
        lvl = ((row >> shift) == (col >> shift)) & ((row & s) != 0) & ((col & s) == 0)
        lvl_a, lvl_b = as_keep(lvl & lane_a), as_keep(lvl & jnp.logical_not(lane_a))
        x_bf = [xi.astype(BF16) for xi in x]
        xn = [dot_f(x_bf[i], _block_diag(n_bf[i], lvl_a, lvl_b)) for i in range(nu)]
        yield
        x = [x[i] + dot_f(xn[i].astype(BF16), bd(x_bf[i])) for i in range(nu)]
        yield
        s *= 2

    x_bf = [xi.astype(BF16) for xi in x]
    v_bd = [bd(one("v", u)) for u in units]
    av = [dot_f(a_ak[i], v_bd[i]).astype(BF16) for i in range(nu)]
    yield
    p_mat = [dot_f(x_bf[i], bd(two("ar", u)[:C])).astype(BF16) for i, u in enumerate(units)]
    yield

    t_old = [jnp.where(first_chunk, 0.0, state_ref[u[0], :, lanes(u)]) for u in units]
    t_bd = [bd(t.astype(BF16)) for t in t_old]
    u_mat = [dot_f(jnp.concatenate([p_mat[i], x_bf[i]], axis=1),
                   jnp.concatenate([t_bd[i], bd(av[i])], axis=0))
             for i in range(nu)]
    u_bf = [um.astype(BF16) for um in u_mat]
    yield
    y = [dot_f(jnp.concatenate([two("ar", u)[C:], a_rb[i], a_rk[i]], axis=1),
               jnp.concatenate([t_bd[i], bd(u_bf[i]), v_bd[i]], axis=0))
         for i, u in enumerate(units)]
    yield
    full = [_dot_tn(two("bk_h", u), jnp.concatenate([u_bf[i], one("v", u)], axis=0))
            for i, u in enumerate(units)]
    for i, u in enumerate(units):
        state_ref[u[0], :, lanes(u)] = (one("g_col", u) * t_old[i]
                                        + jnp.where(lane_a, full[i][:N], full[i][N:]))
    yield

    for b in range(RWKV_BATCHES):
        rows = slice(b * C, (b + 1) * C)
        yb = jnp.concatenate(y[b * RWKV_PAIRS:(b + 1) * RWKV_PAIRS], axis=1)
        yc = yb - _head_sum(yb, lane_a) * (1.0 / N)
        var = _head_sum(yc * yc, lane_a) * (1.0 / N)
        yn = yc * lax.rsqrt(var + GN_EPS) * gng_ref[...] + gnb_ref[...]
        o_ref[b] = ((yn + buf["bonus"][rows, :]) * buf["gate"][rows, :]).astype(o_ref.dtype)
        yield


def _rwkv_kernel(chunks, z_ref, mu_ref, w0_ref, w2_ref, a0_ref, a2_ref, kk_ref, ka_ref, rk_ref,
                 gng_ref, gnb_ref, o_ref, state_ref, prev_ref, *bufs):
    names = [name for name, _, _ in RWKV_BUFFERS]
    sets = (dict(zip(names, bufs[:len(names)])), dict(zip(names, bufs[len(names):])))
    par = (mu_ref, w0_ref, w2_ref, a0_ref, a2_ref, kk_ref, ka_ref, rk_ref)
    step = pl.program_id(0)

    @pl.when(step == 0)
    def _():
        for ref in sets[1].values():
            ref[...] = jnp.zeros_like(ref)
        state_ref[...] = jnp.zeros_like(state_ref)
        prev_ref[...] = jnp.zeros_like(prev_ref)

    first_prepared = step % chunks == 0
    first_consumed = step % chunks == 1

    def run(nxt, cur):
        prepare = _rwkv_prepare(z_ref, par, prev_ref, first_prepared, nxt)
        for stage, _ in enumerate(
                _rwkv_chunks(cur, state_ref, first_consumed, gng_ref, gnb_ref, o_ref)):
            if stage in RWKV_PREPARE_AFTER_STAGE:
                next(prepare, None)
        for _ in prepare:
            pass

    @pl.when(step % 2 == 0)
    def _():
        run(sets[0], sets[1])

    @pl.when(step % 2 == 1)
    def _():
        run(sets[1], sets[0])


def _rwkv(z_rwkv, mu, w0, w2, a0, a2, k_k, k_a, r_k, gn_g, gn_b, batch, seq):
    C, NB = RWKV_CHUNK, RWKV_BATCHES
    chunks = seq // C
    n_blocks = (batch // NB) * chunks
    z3 = z_rwkv.reshape(batch, seq, C_RWKV)
    vec = lambda n: pl.BlockSpec((1, n), lambda s: (0, 0))
    lora = pl.BlockSpec((LORA, D_RWKV), lambda s: (0, 0))

    def block_of(s):
        return (s // chunks, s % chunks, 0)

    out = pl.pallas_call(
        functools.partial(_rwkv_kernel, chunks),
        grid=(n_blocks + 1,),
        in_specs=[
            pl.BlockSpec((NB, C, C_RWKV), lambda s: block_of(jnp.minimum(s, n_blocks - 1))),
            vec(C_RWKV), vec(D_RWKV), lora, vec(D_RWKV), lora,
            vec(D_RWKV), vec(D_RWKV), vec(D_RWKV), vec(D_RWKV), vec(D_RWKV),
        ],
        out_specs=pl.BlockSpec((NB, C, D_RWKV), lambda s: block_of(jnp.maximum(s - 1, 0))),
        out_shape=jax.ShapeDtypeStruct((batch, seq, D_RWKV), BF16),
        scratch_shapes=[
            pltpu.VMEM((NB, RWKV_HEAD_DIM, D_RWKV), F32),
            pltpu.VMEM((NB, C_RWKV), F32),
        ] + 2 * [pltpu.VMEM((n, D_RWKV), dt) for _, n, dt in RWKV_BUFFERS],
        compiler_params=pltpu.CompilerParams(
            dimension_semantics=("arbitrary",), vmem_limit_bytes=VMEM_LIMIT_BYTES),
        name="rwkv",
    )(z3, mu, w0, w2, a0, a2, k_k, k_a, r_k, gn_g, gn_b)
    return out.reshape(batch * seq, D_RWKV)


def _kv_proj_kernel(m_ref, g_ref, w_ref, k_ref, v_ref):
    mn = _rms_norm(m_ref[...], g_ref[...]).astype(BF16)
    k_ref[...] = jnp.dot(mn, w_ref[:, :D_MODEL], preferred_element_type=F32).astype(k_ref.dtype)
    v_ref[...] = jnp.dot(mn, w_ref[:, D_MODEL:], preferred_element_type=F32).astype(v_ref.dtype)


def _kv_proj(mem2, g, w_kv):
    m = mem2.shape[0]
    tm = MEM_LEN
    return pl.pallas_call(
        _kv_proj_kernel,
        grid=(m // tm,),
        in_specs=[
            pl.BlockSpec((tm, D_MODEL), lambda i: (i, 0)),
            pl.BlockSpec((1, D_MODEL), lambda i: (0, 0)),
            pl.BlockSpec((D_MODEL, 2 * D_MODEL), lambda i: (0, 0)),
        ],
        out_specs=[
            pl.BlockSpec((tm, D_MODEL), lambda i: (i, 0)),
            pl.BlockSpec((tm, D_MODEL), lambda i: (i, 0)),
        ],
        out_shape=[
            jax.ShapeDtypeStruct((m, D_MODEL), BF16),
            jax.ShapeDtypeStruct((m, D_MODEL), BF16),
        ],
        compiler_params=pltpu.CompilerParams(
            dimension_semantics=("arbitrary",), vmem_limit_bytes=VMEM_LIMIT_BYTES),
        name="kv_proj",
    )(mem2, g, w_kv)


def _out_attn_kernel(x_ref, ys_ref, yr_ref, wo1_ref, wo2_ref, gx_ref, wq_ref, k_ref, v_ref,
                     wo_ref, gf_ref, o_ref):
    h1 = (x_ref[...]
          + jnp.dot(ys_ref[...], wo1_ref[...], preferred_element_type=F32)
          + jnp.dot(yr_ref[...], wo2_ref[...], preferred_element_type=F32))
    hn = _rms_norm(h1, gx_ref[...]).astype(BF16)
    q = jnp.dot(hn, wq_ref[...], preferred_element_type=F32).astype(BF16)
    heads = []
    for h in range(XATTN_HEADS):
        sl = slice(h * XATTN_HEAD_DIM, (h + 1) * XATTN_HEAD_DIM)
        s = _dot_nt(q[:, sl], k_ref[:, sl]) * (XATTN_HEAD_DIM ** -0.5)
        s = s - jnp.max(s, axis=-1, keepdims=True)
        e = jnp.exp(s)
        p = e / jnp.sum(e, axis=-1, keepdims=True)
        heads.append(jnp.dot(p.astype(BF16), v_ref[:, sl], preferred_element_type=F32).astype(BF16))
    o = jnp.concatenate(heads, axis=-1)
    h2 = h1 + jnp.dot(o, wo_ref[...], preferred_element_type=F32)
    o_ref[...] = _rms_norm(h2, gf_ref[...])


def _out_attn(x2, y_sgu, y_rwkv, wo1, wo2, g_x, w_q, k_mem, v_mem, w_o, g_f, seq):
    m = x2.shape[0]
    tq = ATTN_ROWS
    per_batch = seq // tq
    row_blk = lambda n: pl.BlockSpec((tq, n), lambda i: (i, 0))
    full = lambda a, b: pl.BlockSpec((a, b), lambda i: (0, 0))
    mem_blk = pl.BlockSpec((MEM_LEN, D_MODEL), lambda i: (i // per_batch, 0))
    return pl.pallas_call(
        _out_attn_kernel,
        grid=(m // tq,),
        in_specs=[
            row_blk(D_MODEL), row_blk(D_SGU), row_blk(D_RWKV),
            full(D_SGU, D_MODEL), full(D_RWKV, D_MODEL), full(1, D_MODEL),
            full(D_MODEL, D_MODEL), mem_blk, mem_blk, full(D_MODEL, D_MODEL), full(1, D_MODEL),
        ],
        out_specs=row_blk(D_MODEL),
        out_shape=jax.ShapeDtypeStruct((m, D_MODEL), F32),
        compiler_params=pltpu.CompilerParams(
            dimension_semantics=("arbitrary",), vmem_limit_bytes=VMEM_LIMIT_BYTES),
        name="out_attn",
    )(x2, y_sgu, y_rwkv, wo1, wo2, g_x, w_q, k_mem, v_mem, w_o, g_f)


def kernel(x, mem, ln_mix_g, w_in, sgu_ln_g, sgu_ln_b, sgu_ws, sgu_bs, sgu_out_g, rw_mu, rw_w0, rw_w2, rw_a0, rw_a2, rw_k_k, rw_k_a, rw_r_k, rw_gn_g, rw_gn_b, w_out, ln_x_g, ln_mem_g, w_q, w_kv, w_o, ln_f_g):
    batch, seq, _ = x.shape
    assert w_in.shape[0] == 1, "the final norm is fused into the only layer's last call"
    row = lambda a: a.reshape(1, -1)
    h = x.reshape(batch * seq, D_MODEL)
    mem2 = mem.reshape(batch * MEM_LEN, D_MODEL)
    for l in range(1):
        bs_full = jnp.repeat(sgu_bs[l].T, SGU_HEAD_DIM, axis=1)
        y_sgu, z_rwkv = _in_proj(h, row(ln_mix_g[l]), w_in[l].astype(BF16), row(sgu_ln_g[l]),
                                 row(sgu_ln_b[l]), sgu_ws[l], bs_full, row(sgu_out_g[l]))
        y_rwkv = _rwkv(z_rwkv, row(rw_mu[l]), row(rw_w0[l]), rw_w2[l].astype(BF16),
                       row(rw_a0[l]), rw_a2[l].astype(BF16), row(rw_k_k[l]), row(rw_k_a[l]),
                       row(rw_r_k[l]), row(rw_gn_g[l]), row(rw_gn_b[l]), batch, seq)
        k_mem, v_mem = _kv_proj(mem2, row(ln_mem_g[l]), w_kv[l].astype(BF16))
        w_out_b = w_out[l].astype(BF16)
        h = _out_attn(h, y_sgu, y_rwkv, w_out_b[:D_SGU], w_out_b[D_SGU:], row(ln_x_g[l]),
                      w_q[l].astype(BF16), k_mem, v_mem, w_o[l].astype(BF16), row(ln_f_g), seq)
    return h.reshape(batch, seq, D_MODEL)
```

```python
import functools
import math

import jax
import jax.numpy as jnp
from jax import lax
from jax.experimental import pallas as pl
from jax.experimental.pallas import tpu as pltpu

D_MODEL = 1024
MEM_LEN = 256
D_SGU = 512
SGU_HEADS = 4
SGU_HEAD_DIM = D_SGU // SGU_HEADS
SGU_CHUNK = 128
D_RWKV = 512
RWKV_HEAD_DIM = 64
RWKV_HEADS = D_RWKV // RWKV_HEAD_DIM
LORA = 64
C_SGU = 3 * D_SGU
C_RWKV = 4 * D_RWKV + 2 * LORA
XATTN_HEADS = 4
XATTN_HEAD_DIM = D_MODEL // XATTN_HEADS
RMS_EPS = 1e-6
LN_EPS = 1e-5
GN_EPS = 64e-5

RWKV_CHUNK = 64
RWKV_BATCHES = 4
PAIR = 2 * RWKV_HEAD_DIM
RWKV_PAIRS = RWKV_HEADS // 2
IN_PROJ_ROWS = 512
ATTN_ROWS = 512
VMEM_LIMIT_BYTES = 48 * 1024 * 1024

SUBLANES = 8
LOG2_E = math.log2(math.e)

F32 = jnp.float32
BF16 = jnp.bfloat16


def _dot(a, b):
    return jnp.dot(a.astype(BF16), b.astype(BF16), preferred_element_type=F32)


def _dot_nt(a, b):
    return lax.dot_general(a.astype(BF16), b.astype(BF16), (((1,), (1,)), ((), ())),
                           preferred_element_type=F32)


def _dot_tn(a, b):
    return lax.dot_general(a.astype(BF16), b.astype(BF16), (((0,), (0,)), ((), ())),
                           preferred_element_type=F32)


def _rms_norm(x, g):
    return x * lax.rsqrt(jnp.mean(x * x, axis=-1, keepdims=True) + RMS_EPS) * g


def _gelu(x):
    return 0.5 * x * (1.0 + lax.erf(x * (2.0 ** -0.5)))


def _silu(x):
    return x * jax.nn.sigmoid(x)


def _sgu_chunk(z, lng_ref, lnb_ref, ws_c, bs_ref, og_ref):
    u = _gelu(z[:, 0:D_SGU])
    v = _gelu(z[:, D_SGU:2 * D_SGU])
    gate = z[:, 2 * D_SGU:3 * D_SGU]
    vc = v - jnp.mean(v, axis=-1, keepdims=True)
    var = jnp.mean(vc * vc, axis=-1, keepdims=True)
    vn = (vc * lax.rsqrt(var + LN_EPS) * lng_ref[...] + lnb_ref[...]).astype(BF16)
    sv = jnp.concatenate(
        [jnp.dot(ws_c[h], vn[:, h * SGU_HEAD_DIM:(h + 1) * SGU_HEAD_DIM],
                 preferred_element_type=F32) for h in range(SGU_HEADS)], axis=1) + bs_ref[...]
    return _rms_norm(u * sv, og_ref[...]) * _silu(gate)


def _in_proj_kernel(x_ref, g_ref, w_ref, lng_ref, lnb_ref, ws_ref, bs_ref, og_ref,
                    ys_ref, zr_ref, zs_ref):
    xn = _rms_norm(x_ref[...], g_ref[...]).astype(BF16)
    zs_ref[...] = jnp.dot(xn, w_ref[:, :C_SGU], preferred_element_type=F32)
    row = lax.broadcasted_iota(jnp.int32, (SGU_CHUNK, SGU_CHUNK), 0)
    col = lax.broadcasted_iota(jnp.int32, (SGU_CHUNK, SGU_CHUNK), 1)
    ws_c = [jnp.where(col <= row, ws_ref[h], 0.0).astype(BF16) for h in range(SGU_HEADS)]
    n_chunks = IN_PROJ_ROWS // SGU_CHUNK
    lane_tiles = C_RWKV // PAIR
    bounds = [C_SGU + PAIR * (lane_tiles * c // n_chunks) for c in range(n_chunks + 1)]
    for c in range(n_chunks):
        cols = slice(bounds[c], bounds[c + 1])
        zr_ref[:, bounds[c] - C_SGU:bounds[c + 1] - C_SGU] = jnp.dot(
            xn, w_ref[:, cols], preferred_element_type=F32)
        rows = slice(c * SGU_CHUNK, (c + 1) * SGU_CHUNK)
        ys_ref[rows, :] = _sgu_chunk(zs_ref[rows, :], lng_ref, lnb_ref, ws_c, bs_ref,
                                     og_ref).astype(ys_ref.dtype)


def _in_proj(x2, g, w_in, ln_g, ln_b, ws, bs_full, out_g):
    m = x2.shape[0]
    tm = IN_PROJ_ROWS
    vec = pl.BlockSpec((1, D_SGU), lambda i: (0, 0))
    return pl.pallas_call(
        _in_proj_kernel,
        grid=(m // tm,),
        in_specs=[
            pl.BlockSpec((tm, D_MODEL), lambda i: (i, 0)),
            pl.BlockSpec((1, D_MODEL), lambda i: (0, 0)),
            pl.BlockSpec((D_MODEL, C_SGU + C_RWKV), lambda i: (0, 0)),
            vec, vec,
            pl.BlockSpec((SGU_HEADS, SGU_CHUNK, SGU_CHUNK), lambda i: (0, 0, 0)),
            pl.BlockSpec((SGU_CHUNK, D_SGU), lambda i: (0, 0)),
            vec,
        ],
        out_specs=[
            pl.BlockSpec((tm, D_SGU), lambda i: (i, 0)),
            pl.BlockSpec((tm, C_RWKV), lambda i: (i, 0)),
        ],
        out_shape=[
            jax.ShapeDtypeStruct((m, D_SGU), BF16),
            jax.ShapeDtypeStruct((m, C_RWKV), F32),
        ],
        scratch_shapes=[pltpu.VMEM((tm, C_SGU), F32)],
        compiler_params=pltpu.CompilerParams(
            dimension_semantics=("arbitrary",), vmem_limit_bytes=VMEM_LIMIT_BYTES),
        name="in_proj",
    )(x2, g, w_in, ln_g, ln_b, ws, bs_full, out_g)


RWKV_ROWS = RWKV_CHUNK * RWKV_BATCHES
RWKV_PREPARE_AFTER_STAGE = (1, 3, 5, 7, 8, 9, 10, 11, 12, 13, 14, 15)
RWKV_BUFFERS = (
    ("ar", 2 * RWKV_ROWS, BF16),
    ("bk_h", 2 * RWKV_ROWS, BF16),
    ("b_t", RWKV_ROWS, BF16),
    ("k_t", RWKV_ROWS, BF16),
    ("v", RWKV_ROWS, BF16),
    ("g_col", RWKV_ROWS, F32),
    ("bonus", RWKV_ROWS, F32),
    ("gate", RWKV_ROWS, F32),
)


def _block_diag(x, keep_a, keep_b):
    return jnp.concatenate([x * keep_a, x * keep_b], axis=0)


def _dots_paired(lhs, rhs):
    out = []
    for i in range(0, len(lhs), 2):
        m, n = lhs[i].shape[0], rhs[i].shape[1]
        both = jnp.dot(jnp.concatenate([lhs[i], lhs[i + 1]], axis=0),
                       jnp.concatenate([rhs[i], rhs[i + 1]], axis=1),
                       preferred_element_type=F32)
        out += [both[:m, :n], both[m:, n:]]
    return out


def _dots_paired_tn(lhs, rhs):
    out = []
    for i in range(0, len(lhs), 2):
        m, n = lhs[i].shape[1], rhs[i].shape[1]
        both = _dot_tn(jnp.concatenate([lhs[i], lhs[i + 1]], axis=1),
                       jnp.concatenate([rhs[i], rhs[i + 1]], axis=1))
        out += [both[:m, :n], both[m:, n:]]
    return out


def _head_sum(x, lane_a):
    outs = []
    for p in range(RWKV_PAIRS):
        xp = x[:, p * PAIR:(p + 1) * PAIR]
        sum_a = jnp.sum(jnp.where(lane_a, xp, 0.0), axis=-1, keepdims=True)
        sum_b = jnp.sum(jnp.where(lane_a, 0.0, xp), axis=-1, keepdims=True)
        outs.append(jnp.where(lane_a, sum_a, sum_b))
    return jnp.concatenate(outs, axis=-1)


def _rwkv_prepare(z_ref, par, prev_ref, first_chunk, out):
    C, N = RWKV_CHUNK, RWKV_HEAD_DIM
    mu_ref, w0_ref, w2_ref, a0_ref, a2_ref, kk_ref, ka_ref, rk_ref = par
    row1 = lax.broadcasted_iota(jnp.int32, (C, 1), 0)
    lane_c = lax.broadcasted_iota(jnp.int32, (C, PAIR), 1)
    lane_a = lane_c < N
    eye = (lane_c & (N - 1)) == lax.broadcasted_iota(jnp.int32, (C, PAIR), 0)
    eye_full = jnp.concatenate([eye] * RWKV_PAIRS, axis=1)
    row_c = lax.broadcasted_iota(jnp.int32, (C, C), 0)
    col_c = lax.broadcasted_iota(jnp.int32, (C, C), 1)
    tril = (col_c <= row_c).astype(F32).astype(BF16)
    tril3 = jnp.concatenate([tril, tril, tril], axis=1)

    def one_batch(b):
        z = z_ref[b]
        prev = jnp.where(first_chunk, 0.0, prev_ref[b:b + 1, :])
        rolled = pltpu.roll(z, 1, axis=0)
        z_prev = jnp.concatenate(
            [jnp.where(row1[:SUBLANES] == 0, prev, rolled[:SUBLANES]), rolled[SUBLANES:]], axis=0)
        prev_ref[b:b + 1, :] = z[C - 1:C, :]
        zs = z + (z_prev - z) * mu_ref[...]

        r = zs[:, 0:D_RWKV]
        k = zs[:, D_RWKV:2 * D_RWKV]
        v = zs[:, 2 * D_RWKV:3 * D_RWKV]
        gate = zs[:, 3 * D_RWKV:4 * D_RWKV]
        wd = zs[:, 4 * D_RWKV:4 * D_RWKV + LORA]
        ad = zs[:, 4 * D_RWKV + LORA:4 * D_RWKV + 2 * LORA]

        lw = w0_ref[...] + _dot(jnp.tanh(wd), w2_ref[...])
        ld = -(math.exp(-0.5) * LOG2_E) * jax.nn.sigmoid(lw)
        icl = jax.nn.sigmoid(a0_ref[...] + _dot(ad, a2_ref[...]))
        yield
        k2 = k * ((1.0 - ka_ref[...]) + ka_ref[...] * icl)
        kk = k * kk_ref[...]
        kk = kk * lax.rsqrt(jnp.maximum(_head_sum(kk * kk, lane_a), 1e-24))
        bvec = kk * icl

        p1 = ld.astype(BF16)
        r1 = ld - p1.astype(F32)
        p2 = r1.astype(BF16)
        p3 = (r1 - p2.astype(F32)).astype(BF16)
        cum = jnp.dot(tril3, jnp.concatenate([p1, p2, p3], axis=0), preferred_element_type=F32)
        last = cum[C - 1:C, :]
        yield
        e_neg = jnp.exp2(-cum)
        g_last = jnp.exp2(last)
        a_t = -kk * jnp.exp2(cum - ld)
        r_t = r * jnp.exp2(cum)
        b_t = bvec * e_neg
        k_t = k2 * e_neg

        stacked = {
            "ar": (a_t, r_t),
            "bk_h": (b_t * g_last, k_t * g_last),
        }
        for name, (top, bot) in stacked.items():
            out[name][2 * b * C:(2 * b + 1) * C, :] = top.astype(BF16)
            out[name][(2 * b + 1) * C:(2 * b + 2) * C, :] = bot.astype(BF16)
        rows = slice(b * C, (b + 1) * C)
        out["b_t"][rows, :] = b_t.astype(BF16)
        out["k_t"][rows, :] = k_t.astype(BF16)
        out["v"][rows, :] = v.astype(BF16)
        out["g_col"][rows, :] = _head_sum(jnp.where(eye_full, g_last, 0.0), lane_a)
        out["bonus"][rows, :] = _head_sum(r * k2 * rk_ref[...], lane_a) * v
        out["gate"][rows, :] = _silu(gate)
        yield

    batches = [one_batch(b) for b in range(RWKV_BATCHES)]
    for _ in range(3):
        for gen in batches:
            next(gen)
            yield


def _rwkv_chunks(buf, state_ref, first_chunk, gng_ref, gnb_ref, o_ref):
    C, N = RWKV_CHUNK, RWKV_HEAD_DIM
    row = lax.broadcasted_iota(jnp.int32, (C, PAIR), 0)
    lane = lax.broadcasted_iota(jnp.int32, (C, PAIR), 1)
    col = lane & (N - 1)
    lane_a = lane < N
    incl = col <= row
    strict = col < row

    units = [(b, p) for b in range(RWKV_BATCHES) for p in range(RWKV_PAIRS)]
    nu = len(units)
    lanes = lambda u: slice(u[1] * PAIR, (u[1] + 1) * PAIR)
    two = lambda name, u: buf[name][2 * u[0] * C:(2 * u[0] + 2) * C, lanes(u)]
    one = lambda name, u: buf[name][u[0] * C:(u[0] + 1) * C, lanes(u)]
    dot_f = lambda a, b: jnp.dot(a, b, preferred_element_type=F32)
    as_keep = lambda mask: jnp.where(mask, 1.0, 0.0).astype(BF16)
    keep_a, keep_b = as_keep(lane_a), as_keep(jnp.logical_not(lane_a))
    bd = lambda x: _block_diag(x, keep_a, keep_b)

    s_bk = [_dot_nt(two("ar", u), jnp.concatenate([bd(one("b_t", u)), bd(one("k_t", u))], axis=0))
            for u in units]
    yield
    s_b = [s[:, :PAIR] for s in s_bk]
    s_k = [s[:, PAIR:] for s in s_bk]
    n_mat = [jnp.where(strict, s[:C], 0.0) for s in s_b]
    n_bf = [n.astype(BF16) for n in n_mat]
    a_ak = [jnp.where(strict, s[:C], 0.0).astype(BF16) for s in s_k]
    a_rb = [jnp.where(incl, s[C:], 0.0).astype(BF16) for s in s_b]
    a_rk = [jnp.where(incl, s[C:], 0.0).astype(BF16) for s in s_k]
    yield

    lvl = ((row >> 1) == (col >> 1)) & ((row & 1) == 1) & ((col & 1) == 0)
    eye_f = (col == row).astype(F32)
    x = [eye_f + jnp.where(lvl, n, 0.0) for n in n_mat]
    s = 2
    while s < C:
        shift = s.bit_length()
        lvl = ((row >> shift) == (col >> shift)) & ((row & s) != 0) & ((col & s) == 0)
        lvl_a, lvl_b = as_keep(lvl & lane_a), as_keep(lvl & jnp.logical_not(lane_a))
        x_bf = [xi.astype(BF16) for xi in x]
        xn = _dots_paired(x_bf, [_block_diag(n, lvl_a, lvl_b) for n in n_bf])
        yield
        xnx = _dots_paired([t.astype(BF16) for t in xn], [bd(t) for t in x_bf])
        x = [x[i] + xnx[i] for i in range(nu)]
        yield
        s *= 2

    x_bf = [xi.astype(BF16) for xi in x]
    v_bd = [bd(one("v", u)) for u in units]
    av = [t.astype(BF16) for t in _dots_paired(a_ak, v_bd)]
    yield
    p_mat = [t.astype(BF16) for t in _dots_paired(x_bf, [bd(two("ar", u)[:C]) for u in units])]
    yield

    t_old = [jnp.where(first_chunk, 0.0, state_ref[u[0], :, lanes(u)]) for u in units]
    t_bd = [bd(t.astype(BF16)) for t in t_old]
    u_mat = _dots_paired(
        [jnp.concatenate([p_mat[i], x_bf[i]], axis=1) for i in range(nu)],
        [jnp.concatenate([t_bd[i], bd(av[i])], axis=0) for i in range(nu)])
    u_bf = [um.astype(BF16) for um in u_mat]
    yield
    y = _dots_paired(
        [jnp.concatenate([two("ar", u)[C:], a_rb[i], a_rk[i]], axis=1)
         for i, u in enumerate(units)],
        [jnp.concatenate([t_bd[i], bd(u_bf[i]), v_bd[i]], axis=0) for i in range(nu)])
    yield
    full = _dots_paired_tn(
        [two("bk_h", u) for u in units],
        [jnp.concatenate([u_bf[i], one("v", u)], axis=0) for i, u in enumerate(units)])
    for i, u in enumerate(units):
        state_ref[u[0], :, lanes(u)] = (one("g_col", u) * t_old[i]
                                        + jnp.where(lane_a, full[i][:N], full[i][N:]))
    yield

    for b in range(RWKV_BATCHES):
        rows = slice(b * C, (b + 1) * C)
        yb = jnp.concatenate(y[b * RWKV_PAIRS:(b + 1) * RWKV_PAIRS], axis=1)
        yc = yb - _head_sum(yb, lane_a) * (1.0 / N)
        var = _head_sum(yc * yc, lane_a) * (1.0 / N)
        yn = yc * lax.rsqrt(var + GN_EPS) * gng_ref[...] + gnb_ref[...]
        o_ref[b] = ((yn + buf["bonus"][rows, :]) * buf["gate"][rows, :]).astype(o_ref.dtype)
        yield


def _rwkv_kernel(chunks, z_ref, mu_ref, w0_ref, w2_ref, a0_ref, a2_ref, kk_ref, ka_ref, rk_ref,
                 gng_ref, gnb_ref, o_ref, state_ref, prev_ref, *bufs):
    names = [name for name, _, _ in RWKV_BUFFERS]
    sets = (dict(zip(names, bufs[:len(names)])), dict(zip(names, bufs[len(names):])))
    par = (mu_ref, w0_ref, w2_ref, a0_ref, a2_ref, kk_ref, ka_ref, rk_ref)
    step = pl.program_id(0)

    @pl.when(step == 0)
    def _():
        for ref in sets[1].values():
            ref[...] = jnp.zeros_like(ref)
        state_ref[...] = jnp.zeros_like(state_ref)
        prev_ref[...] = jnp.zeros_like(prev_ref)

    first_prepared = step % chunks == 0
    first_consumed = step % chunks == 1

    def run(nxt, cur):
        prepare = _rwkv_prepare(z_ref, par, prev_ref, first_prepared, nxt)
        for stage, _ in enumerate(
                _rwkv_chunks(cur, state_ref, first_consumed, gng_ref, gnb_ref, o_ref)):
            if stage in RWKV_PREPARE_AFTER_STAGE:
                next(prepare, None)
        for _ in prepare:
            pass

    @pl.when(step % 2 == 0)
    def _():
        run(sets[0], sets[1])

    @pl.when(step % 2 == 1)
    def _():
        run(sets[1], sets[0])


def _rwkv(z_rwkv, mu, w0, w2, a0, a2, k_k, k_a, r_k, gn_g, gn_b, batch, seq):
    C, NB = RWKV_CHUNK, RWKV_BATCHES
    chunks = seq // C
    n_blocks = (batch // NB) * chunks
    z3 = z_rwkv.reshape(batch, seq, C_RWKV)
    vec = lambda n: pl.BlockSpec((1, n), lambda s: (0, 0))
    lora = pl.BlockSpec((LORA, D_RWKV), lambda s: (0, 0))

    def block_of(s):
        return (s // chunks, s % chunks, 0)

    out = pl.pallas_call(
        functools.partial(_rwkv_kernel, chunks),
        grid=(n_blocks + 1,),
        in_specs=[
            pl.BlockSpec((NB, C, C_RWKV), lambda s: block_of(jnp.minimum(s, n_blocks - 1))),
            vec(C_RWKV), vec(D_RWKV), lora, vec(D_RWKV), lora,
            vec(D_RWKV), vec(D_RWKV), vec(D_RWKV), vec(D_RWKV), vec(D_RWKV),
        ],
        out_specs=pl.BlockSpec((NB, C, D_RWKV), lambda s: block_of(jnp.maximum(s - 1, 0))),
        out_shape=jax.ShapeDtypeStruct((batch, seq, D_RWKV), BF16),
        scratch_shapes=[
            pltpu.VMEM((NB, RWKV_HEAD_DIM, D_RWKV), F32),
            pltpu.VMEM((NB, C_RWKV), F32),
        ] + 2 * [pltpu.VMEM((n, D_RWKV), dt) for _, n, dt in RWKV_BUFFERS],
        compiler_params=pltpu.CompilerParams(
            dimension_semantics=("arbitrary",), vmem_limit_bytes=VMEM_LIMIT_BYTES),
        name="rwkv",
    )(z3, mu, w0, w2, a0, a2, k_k, k_a, r_k, gn_g, gn_b)
    return out.reshape(batch * seq, D_RWKV)


def _kv_proj_kernel(m_ref, g_ref, w_ref, k_ref, v_ref):
    mn = _rms_norm(m_ref[...], g_ref[...]).astype(BF16)
    k_ref[...] = jnp.dot(mn, w_ref[:, :D_MODEL], preferred_element_type=F32).astype(k_ref.dtype)
    v_ref[...] = jnp.dot(mn, w_ref[:, D_MODEL:], preferred_element_type=F32).astype(v_ref.dtype)


def _kv_proj(mem2, g, w_kv):
    m = mem2.shape[0]
    tm = MEM_LEN
    return pl.pallas_call(
        _kv_proj_kernel,
        grid=(m // tm,),
        in_specs=[
            pl.BlockSpec((tm, D_MODEL), lambda i: (i, 0)),
            pl.BlockSpec((1, D_MODEL), lambda i: (0, 0)),
            pl.BlockSpec((D_MODEL, 2 * D_MODEL), lambda i: (0, 0)),
        ],
        out_specs=[
            pl.BlockSpec((tm, D_MODEL), lambda i: (i, 0)),
            pl.BlockSpec((tm, D_MODEL), lambda i: (i, 0)),
        ],
        out_shape=[
            jax.ShapeDtypeStruct((m, D_MODEL), BF16),
            jax.ShapeDtypeStruct((m, D_MODEL), BF16),
        ],
        compiler_params=pltpu.CompilerParams(
            dimension_semantics=("arbitrary",), vmem_limit_bytes=VMEM_LIMIT_BYTES),
        name="kv_proj",
    )(mem2, g, w_kv)


def _out_attn_kernel(x_ref, ys_ref, yr_ref, wo1_ref, wo2_ref, gx_ref, wq_ref, k_ref, v_ref,
                     wo_ref, gf_ref, o_ref):
    h1 = (x_ref[...]
          + jnp.dot(ys_ref[...], wo1_ref[...], preferred_element_type=F32)
          + jnp.dot(yr_ref[...], wo2_ref[...], preferred_element_type=F32))
    hn = _rms_norm(h1, gx_ref[...]).astype(BF16)
    q = jnp.dot(hn, wq_ref[...], preferred_element_type=F32).astype(BF16)
    heads = []
    for h in range(XATTN_HEADS):
        sl = slice(h * XATTN_HEAD_DIM, (h + 1) * XATTN_HEAD_DIM)
        s = _dot_nt(q[:, sl], k_ref[:, sl]) * (XATTN_HEAD_DIM ** -0.5)
        s = s - jnp.max(s, axis=-1, keepdims=True)
        e = jnp.exp(s)
        p = e / jnp.sum(e, axis=-1, keepdims=True)
        heads.append(jnp.dot(p.astype(BF16), v_ref[:, sl], preferred_element_type=F32).astype(BF16))
    o = jnp.concatenate(heads, axis=-1)
    h2 = h1 + jnp.dot(o, wo_ref[...], preferred_element_type=F32)
    o_ref[...] = _rms_norm(h2, gf_ref[...])


def _out_attn(x2, y_sgu, y_rwkv, wo1, wo2, g_x, w_q, k_mem, v_mem, w_o, g_f, seq):
    m = x2.shape[0]
    tq = ATTN_ROWS
    per_batch = seq // tq
    row_blk = lambda n: pl.BlockSpec((tq, n), lambda i: (i, 0))
    full = lambda a, b: pl.BlockSpec((a, b), lambda i: (0, 0))
    mem_blk = pl.BlockSpec((MEM_LEN, D_MODEL), lambda i: (i // per_batch, 0))
    return pl.pallas_call(
        _out_attn_kernel,
        grid=(m // tq,),
        in_specs=[
            row_blk(D_MODEL), row_blk(D_SGU), row_blk(D_RWKV),
            full(D_SGU, D_MODEL), full(D_RWKV, D_MODEL), full(1, D_MODEL),
            full(D_MODEL, D_MODEL), mem_blk, mem_blk, full(D_MODEL, D_MODEL), full(1, D_MODEL),
        ],
        out_specs=row_blk(D_MODEL),
        out_shape=jax.ShapeDtypeStruct((m, D_MODEL), F32),
        compiler_params=pltpu.CompilerParams(
            dimension_semantics=("arbitrary",), vmem_limit_bytes=VMEM_LIMIT_BYTES),
        name="out_attn",
    )(x2, y_sgu, y_rwkv, wo1, wo2, g_x, w_q, k_mem, v_mem, w_o, g_f)


def kernel(x, mem, ln_mix_g, w_in, sgu_ln_g, sgu_ln_b, sgu_ws, sgu_bs, sgu_out_g, rw_mu, rw_w0, rw_w2, rw_a0, rw_a2, rw_k_k, rw_k_a, rw_r_k, rw_gn_g, rw_gn_b, w_out, ln_x_g, ln_mem_g, w_q, w_kv, w_o, ln_f_g):
    batch, seq, _ = x.shape
    assert w_in.shape[0] == 1, "the final norm is fused into the only layer's last call"
    row = lambda a: a.reshape(1, -1)
    h = x.reshape(batch * seq, D_MODEL)
    mem2 = mem.reshape(batch * MEM_LEN, D_MODEL)
    for l in range(1):
        bs_full = jnp.repeat(sgu_bs[l].T, SGU_HEAD_DIM, axis=1)
        y_sgu, z_rwkv = _in_proj(h, row(ln_mix_g[l]), w_in[l].astype(BF16), row(sgu_ln_g[l]),
                                 row(sgu_ln_b[l]), sgu_ws[l], bs_full, row(sgu_out_g[l]))
        y_rwkv = _rwkv(z_rwkv, row(rw_mu[l]), row(rw_w0[l]), rw_w2[l].astype(BF16),
                       row(rw_a0[l]), rw_a2[l].astype(BF16), row(rw_k_k[l]), row(rw_k_a[l]),
                       row(rw_r_k[l]), row(rw_gn_g[l]), row(rw_gn_b[l]), batch, seq)
        k_mem, v_mem = _kv_proj(mem2, row(ln_mem_g[l]), w_kv[l].astype(BF16))
        w_out_b = w_out[l].astype(BF16)
        h = _out_attn(h, y_sgu, y_rwkv, w_out_b[:D_SGU], w_out_b[D_SGU:], row(ln_x_g[l]),
                      w_q[l].astype(BF16), k_mem, v_mem, w_o[l].astype(BF16), row(ln_f_g), seq)
    return h.reshape(batch, seq, D_MODEL)
```

```python
import functools
import math

import jax
import jax.numpy as jnp
from jax import lax
from jax.experimental import pallas as pl
from jax.experimental.pallas import tpu as pltpu

D_MODEL = 1024
MEM_LEN = 256
D_SGU = 512
SGU_HEADS = 4
SGU_HEAD_DIM = D_SGU // SGU_HEADS
SGU_CHUNK = 128
D_RWKV = 512
RWKV_HEAD_DIM = 64
RWKV_HEADS = D_RWKV // RWKV_HEAD_DIM
LORA = 64
C_SGU = 3 * D_SGU
C_RWKV = 4 * D_RWKV + 2 * LORA
XATTN_HEADS = 4
XATTN_HEAD_DIM = D_MODEL // XATTN_HEADS
RMS_EPS = 1e-6
LN_EPS = 1e-5
GN_EPS = 64e-5

RWKV_CHUNK = 64
RWKV_BATCHES = 4
PAIR = 2 * RWKV_HEAD_DIM
RWKV_PAIRS = RWKV_HEADS // 2
IN_PROJ_ROWS = 512
ATTN_ROWS = 512
VMEM_LIMIT_BYTES = 48 * 1024 * 1024

SUBLANES = 8
LOG2_E = math.log2(math.e)

F32 = jnp.float32
BF16 = jnp.bfloat16


def _dot(a, b):
    return jnp.dot(a.astype(BF16), b.astype(BF16), preferred_element_type=F32)


def _dot_nt(a, b):
    return lax.dot_general(a.astype(BF16), b.astype(BF16), (((1,), (1,)), ((), ())),
                           preferred_element_type=F32)


def _dot_tn(a, b):
    return lax.dot_general(a.astype(BF16), b.astype(BF16), (((0,), (0,)), ((), ())),
                           preferred_element_type=F32)


def _rms_norm(x, g):
    return x * lax.rsqrt(jnp.mean(x * x, axis=-1, keepdims=True) + RMS_EPS) * g


def _gelu(x):
    return 0.5 * x * (1.0 + lax.erf(x * (2.0 ** -0.5)))


def _silu(x):
    return x * jax.nn.sigmoid(x)


def _sgu_chunk(z, lng_ref, lnb_ref, ws_c, bs_ref, og_ref):
    u = _gelu(z[:, 0:D_SGU])
    v = _gelu(z[:, D_SGU:2 * D_SGU])
    gate = z[:, 2 * D_SGU:3 * D_SGU]
    vc = v - jnp.mean(v, axis=-1, keepdims=True)
    var = jnp.mean(vc * vc, axis=-1, keepdims=True)
    vn = (vc * lax.rsqrt(var + LN_EPS) * lng_ref[...] + lnb_ref[...]).astype(BF16)
    sv = jnp.concatenate(
        [jnp.dot(ws_c[h], vn[:, h * SGU_HEAD_DIM:(h + 1) * SGU_HEAD_DIM],
                 preferred_element_type=F32) for h in range(SGU_HEADS)], axis=1) + bs_ref[...]
    return _rms_norm(u * sv, og_ref[...]) * _silu(gate)


def _in_proj_kernel(tiles_per_batch, x_ref, g_ref, w_ref, lng_ref, lnb_ref, ws_ref, bs_ref, og_ref,
                    mu_ref, ys_ref, zr_ref, zs_ref, prev_ref):
    tile = pl.program_id(0)

    @pl.when(tile == 0)
    def _():
        prev_ref[...] = jnp.zeros_like(prev_ref)

    first_tile = tile % tiles_per_batch == 0
    row8 = lax.broadcasted_iota(jnp.int32, (SUBLANES, 1), 0)
    xn = _rms_norm(x_ref[...], g_ref[...]).astype(BF16)
    zs_ref[...] = jnp.dot(xn, w_ref[:, :C_SGU], preferred_element_type=F32)
    row = lax.broadcasted_iota(jnp.int32, (SGU_CHUNK, SGU_CHUNK), 0)
    col = lax.broadcasted_iota(jnp.int32, (SGU_CHUNK, SGU_CHUNK), 1)
    ws_c = [jnp.where(col <= row, ws_ref[h], 0.0).astype(BF16) for h in range(SGU_HEADS)]
    n_chunks = IN_PROJ_ROWS // SGU_CHUNK
    lane_tiles = C_RWKV // PAIR
    bounds = [C_SGU + PAIR * (lane_tiles * c // n_chunks) for c in range(n_chunks + 1)]
    gate_cols = slice(3 * D_RWKV, 4 * D_RWKV)
    for c in range(n_chunks):
        z = jnp.dot(xn, w_ref[:, bounds[c]:bounds[c + 1]], preferred_element_type=F32)
        cols = slice(bounds[c] - C_SGU, bounds[c + 1] - C_SGU)
        prev = jnp.where(first_tile, 0.0, prev_ref[:, cols])
        rolled = pltpu.roll(z, 1, axis=0)
        z_prev = jnp.concatenate(
            [jnp.where(row8 == 0, prev, rolled[:SUBLANES]), rolled[SUBLANES:]], axis=0)
        prev_ref[:, cols] = z[IN_PROJ_ROWS - 1:IN_PROJ_ROWS, :]
        zs = z + (z_prev - z) * mu_ref[:, cols]
        if cols.start <= gate_cols.start < cols.stop:
            assert gate_cols.stop <= cols.stop
            lo, hi = gate_cols.start - cols.start, gate_cols.stop - cols.start
            pieces = [zs[:, :lo], _silu(zs[:, lo:hi]), zs[:, hi:]]
            zs = jnp.concatenate([p for p in pieces if p.shape[1]], axis=1)
        zr_ref[:, cols] = zs
        rows = slice(c * SGU_CHUNK, (c + 1) * SGU_CHUNK)
        ys_ref[rows, :] = _sgu_chunk(zs_ref[rows, :], lng_ref, lnb_ref, ws_c, bs_ref,
                                     og_ref).astype(ys_ref.dtype)


def _in_proj(x2, g, w_in, ln_g, ln_b, ws, bs_full, out_g, mu, seq):
    m = x2.shape[0]
    tm = IN_PROJ_ROWS
    vec = pl.BlockSpec((1, D_SGU), lambda i: (0, 0))
    return pl.pallas_call(
        functools.partial(_in_proj_kernel, seq // tm),
        grid=(m // tm,),
        in_specs=[
            pl.BlockSpec((tm, D_MODEL), lambda i: (i, 0)),
            pl.BlockSpec((1, D_MODEL), lambda i: (0, 0)),
            pl.BlockSpec((D_MODEL, C_SGU + C_RWKV), lambda i: (0, 0)),
            vec, vec,
            pl.BlockSpec((SGU_HEADS, SGU_CHUNK, SGU_CHUNK), lambda i: (0, 0, 0)),
            pl.BlockSpec((SGU_CHUNK, D_SGU), lambda i: (0, 0)),
            vec,
            pl.BlockSpec((1, C_RWKV), lambda i: (0, 0)),
        ],
        out_specs=[
            pl.BlockSpec((tm, D_SGU), lambda i: (i, 0)),
            pl.BlockSpec((tm, C_RWKV), lambda i: (i, 0)),
        ],
        out_shape=[
            jax.ShapeDtypeStruct((m, D_SGU), BF16),
            jax.ShapeDtypeStruct((m, C_RWKV), F32),
        ],
        scratch_shapes=[
            pltpu.VMEM((tm, C_SGU), F32),
            pltpu.VMEM((1, C_RWKV), F32),
        ],
        compiler_params=pltpu.CompilerParams(
            dimension_semantics=("arbitrary",), vmem_limit_bytes=VMEM_LIMIT_BYTES),
        name="in_proj",
    )(x2, g, w_in, ln_g, ln_b, ws, bs_full, out_g, mu)


RWKV_ROWS = RWKV_CHUNK * RWKV_BATCHES
RWKV_PREPARE_AFTER_STAGE = (1, 3, 5, 7, 8, 9, 10, 11, 12, 13, 14, 15)
RWKV_BUFFERS = (
    ("ar", 2 * RWKV_ROWS, BF16),
    ("bk_h", 2 * RWKV_ROWS, BF16),
    ("b_t", RWKV_ROWS, BF16),
    ("k_t", RWKV_ROWS, BF16),
    ("v", RWKV_ROWS, BF16),
    ("g_col", RWKV_ROWS, F32),
    ("bonus", RWKV_ROWS, F32),
    ("gate", RWKV_ROWS, F32),
)


def _block_diag(x, keep_a, keep_b):
    return jnp.concatenate([x * keep_a, x * keep_b], axis=0)


def _dots_paired(lhs, rhs):
    out = []
    for i in range(0, len(lhs), 2):
        m, n = lhs[i].shape[0], rhs[i].shape[1]
        both = jnp.dot(jnp.concatenate([lhs[i], lhs[i + 1]], axis=0),
                       jnp.concatenate([rhs[i], rhs[i + 1]], axis=1),
                       preferred_element_type=F32)
        out += [both[:m, :n], both[m:, n:]]
    return out


def _dots_paired_tn(lhs, rhs):
    out = []
    for i in range(0, len(lhs), 2):
        m, n = lhs[i].shape[1], rhs[i].shape[1]
        both = _dot_tn(jnp.concatenate([lhs[i], lhs[i + 1]], axis=1),
                       jnp.concatenate([rhs[i], rhs[i + 1]], axis=1))
        out += [both[:m, :n], both[m:, n:]]
    return out


def _head_sum(x, lane_a):
    outs = []
    for p in range(RWKV_PAIRS):
        xp = x[:, p * PAIR:(p + 1) * PAIR]
        sum_a = jnp.sum(jnp.where(lane_a, xp, 0.0), axis=-1, keepdims=True)
        sum_b = jnp.sum(jnp.where(lane_a, 0.0, xp), axis=-1, keepdims=True)
        outs.append(jnp.where(lane_a, sum_a, sum_b))
    return jnp.concatenate(outs, axis=-1)


def _rwkv_prepare(z_ref, par, out):
    C, N = RWKV_CHUNK, RWKV_HEAD_DIM
    w0_ref, w2_ref, a0_ref, a2_ref, kk_ref, ka_ref, rk_ref = par
    lane_c = lax.broadcasted_iota(jnp.int32, (C, PAIR), 1)
    lane_a = lane_c < N
    eye = (lane_c & (N - 1)) == lax.broadcasted_iota(jnp.int32, (C, PAIR), 0)
    eye_full = jnp.concatenate([eye] * RWKV_PAIRS, axis=1)
    row_c = lax.broadcasted_iota(jnp.int32, (C, C), 0)
    col_c = lax.broadcasted_iota(jnp.int32, (C, C), 1)
    tril = (col_c <= row_c).astype(F32).astype(BF16)
    tril3 = jnp.concatenate([tril, tril, tril], axis=1)

    def one_batch(b):
        r = z_ref[b, :, 0:D_RWKV]
        k = z_ref[b, :, D_RWKV:2 * D_RWKV]
        v = z_ref[b, :, 2 * D_RWKV:3 * D_RWKV]
        gate_act = z_ref[b, :, 3 * D_RWKV:4 * D_RWKV]
        wd = z_ref[b, :, 4 * D_RWKV:4 * D_RWKV + LORA]
        ad = z_ref[b, :, 4 * D_RWKV + LORA:4 * D_RWKV + 2 * LORA]

        lw = w0_ref[...] + _dot(jnp.tanh(wd), w2_ref[...])
        ld = -(math.exp(-0.5) * LOG2_E) * jax.nn.sigmoid(lw)
        icl = jax.nn.sigmoid(a0_ref[...] + _dot(ad, a2_ref[...]))
        yield
        k2 = k * ((1.0 - ka_ref[...]) + ka_ref[...] * icl)
        kk = k * kk_ref[...]
        kk = kk * lax.rsqrt(jnp.maximum(_head_sum(kk * kk, lane_a), 1e-24))
        bvec = kk * icl

        p1 = ld.astype(BF16)
        r1 = ld - p1.astype(F32)
        p2 = r1.astype(BF16)
        p3 = (r1 - p2.astype(F32)).astype(BF16)
        cum = jnp.dot(tril3, jnp.concatenate([p1, p2, p3], axis=0), preferred_element_type=F32)
        last = cum[C - 1:C, :]
        yield
        e_neg = jnp.exp2(-cum)
        g_last = jnp.exp2(last)
        a_t = -kk * jnp.exp2(cum - ld)
        r_t = r * jnp.exp2(cum)
        b_t = bvec * e_neg
        k_t = k2 * e_neg

        stacked = {
            "ar": (a_t, r_t),
            "bk_h": (b_t * g_last, k_t * g_last),
        }
        for name, (top, bot) in stacked.items():
            out[name][2 * b * C:(2 * b + 1) * C, :] = top.astype(BF16)
            out[name][(2 * b + 1) * C:(2 * b + 2) * C, :] = bot.astype(BF16)
        rows = slice(b * C, (b + 1) * C)
        out["b_t"][rows, :] = b_t.astype(BF16)
        out["k_t"][rows, :] = k_t.astype(BF16)
        out["v"][rows, :] = v.astype(BF16)
        out["g_col"][rows, :] = _head_sum(jnp.where(eye_full, g_last, 0.0), lane_a)
        out["bonus"][rows, :] = _head_sum(r * k2 * rk_ref[...], lane_a) * v
        out["gate"][rows, :] = gate_act
        yield

    batches = [one_batch(b) for b in range(RWKV_BATCHES)]
    for _ in range(3):
        for gen in batches:
            next(gen)
            yield


def _rwkv_chunks(buf, state_ref, first_chunk, gng_ref, gnb_ref, o_ref):
    C, N = RWKV_CHUNK, RWKV_HEAD_DIM
    row = lax.broadcasted_iota(jnp.int32, (C, PAIR), 0)
    lane = lax.broadcasted_iota(jnp.int32, (C, PAIR), 1)
    col = lane & (N - 1)
    lane_a = lane < N
    incl = col <= row
    strict = col < row

    units = [(b, p) for b in range(RWKV_BATCHES) for p in range(RWKV_PAIRS)]
    nu = len(units)
    lanes = lambda u: slice(u[1] * PAIR, (u[1] + 1) * PAIR)
    two = lambda name, u: buf[name][2 * u[0] * C:(2 * u[0] + 2) * C, lanes(u)]
    one = lambda name, u: buf[name][u[0] * C:(u[0] + 1) * C, lanes(u)]
    dot_f = lambda a, b: jnp.dot(a, b, preferred_element_type=F32)
    as_keep = lambda mask: jnp.where(mask, 1.0, 0.0).astype(BF16)
    keep_a, keep_b = as_keep(lane_a), as_keep(jnp.logical_not(lane_a))
    bd = lambda x: _block_diag(x, keep_a, keep_b)

    s_bk = [_dot_nt(two("ar", u), jnp.concatenate([bd(one("b_t", u)), bd(one("k_t", u))], axis=0))
            for u in units]
    yield
    s_b = [s[:, :PAIR] for s in s_bk]
    s_k = [s[:, PAIR:] for s in s_bk]
    n_mat = [jnp.where(strict, s[:C], 0.0) for s in s_b]
    n_bf = [n.astype(BF16) for n in n_mat]
    a_ak = [jnp.where(strict, s[:C], 0.0).astype(BF16) for s in s_k]
    a_rb = [jnp.where(incl, s[C:], 0.0).astype(BF16) for s in s_b]
    a_rk = [jnp.where(incl, s[C:], 0.0).astype(BF16) for s in s_k]
    yield

    lvl = ((row >> 1) == (col >> 1)) & ((row & 1) == 1) & ((col & 1) == 0)
    eye_f = (col == row).astype(F32)
    x = [eye_f + jnp.where(lvl, n, 0.0) for n in n_mat]
    s = 2
    while s < C:
        shift = s.bit_length()
        lvl = ((row >> shift) == (col >> shift)) & ((row & s) != 0) & ((col & s) == 0)
        lvl_a, lvl_b = as_keep(lvl & lane_a), as_keep(lvl & jnp.logical_not(lane_a))
        x_bf = [xi.astype(BF16) for xi in x]
        xn = _dots_paired(x_bf, [_block_diag(n, lvl_a, lvl_b) for n in n_bf])
        yield
        xnx = _dots_paired([t.astype(BF16) for t in xn], [bd(t) for t in x_bf])
        x = [x[i] + xnx[i] for i in range(nu)]
        yield
        s *= 2

    x_bf = [xi.astype(BF16) for xi in x]
    v_bd = [bd(one("v", u)) for u in units]
    av = [t.astype(BF16) for t in _dots_paired(a_ak, v_bd)]
    yield
    p_mat = [t.astype(BF16) for t in _dots_paired(x_bf, [bd(two("ar", u)[:C]) for u in units])]
    yield

    t_old = [jnp.where(first_chunk, 0.0, state_ref[u[0], :, lanes(u)]) for u in units]
    t_bd = [bd(t.astype(BF16)) for t in t_old]
    u_mat = _dots_paired(
        [jnp.concatenate([p_mat[i], x_bf[i]], axis=1) for i in range(nu)],
        [jnp.concatenate([t_bd[i], bd(av[i])], axis=0) for i in range(nu)])
    u_bf = [um.astype(BF16) for um in u_mat]
    yield
    y = _dots_paired(
        [jnp.concatenate([two("ar", u)[C:], a_rb[i], a_rk[i]], axis=1)
         for i, u in enumerate(units)],
        [jnp.concatenate([t_bd[i], bd(u_bf[i]), v_bd[i]], axis=0) for i in range(nu)])
    yield
    full = _dots_paired_tn(
        [two("bk_h", u) for u in units],
        [jnp.concatenate([u_bf[i], one("v", u)], axis=0) for i, u in enumerate(units)])
    for i, u in enumerate(units):
        state_ref[u[0], :, lanes(u)] = (one("g_col", u) * t_old[i]
                                        + jnp.where(lane_a, full[i][:N], full[i][N:]))
    yield

    for b in range(RWKV_BATCHES):
        rows = slice(b * C, (b + 1) * C)
        yb = jnp.concatenate(y[b * RWKV_PAIRS:(b + 1) * RWKV_PAIRS], axis=1)
        yc = yb - _head_sum(yb, lane_a) * (1.0 / N)
        var = _head_sum(yc * yc, lane_a) * (1.0 / N)
        yn = yc * lax.rsqrt(var + GN_EPS) * gng_ref[...] + gnb_ref[...]
        o_ref[b] = ((yn + buf["bonus"][rows, :]) * buf["gate"][rows, :]).astype(o_ref.dtype)
        yield


def _rwkv_kernel(chunks, z_ref, w0_ref, w2_ref, a0_ref, a2_ref, kk_ref, ka_ref, rk_ref,
                 gng_ref, gnb_ref, o_ref, state_ref, *bufs):
    names = [name for name, _, _ in RWKV_BUFFERS]
    sets = (dict(zip(names, bufs[:len(names)])), dict(zip(names, bufs[len(names):])))
    par = (w0_ref, w2_ref, a0_ref, a2_ref, kk_ref, ka_ref, rk_ref)
    step = pl.program_id(0)

    @pl.when(step == 0)
    def _():
        for ref in sets[1].values():
            ref[...] = jnp.zeros_like(ref)
        state_ref[...] = jnp.zeros_like(state_ref)

    first_consumed = step % chunks == 1

    def run(nxt, cur):
        prepare = _rwkv_prepare(z_ref, par, nxt)
        for stage, _ in enumerate(
                _rwkv_chunks(cur, state_ref, first_consumed, gng_ref, gnb_ref, o_ref)):
            if stage in RWKV_PREPARE_AFTER_STAGE:
                next(prepare, None)
        for _ in prepare:
            pass

    @pl.when(step % 2 == 0)
    def _():
        run(sets[0], sets[1])

    @pl.when(step % 2 == 1)
    def _():
        run(sets[1], sets[0])


def _rwkv(z_rwkv, w0, w2, a0, a2, k_k, k_a, r_k, gn_g, gn_b, batch, seq):
    C, NB = RWKV_CHUNK, RWKV_BATCHES
    chunks = seq // C
    n_blocks = (batch // NB) * chunks
    z3 = z_rwkv.reshape(batch, seq, C_RWKV)
    vec = lambda n: pl.BlockSpec((1, n), lambda s: (0, 0))
    lora = pl.BlockSpec((LORA, D_RWKV), lambda s: (0, 0))

    def block_of(s):
        return (s // chunks, s % chunks, 0)

    out = pl.pallas_call(
        functools.partial(_rwkv_kernel, chunks),
        grid=(n_blocks + 1,),
        in_specs=[
            pl.BlockSpec((NB, C, C_RWKV), lambda s: block_of(jnp.minimum(s, n_blocks - 1))),
            vec(D_RWKV), lora, vec(D_RWKV), lora,
            vec(D_RWKV), vec(D_RWKV), vec(D_RWKV), vec(D_RWKV), vec(D_RWKV),
        ],
        out_specs=pl.BlockSpec((NB, C, D_RWKV), lambda s: block_of(jnp.maximum(s - 1, 0))),
        out_shape=jax.ShapeDtypeStruct((batch, seq, D_RWKV), BF16),
        scratch_shapes=[
            pltpu.VMEM((NB, RWKV_HEAD_DIM, D_RWKV), F32),
        ] + 2 * [pltpu.VMEM((n, D_RWKV), dt) for _, n, dt in RWKV_BUFFERS],
        compiler_params=pltpu.CompilerParams(
            dimension_semantics=("arbitrary",), vmem_limit_bytes=VMEM_LIMIT_BYTES),
        name="rwkv",
    )(z3, w0, w2, a0, a2, k_k, k_a, r_k, gn_g, gn_b)
    return out.reshape(batch * seq, D_RWKV)


def _kv_proj_kernel(m_ref, g_ref, w_ref, k_ref, v_ref):
    mn = _rms_norm(m_ref[...], g_ref[...]).astype(BF16)
    k_ref[...] = jnp.dot(mn, w_ref[:, :D_MODEL], preferred_element_type=F32).astype(k_ref.dtype)
    v_ref[...] = jnp.dot(mn, w_ref[:, D_MODEL:], preferred_element_type=F32).astype(v_ref.dtype)


def _kv_proj(mem2, g, w_kv):
    m = mem2.shape[0]
    tm = MEM_LEN
    return pl.pallas_call(
        _kv_proj_kernel,
        grid=(m // tm,),
        in_specs=[
            pl.BlockSpec((tm, D_MODEL), lambda i: (i, 0)),
            pl.BlockSpec((1, D_MODEL), lambda i: (0, 0)),
            pl.BlockSpec((D_MODEL, 2 * D_MODEL), lambda i: (0, 0)),
        ],
        out_specs=[
            pl.BlockSpec((tm, D_MODEL), lambda i: (i, 0)),
            pl.BlockSpec((tm, D_MODEL), lambda i: (i, 0)),
        ],
        out_shape=[
            jax.ShapeDtypeStruct((m, D_MODEL), BF16),
            jax.ShapeDtypeStruct((m, D_MODEL), BF16),
        ],
        compiler_params=pltpu.CompilerParams(
            dimension_semantics=("arbitrary",), vmem_limit_bytes=VMEM_LIMIT_BYTES),
        name="kv_proj",
    )(mem2, g, w_kv)


def _out_attn_kernel(x_ref, ys_ref, yr_ref, wo1_ref, wo2_ref, gx_ref, wq_ref, k_ref, v_ref,
                     wo_ref, gf_ref, o_ref):
    h1 = (x_ref[...]
          + jnp.dot(ys_ref[...], wo1_ref[...], preferred_element_type=F32)
          + jnp.dot(yr_ref[...], wo2_ref[...], preferred_element_type=F32))
    hn = _rms_norm(h1, gx_ref[...]).astype(BF16)
    q = jnp.dot(hn, wq_ref[...], preferred_element_type=F32).astype(BF16)
    heads = []
    for h in range(XATTN_HEADS):
        sl = slice(h * XATTN_HEAD_DIM, (h + 1) * XATTN_HEAD_DIM)
        s = _dot_nt(q[:, sl], k_ref[:, sl]) * (XATTN_HEAD_DIM ** -0.5)
        s = s - jnp.max(s, axis=-1, keepdims=True)
        e = jnp.exp(s)
        p = e / jnp.sum(e, axis=-1, keepdims=True)
        heads.append(jnp.dot(p.astype(BF16), v_ref[:, sl], preferred_element_type=F32).astype(BF16))
    o = jnp.concatenate(heads, axis=-1)
    h2 = h1 + jnp.dot(o, wo_ref[...], preferred_element_type=F32)
    o_ref[...] = _rms_norm(h2, gf_ref[...])


def _out_attn(x2, y_sgu, y_rwkv, wo1, wo2, g_x, w_q, k_mem, v_mem, w_o, g_f, seq):
    m = x2.shape[0]
    tq = ATTN_ROWS
    per_batch = seq // tq
    row_blk = lambda n: pl.BlockSpec((tq, n), lambda i: (i, 0))
    full = lambda a, b: pl.BlockSpec((a, b), lambda i: (0, 0))
    mem_blk = pl.BlockSpec((MEM_LEN, D_MODEL), lambda i: (i // per_batch, 0))
    return pl.pallas_call(
        _out_attn_kernel,
        grid=(m // tq,),
        in_specs=[
            row_blk(D_MODEL), row_blk(D_SGU), row_blk(D_RWKV),
            full(D_SGU, D_MODEL), full(D_RWKV, D_MODEL), full(1, D_MODEL),
            full(D_MODEL, D_MODEL), mem_blk, mem_blk, full(D_MODEL, D_MODEL), full(1, D_MODEL),
        ],
        out_specs=row_blk(D_MODEL),
        out_shape=jax.ShapeDtypeStruct((m, D_MODEL), F32),
        compiler_params=pltpu.CompilerParams(
            dimension_semantics=("arbitrary",), vmem_limit_bytes=VMEM_LIMIT_BYTES),
        name="out_attn",
    )(x2, y_sgu, y_rwkv, wo1, wo2, g_x, w_q, k_mem, v_mem, w_o, g_f)


def kernel(x, mem, ln_mix_g, w_in, sgu_ln_g, sgu_ln_b, sgu_ws, sgu_bs, sgu_out_g, rw_mu, rw_w0, rw_w2, rw_a0, rw_a2, rw_k_k, rw_k_a, rw_r_k, rw_gn_g, rw_gn_b, w_out, ln_x_g, ln_mem_g, w_q, w_kv, w_o, ln_f_g):
    batch, seq, _ = x.shape
    assert w_in.shape[0] == 1, "the final norm is fused into the only layer's last call"
    row = lambda a: a.reshape(1, -1)
    h = x.reshape(batch * seq, D_MODEL)
    mem2 = mem.reshape(batch * MEM_LEN, D_MODEL)
    for l in range(1):
        bs_full = jnp.repeat(sgu_bs[l].T, SGU_HEAD_DIM, axis=1)
        y_sgu, z_rwkv = _in_proj(h, row(ln_mix_g[l]), w_in[l].astype(BF16), row(sgu_ln_g[l]),
                                 row(sgu_ln_b[l]), sgu_ws[l], bs_full, row(sgu_out_g[l]),
                                 row(rw_mu[l]), seq)
        y_rwkv = _rwkv(z_rwkv, row(rw_w0[l]), rw_w2[l].astype(BF16),
                       row(rw_a0[l]), rw_a2[l].astype(BF16), row(rw_k_k[l]), row(rw_k_a[l]),
                       row(rw_r_k[l]), row(rw_gn_g[l]), row(rw_gn_b[l]), batch, seq)
        k_mem, v_mem = _kv_proj(mem2, row(ln_mem_g[l]), w_kv[l].astype(BF16))
        w_out_b = w_out[l].astype(BF16)
        h = _out_attn(h, y_sgu, y_rwkv, w_out_b[:D_SGU], w_out_b[D_SGU:], row(ln_x_g[l]),
                      w_q[l].astype(BF16), k_mem, v_mem, w_o[l].astype(BF16), row(ln_f_g), seq)
    return h.reshape(batch, seq, D_MODEL)
```

```python
import functools
import math

import jax
import jax.numpy as jnp
from jax import lax
from jax.experimental import pallas as pl
from jax.experimental.pallas import tpu as pltpu

D_MODEL = 1024
MEM_LEN = 256
D_SGU = 512
SGU_HEADS = 4
SGU_HEAD_DIM = D_SGU // SGU_HEADS
SGU_CHUNK = 128
D_RWKV = 512
RWKV_HEAD_DIM = 64
RWKV_HEADS = D_RWKV // RWKV_HEAD_DIM
LORA = 64
C_SGU = 3 * D_SGU
C_RWKV = 4 * D_RWKV + 2 * LORA
XATTN_HEADS = 4
XATTN_HEAD_DIM = D_MODEL // XATTN_HEADS
RMS_EPS = 1e-6
LN_EPS = 1e-5
GN_EPS = 64e-5

RWKV_CHUNK = 64
RWKV_BATCHES = 8
PAIR = 2 * RWKV_HEAD_DIM
RWKV_PAIRS = RWKV_HEADS // 2
IN_PROJ_ROWS = 512
ATTN_ROWS = 512
VMEM_LIMIT_BYTES = 48 * 1024 * 1024

SUBLANES = 8
LOG2_E = math.log2(math.e)

F32 = jnp.float32
BF16 = jnp.bfloat16


def _dot(a, b):
    return jnp.dot(a.astype(BF16), b.astype(BF16), preferred_element_type=F32)


def _dot_nt(a, b):
    return lax.dot_general(a.astype(BF16), b.astype(BF16), (((1,), (1,)), ((), ())),
                           preferred_element_type=F32)


def _dot_tn(a, b):
    return lax.dot_general(a.astype(BF16), b.astype(BF16), (((0,), (0,)), ((), ())),
                           preferred_element_type=F32)


def _rms_norm(x, g):
    return x * lax.rsqrt(jnp.mean(x * x, axis=-1, keepdims=True) + RMS_EPS) * g


def _gelu(x):
    return 0.5 * x * (1.0 + lax.erf(x * (2.0 ** -0.5)))


def _silu(x):
    return x * jax.nn.sigmoid(x)


def _sgu_chunk(z, lng_ref, lnb_ref, ws_c, bs_ref, og_ref):
    u = _gelu(z[:, 0:D_SGU])
    v = _gelu(z[:, D_SGU:2 * D_SGU])
    gate = z[:, 2 * D_SGU:3 * D_SGU]
    vc = v - jnp.mean(v, axis=-1, keepdims=True)
    var = jnp.mean(vc * vc, axis=-1, keepdims=True)
    vn = (vc * lax.rsqrt(var + LN_EPS) * lng_ref[...] + lnb_ref[...]).astype(BF16)
    sv = jnp.concatenate(
        [jnp.dot(ws_c[h], vn[:, h * SGU_HEAD_DIM:(h + 1) * SGU_HEAD_DIM],
                 preferred_element_type=F32) for h in range(SGU_HEADS)], axis=1) + bs_ref[...]
    return _rms_norm(u * sv, og_ref[...]) * _silu(gate)


def _in_proj_kernel(tiles_per_batch, x_ref, g_ref, w_ref, lng_ref, lnb_ref, ws_ref, bs_ref, og_ref,
                    mu_ref, ys_ref, zr_ref, zs_ref, prev_ref):
    tile = pl.program_id(0)

    @pl.when(tile == 0)
    def _():
        prev_ref[...] = jnp.zeros_like(prev_ref)

    first_tile = tile % tiles_per_batch == 0
    row8 = lax.broadcasted_iota(jnp.int32, (SUBLANES, 1), 0)
    xn = _rms_norm(x_ref[...], g_ref[...]).astype(BF16)
    zs_ref[...] = jnp.dot(xn, w_ref[:, :C_SGU], preferred_element_type=F32)
    row = lax.broadcasted_iota(jnp.int32, (SGU_CHUNK, SGU_CHUNK), 0)
    col = lax.broadcasted_iota(jnp.int32, (SGU_CHUNK, SGU_CHUNK), 1)
    ws_c = [jnp.where(col <= row, ws_ref[h], 0.0).astype(BF16) for h in range(SGU_HEADS)]
    n_chunks = IN_PROJ_ROWS // SGU_CHUNK
    lane_tiles = C_RWKV // PAIR
    bounds = [C_SGU + PAIR * (lane_tiles * c // n_chunks) for c in range(n_chunks + 1)]
    gate_cols = slice(3 * D_RWKV, 4 * D_RWKV)
    for c in range(n_chunks):
        z = jnp.dot(xn, w_ref[:, bounds[c]:bounds[c + 1]], preferred_element_type=F32)
        cols = slice(bounds[c] - C_SGU, bounds[c + 1] - C_SGU)
        prev = jnp.where(first_tile, 0.0, prev_ref[:, cols])
        rolled = pltpu.roll(z, 1, axis=0)
        z_prev = jnp.concatenate(
            [jnp.where(row8 == 0, prev, rolled[:SUBLANES]), rolled[SUBLANES:]], axis=0)
        prev_ref[:, cols] = z[IN_PROJ_ROWS - 1:IN_PROJ_ROWS, :]
        zs = z + (z_prev - z) * mu_ref[:, cols]
        if cols.start <= gate_cols.start < cols.stop:
            assert gate_cols.stop <= cols.stop
            lo, hi = gate_cols.start - cols.start, gate_cols.stop - cols.start
            pieces = [zs[:, :lo], _silu(zs[:, lo:hi]), zs[:, hi:]]
            zs = jnp.concatenate([p for p in pieces if p.shape[1]], axis=1)
        zr_ref[:, cols] = zs
        rows = slice(c * SGU_CHUNK, (c + 1) * SGU_CHUNK)
        ys_ref[rows, :] = _sgu_chunk(zs_ref[rows, :], lng_ref, lnb_ref, ws_c, bs_ref,
                                     og_ref).astype(ys_ref.dtype)


def _in_proj(x2, g, w_in, ln_g, ln_b, ws, bs_full, out_g, mu, seq):
    m = x2.shape[0]
    tm = IN_PROJ_ROWS
    vec = pl.BlockSpec((1, D_SGU), lambda i: (0, 0))
    return pl.pallas_call(
        functools.partial(_in_proj_kernel, seq // tm),
        grid=(m // tm,),
        in_specs=[
            pl.BlockSpec((tm, D_MODEL), lambda i: (i, 0)),
            pl.BlockSpec((1, D_MODEL), lambda i: (0, 0)),
            pl.BlockSpec((D_MODEL, C_SGU + C_RWKV), lambda i: (0, 0)),
            vec, vec,
            pl.BlockSpec((SGU_HEADS, SGU_CHUNK, SGU_CHUNK), lambda i: (0, 0, 0)),
            pl.BlockSpec((SGU_CHUNK, D_SGU), lambda i: (0, 0)),
            vec,
            pl.BlockSpec((1, C_RWKV), lambda i: (0, 0)),
        ],
        out_specs=[
            pl.BlockSpec((tm, D_SGU), lambda i: (i, 0)),
            pl.BlockSpec((tm, C_RWKV), lambda i: (i, 0)),
        ],
        out_shape=[
            jax.ShapeDtypeStruct((m, D_SGU), BF16),
            jax.ShapeDtypeStruct((m, C_RWKV), F32),
        ],
        scratch_shapes=[
            pltpu.VMEM((tm, C_SGU), F32),
            pltpu.VMEM((1, C_RWKV), F32),
        ],
        compiler_params=pltpu.CompilerParams(
            dimension_semantics=("arbitrary",), vmem_limit_bytes=VMEM_LIMIT_BYTES),
        name="in_proj",
    )(x2, g, w_in, ln_g, ln_b, ws, bs_full, out_g, mu)


RWKV_ROWS = RWKV_CHUNK * RWKV_BATCHES
RWKV_MATMUL_STAGES = 16
RWKV_PREPARE_PIECES = 3
RWKV_BUFFERS = (
    ("ar", 2 * RWKV_ROWS, BF16),
    ("bk_h", 2 * RWKV_ROWS, BF16),
    ("b_t", RWKV_ROWS, BF16),
    ("k_t", RWKV_ROWS, BF16),
    ("v", RWKV_ROWS, BF16),
    ("g_col", RWKV_ROWS, F32),
    ("bonus", RWKV_ROWS, F32),
    ("gate", RWKV_ROWS, F32),
)


def _block_diag(x, keep_a, keep_b):
    return jnp.concatenate([x * keep_a, x * keep_b], axis=0)


def _dots_paired(lhs, rhs):
    out = []
    for i in range(0, len(lhs), 2):
        m, n = lhs[i].shape[0], rhs[i].shape[1]
        both = jnp.dot(jnp.concatenate([lhs[i], lhs[i + 1]], axis=0),
                       jnp.concatenate([rhs[i], rhs[i + 1]], axis=1),
                       preferred_element_type=F32)
        out += [both[:m, :n], both[m:, n:]]
    return out


def _dots_paired_tn(lhs, rhs):
    out = []
    for i in range(0, len(lhs), 2):
        m, n = lhs[i].shape[1], rhs[i].shape[1]
        both = _dot_tn(jnp.concatenate([lhs[i], lhs[i + 1]], axis=1),
                       jnp.concatenate([rhs[i], rhs[i + 1]], axis=1))
        out += [both[:m, :n], both[m:, n:]]
    return out


def _head_sum(x, lane_a):
    outs = []
    for p in range(RWKV_PAIRS):
        xp = x[:, p * PAIR:(p + 1) * PAIR]
        sum_a = jnp.sum(jnp.where(lane_a, xp, 0.0), axis=-1, keepdims=True)
        sum_b = jnp.sum(jnp.where(lane_a, 0.0, xp), axis=-1, keepdims=True)
        outs.append(jnp.where(lane_a, sum_a, sum_b))
    return jnp.concatenate(outs, axis=-1)


def _rwkv_prepare(z_ref, par, out):
    C, N = RWKV_CHUNK, RWKV_HEAD_DIM
    w0_ref, w2_ref, a0_ref, a2_ref, kk_ref, ka_ref, rk_ref = par
    lane_c = lax.broadcasted_iota(jnp.int32, (C, PAIR), 1)
    lane_a = lane_c < N
    eye = (lane_c & (N - 1)) == lax.broadcasted_iota(jnp.int32, (C, PAIR), 0)
    eye_full = jnp.concatenate([eye] * RWKV_PAIRS, axis=1)
    row_c = lax.broadcasted_iota(jnp.int32, (C, C), 0)
    col_c = lax.broadcasted_iota(jnp.int32, (C, C), 1)
    tril = (col_c <= row_c).astype(F32).astype(BF16)
    tril3 = jnp.concatenate([tril, tril, tril], axis=1)

    def one_batch(b):
        r = z_ref[b, :, 0:D_RWKV]
        k = z_ref[b, :, D_RWKV:2 * D_RWKV]
        v = z_ref[b, :, 2 * D_RWKV:3 * D_RWKV]
        gate_act = z_ref[b, :, 3 * D_RWKV:4 * D_RWKV]
        wd = z_ref[b, :, 4 * D_RWKV:4 * D_RWKV + LORA]
        ad = z_ref[b, :, 4 * D_RWKV + LORA:4 * D_RWKV + 2 * LORA]

        lw = w0_ref[...] + _dot(jnp.tanh(wd), w2_ref[...])
        ld = -(math.exp(-0.5) * LOG2_E) * jax.nn.sigmoid(lw)
        icl = jax.nn.sigmoid(a0_ref[...] + _dot(ad, a2_ref[...]))
        yield
        k2 = k * ((1.0 - ka_ref[...]) + ka_ref[...] * icl)
        kk = k * kk_ref[...]
        kk = kk * lax.rsqrt(jnp.maximum(_head_sum(kk * kk, lane_a), 1e-24))
        bvec = kk * icl

        p1 = ld.astype(BF16)
        r1 = ld - p1.astype(F32)
        p2 = r1.astype(BF16)
        p3 = (r1 - p2.astype(F32)).astype(BF16)
        cum = jnp.dot(tril3, jnp.concatenate([p1, p2, p3], axis=0), preferred_element_type=F32)
        last = cum[C - 1:C, :]
        yield
        e_neg = jnp.exp2(-cum)
        g_last = jnp.exp2(last)
        a_t = -kk * jnp.exp2(cum - ld)
        r_t = r * jnp.exp2(cum)
        b_t = bvec * e_neg
        k_t = k2 * e_neg

        stacked = {
            "ar": (a_t, r_t),
            "bk_h": (b_t * g_last, k_t * g_last),
        }
        for name, (top, bot) in stacked.items():
            out[name][2 * b * C:(2 * b + 1) * C, :] = top.astype(BF16)
            out[name][(2 * b + 1) * C:(2 * b + 2) * C, :] = bot.astype(BF16)
        rows = slice(b * C, (b + 1) * C)
        out["b_t"][rows, :] = b_t.astype(BF16)
        out["k_t"][rows, :] = k_t.astype(BF16)
        out["v"][rows, :] = v.astype(BF16)
        out["g_col"][rows, :] = _head_sum(jnp.where(eye_full, g_last, 0.0), lane_a)
        out["bonus"][rows, :] = _head_sum(r * k2 * rk_ref[...], lane_a) * v
        out["gate"][rows, :] = gate_act
        yield

    batches = [one_batch(b) for b in range(RWKV_BATCHES)]
    for _ in range(RWKV_PREPARE_PIECES):
        for gen in batches:
            next(gen)
            yield


def _rwkv_chunks(buf, state_ref, first_chunk, gng_ref, gnb_ref, o_ref):
    C, N = RWKV_CHUNK, RWKV_HEAD_DIM
    row = lax.broadcasted_iota(jnp.int32, (C, PAIR), 0)
    lane = lax.broadcasted_iota(jnp.int32, (C, PAIR), 1)
    col = lane & (N - 1)
    lane_a = lane < N
    incl = col <= row
    strict = col < row

    units = [(b, p) for b in range(RWKV_BATCHES) for p in range(RWKV_PAIRS)]
    nu = len(units)
    lanes = lambda u: slice(u[1] * PAIR, (u[1] + 1) * PAIR)
    two = lambda name, u: buf[name][2 * u[0] * C:(2 * u[0] + 2) * C, lanes(u)]
    one = lambda name, u: buf[name][u[0] * C:(u[0] + 1) * C, lanes(u)]
    as_keep = lambda mask: jnp.where(mask, 1.0, 0.0).astype(BF16)
    keep_a, keep_b = as_keep(lane_a), as_keep(jnp.logical_not(lane_a))
    bd = lambda x: _block_diag(x, keep_a, keep_b)

    s_bk = [_dot_nt(two("ar", u), jnp.concatenate([bd(one("b_t", u)), bd(one("k_t", u))], axis=0))
            for u in units]
    yield
    s_b = [s[:, :PAIR] for s in s_bk]
    s_k = [s[:, PAIR:] for s in s_bk]
    n_mat = [jnp.where(strict, s[:C], 0.0) for s in s_b]
    n_bf = [n.astype(BF16) for n in n_mat]
    a_ak = [jnp.where(strict, s[:C], 0.0).astype(BF16) for s in s_k]
    a_rb = [jnp.where(incl, s[C:], 0.0).astype(BF16) for s in s_b]
    a_rk = [jnp.where(incl, s[C:], 0.0).astype(BF16) for s in s_k]
    yield

    lvl = ((row >> 1) == (col >> 1)) & ((row & 1) == 1) & ((col & 1) == 0)
    eye_f = (col == row).astype(F32)
    x = [eye_f + jnp.where(lvl, n, 0.0) for n in n_mat]
    s = 2
    while s < C:
        shift = s.bit_length()
        lvl = ((row >> shift) == (col >> shift)) & ((row & s) != 0) & ((col & s) == 0)
        lvl_a, lvl_b = as_keep(lvl & lane_a), as_keep(lvl & jnp.logical_not(lane_a))
        x_bf = [xi.astype(BF16) for xi in x]
        xn = _dots_paired(x_bf, [_block_diag(n, lvl_a, lvl_b) for n in n_bf])
        yield
        xnx = _dots_paired([t.astype(BF16) for t in xn], [bd(t) for t in x_bf])
        x = [x[i] + xnx[i] for i in range(nu)]
        yield
        s *= 2

    x_bf = [xi.astype(BF16) for xi in x]
    v_bd = [bd(one("v", u)) for u in units]
    av = [t.astype(BF16) for t in _dots_paired(a_ak, v_bd)]
    yield
    p_mat = [t.astype(BF16) for t in _dots_paired(x_bf, [bd(two("ar", u)[:C]) for u in units])]
    yield

    t_old = [jnp.where(first_chunk, 0.0, state_ref[u[0], :, lanes(u)]) for u in units]
    t_bd = [bd(t.astype(BF16)) for t in t_old]
    u_mat = _dots_paired(
        [jnp.concatenate([p_mat[i], x_bf[i]], axis=1) for i in range(nu)],
        [jnp.concatenate([t_bd[i], bd(av[i])], axis=0) for i in range(nu)])
    u_bf = [um.astype(BF16) for um in u_mat]
    yield
    y = _dots_paired(
        [jnp.concatenate([two("ar", u)[C:], a_rb[i], a_rk[i]], axis=1)
         for i, u in enumerate(units)],
        [jnp.concatenate([t_bd[i], bd(u_bf[i]), v_bd[i]], axis=0) for i in range(nu)])
    yield
    full = _dots_paired_tn(
        [two("bk_h", u) for u in units],
        [jnp.concatenate([u_bf[i], one("v", u)], axis=0) for i, u in enumerate(units)])
    for i, u in enumerate(units):
        state_ref[u[0], :, lanes(u)] = (one("g_col", u) * t_old[i]
                                        + jnp.where(lane_a, full[i][:N], full[i][N:]))
    yield

    for b in range(RWKV_BATCHES):
        rows = slice(b * C, (b + 1) * C)
        yb = jnp.concatenate(y[b * RWKV_PAIRS:(b + 1) * RWKV_PAIRS], axis=1)
        yc = yb - _head_sum(yb, lane_a) * (1.0 / N)
        var = _head_sum(yc * yc, lane_a) * (1.0 / N)
        yn = yc * lax.rsqrt(var + GN_EPS) * gng_ref[...] + gnb_ref[...]
        o_ref[b] = ((yn + buf["bonus"][rows, :]) * buf["gate"][rows, :]).astype(o_ref.dtype)
        yield


def _rwkv_kernel(chunks, z_ref, w0_ref, w2_ref, a0_ref, a2_ref, kk_ref, ka_ref, rk_ref,
                 gng_ref, gnb_ref, o_ref, state_ref, *bufs):
    names = [name for name, _, _ in RWKV_BUFFERS]
    sets = (dict(zip(names, bufs[:len(names)])), dict(zip(names, bufs[len(names):])))
    par = (w0_ref, w2_ref, a0_ref, a2_ref, kk_ref, ka_ref, rk_ref)
    step = pl.program_id(0)

    @pl.when(step == 0)
    def _():
        for ref in sets[1].values():
            ref[...] = jnp.zeros_like(ref)
        state_ref[...] = jnp.zeros_like(state_ref)

    first_consumed = step % chunks == 1

    def run(nxt, cur):
        prepare = _rwkv_prepare(z_ref, par, nxt)
        pieces = RWKV_PREPARE_PIECES * RWKV_BATCHES
        for stage, _ in enumerate(
                _rwkv_chunks(cur, state_ref, first_consumed, gng_ref, gnb_ref, o_ref)):
            due = min(stage + 1, RWKV_MATMUL_STAGES) * pieces // RWKV_MATMUL_STAGES
            done = min(stage, RWKV_MATMUL_STAGES) * pieces // RWKV_MATMUL_STAGES
            for _ in range(due - done):
                next(prepare)

    @pl.when(step % 2 == 0)
    def _():
        run(sets[0], sets[1])

    @pl.when(step % 2 == 1)
    def _():
        run(sets[1], sets[0])


def _rwkv(z_rwkv, w0, w2, a0, a2, k_k, k_a, r_k, gn_g, gn_b, batch, seq):
    C, NB = RWKV_CHUNK, RWKV_BATCHES
    chunks = seq // C
    n_blocks = (batch // NB) * chunks
    z3 = z_rwkv.reshape(batch, seq, C_RWKV)
    vec = lambda n: pl.BlockSpec((1, n), lambda s: (0, 0))
    lora = pl.BlockSpec((LORA, D_RWKV), lambda s: (0, 0))

    def block_of(s):
        return (s // chunks, s % chunks, 0)

    out = pl.pallas_call(
        functools.partial(_rwkv_kernel, chunks),
        grid=(n_blocks + 1,),
        in_specs=[
            pl.BlockSpec((NB, C, C_RWKV), lambda s: block_of(jnp.minimum(s, n_blocks - 1))),
            vec(D_RWKV), lora, vec(D_RWKV), lora,
            vec(D_RWKV), vec(D_RWKV), vec(D_RWKV), vec(D_RWKV), vec(D_RWKV),
        ],
        out_specs=pl.BlockSpec((NB, C, D_RWKV), lambda s: block_of(jnp.maximum(s - 1, 0))),
        out_shape=jax.ShapeDtypeStruct((batch, seq, D_RWKV), BF16),
        scratch_shapes=[
            pltpu.VMEM((NB, RWKV_HEAD_DIM, D_RWKV), F32),
        ] + 2 * [pltpu.VMEM((n, D_RWKV), dt) for _, n, dt in RWKV_BUFFERS],
        compiler_params=pltpu.CompilerParams(
            dimension_semantics=("arbitrary",), vmem_limit_bytes=VMEM_LIMIT_BYTES),
        name="rwkv",
    )(z3, w0, w2, a0, a2, k_k, k_a, r_k, gn_g, gn_b)
    return out.reshape(batch * seq, D_RWKV)


def _kv_proj_kernel(m_ref, g_ref, w_ref, k_ref, v_ref):
    mn = _rms_norm(m_ref[...], g_ref[...]).astype(BF16)
    k_ref[...] = jnp.dot(mn, w_ref[:, :D_MODEL], preferred_element_type=F32).astype(k_ref.dtype)
    v_ref[...] = jnp.dot(mn, w_ref[:, D_MODEL:], preferred_element_type=F32).astype(v_ref.dtype)


def _kv_proj(mem2, g, w_kv):
    m = mem2.shape[0]
    tm = MEM_LEN
    return pl.pallas_call(
        _kv_proj_kernel,
        grid=(m // tm,),
        in_specs=[
            pl.BlockSpec((tm, D_MODEL), lambda i: (i, 0)),
            pl.BlockSpec((1, D_MODEL), lambda i: (0, 0)),
            pl.BlockSpec((D_MODEL, 2 * D_MODEL), lambda i: (0, 0)),
        ],
        out_specs=[
            pl.BlockSpec((tm, D_MODEL), lambda i: (i, 0)),
            pl.BlockSpec((tm, D_MODEL), lambda i: (i, 0)),
        ],
        out_shape=[
            jax.ShapeDtypeStruct((m, D_MODEL), BF16),
            jax.ShapeDtypeStruct((m, D_MODEL), BF16),
        ],
        compiler_params=pltpu.CompilerParams(
            dimension_semantics=("arbitrary",), vmem_limit_bytes=VMEM_LIMIT_BYTES),
        name="kv_proj",
    )(mem2, g, w_kv)


def _out_attn_kernel(x_ref, ys_ref, yr_ref, wo1_ref, wo2_ref, gx_ref, wq_ref, k_ref, v_ref,
                     wo_ref, gf_ref, o_ref):
    h1 = (x_ref[...]
          + jnp.dot(ys_ref[...], wo1_ref[...], preferred_element_type=F32)
          + jnp.dot(yr_ref[...], wo2_ref[...], preferred_element_type=F32))
    hn = _rms_norm(h1, gx_ref[...]).astype(BF16)
    q = jnp.dot(hn, wq_ref[...], preferred_element_type=F32).astype(BF16)
    heads = []
    for h in range(XATTN_HEADS):
        sl = slice(h * XATTN_HEAD_DIM, (h + 1) * XATTN_HEAD_DIM)
        s = _dot_nt(q[:, sl], k_ref[:, sl]) * (XATTN_HEAD_DIM ** -0.5)
        s = s - jnp.max(s, axis=-1, keepdims=True)
        e = jnp.exp(s)
        p = e / jnp.sum(e, axis=-1, keepdims=True)
        heads.append(jnp.dot(p.astype(BF16), v_ref[:, sl], preferred_element_type=F32).astype(BF16))
    o = jnp.concatenate(heads, axis=-1)
    h2 = h1 + jnp.dot(o, wo_ref[...], preferred_element_type=F32)
    o_ref[...] = _rms_norm(h2, gf_ref[...])


def _out_attn(x2, y_sgu, y_rwkv, wo1, wo2, g_x, w_q, k_mem, v_mem, w_o, g_f, seq):
    m = x2.shape[0]
    tq = ATTN_ROWS
    per_batch = seq // tq
    row_blk = lambda n: pl.BlockSpec((tq, n), lambda i: (i, 0))
    full = lambda a, b: pl.BlockSpec((a, b), lambda i: (0, 0))
    mem_blk = pl.BlockSpec((MEM_LEN, D_MODEL), lambda i: (i // per_batch, 0))
    return pl.pallas_call(
        _out_attn_kernel,
        grid=(m // tq,),
        in_specs=[
            row_blk(D_MODEL), row_blk(D_SGU), row_blk(D_RWKV),
            full(D_SGU, D_MODEL), full(D_RWKV, D_MODEL), full(1, D_MODEL),
            full(D_MODEL, D_MODEL), mem_blk, mem_blk, full(D_MODEL, D_MODEL), full(1, D_MODEL),
        ],
        out_specs=row_blk(D_MODEL),
        out_shape=jax.ShapeDtypeStruct((m, D_MODEL), F32),
        compiler_params=pltpu.CompilerParams(
            dimension_semantics=("arbitrary",), vmem_limit_bytes=VMEM_LIMIT_BYTES),
        name="out_attn",
    )(x2, y_sgu, y_rwkv, wo1, wo2, g_x, w_q, k_mem, v_mem, w_o, g_f)


def kernel(x, mem, ln_mix_g, w_in, sgu_ln_g, sgu_ln_b, sgu_ws, sgu_bs, sgu_out_g, rw_mu, rw_w0, rw_w2, rw_a0, rw_a2, rw_k_k, rw_k_a, rw_r_k, rw_gn_g, rw_gn_b, w_out, ln_x_g, ln_mem_g, w_q, w_kv, w_o, ln_f_g):
    batch, seq, _ = x.shape
    assert w_in.shape[0] == 1, "the final norm is fused into the only layer's last call"
    row = lambda a: a.reshape(1, -1)
    h = x.reshape(batch * seq, D_MODEL)
    mem2 = mem.reshape(batch * MEM_LEN, D_MODEL)
    for l in range(1):
        bs_full = jnp.repeat(sgu_bs[l].T, SGU_HEAD_DIM, axis=1)
        y_sgu, z_rwkv = _in_proj(h, row(ln_mix_g[l]), w_in[l].astype(BF16), row(sgu_ln_g[l]),
                                 row(sgu_ln_b[l]), sgu_ws[l], bs_full, row(sgu_out_g[l]),
                                 row(rw_mu[l]), seq)
        y_rwkv = _rwkv(z_rwkv, row(rw_w0[l]), rw_w2[l].astype(BF16),
                       row(rw_a0[l]), rw_a2[l].astype(BF16), row(rw_k_k[l]), row(rw_k_a[l]),
                       row(rw_r_k[l]), row(rw_gn_g[l]), row(rw_gn_b[l]), batch, seq)
        k_mem, v_mem = _kv_proj(mem2, row(ln_mem_g[l]), w_kv[l].astype(BF16))
        w_out_b = w_out[l].astype(BF16)
        h = _out_attn(h, y_sgu, y_rwkv, w_out_b[:D_SGU], w_out_b[D_SGU:], row(ln_x_g[l]),
                      w_q[l].astype(BF16), k_mem, v_mem, w_o[l].astype(BF16), row(ln_f_g), seq)
    return h.reshape(batch, seq, D_MODEL)
```

```python
import functools
import math

import jax
import jax.numpy as jnp
from jax import lax
from jax.experimental import pallas as pl
from jax.experimental.pallas import tpu as pltpu

D_MODEL = 1024
MEM_LEN = 256
D_SGU = 512
SGU_HEADS = 4
SGU_HEAD_DIM = D_SGU // SGU_HEADS
SGU_CHUNK = 128
D_RWKV = 512
RWKV_HEAD_DIM = 64
RWKV_HEADS = D_RWKV // RWKV_HEAD_DIM
LORA = 64
C_SGU = 3 * D_SGU
C_RWKV = 4 * D_RWKV + 2 * LORA
XATTN_HEADS = 4
XATTN_HEAD_DIM = D_MODEL // XATTN_HEADS
RMS_EPS = 1e-6
LN_EPS = 1e-5
GN_EPS = 64e-5

RWKV_CHUNK = 64
RWKV_BATCHES = 8
PAIR = 2 * RWKV_HEAD_DIM
RWKV_PAIRS = RWKV_HEADS // 2
IN_PROJ_ROWS = 512
ATTN_ROWS = 512
VMEM_LIMIT_BYTES = 48 * 1024 * 1024

SUBLANES = 8
LOG2_E = math.log2(math.e)

F32 = jnp.float32
BF16 = jnp.bfloat16


def _dot(a, b):
    return jnp.dot(a.astype(BF16), b.astype(BF16), preferred_element_type=F32)


def _dot_nt(a, b):
    return lax.dot_general(a.astype(BF16), b.astype(BF16), (((1,), (1,)), ((), ())),
                           preferred_element_type=F32)


def _dot_tn(a, b):
    return lax.dot_general(a.astype(BF16), b.astype(BF16), (((0,), (0,)), ((), ())),
                           preferred_element_type=F32)


def _rms_norm(x, g):
    return x * lax.rsqrt(jnp.mean(x * x, axis=-1, keepdims=True) + RMS_EPS) * g


def _gelu(x):
    return 0.5 * x * (1.0 + lax.erf(x * (2.0 ** -0.5)))


def _silu(x):
    return x * jax.nn.sigmoid(x)


def _sgu_chunk(z, lng_ref, lnb_ref, ws_c, bs_ref, og_ref):
    u = _gelu(z[:, 0:D_SGU])
    v = _gelu(z[:, D_SGU:2 * D_SGU])
    gate = z[:, 2 * D_SGU:3 * D_SGU]
    vc = v - jnp.mean(v, axis=-1, keepdims=True)
    var = jnp.mean(vc * vc, axis=-1, keepdims=True)
    vn = (vc * lax.rsqrt(var + LN_EPS) * lng_ref[...] + lnb_ref[...]).astype(BF16)
    sv = jnp.concatenate(
        [jnp.dot(ws_c[h], vn[:, h * SGU_HEAD_DIM:(h + 1) * SGU_HEAD_DIM],
                 preferred_element_type=F32) for h in range(SGU_HEADS)], axis=1) + bs_ref[...]
    return _rms_norm(u * sv, og_ref[...]) * _silu(gate)


def _in_proj_kernel(tiles_per_batch, x_ref, g_ref, w_ref, lng_ref, lnb_ref, ws_ref, bs_ref, og_ref,
                    mu_ref, ys_ref, zr_ref, zs_ref, prev_ref):
    tile = pl.program_id(0)

    @pl.when(tile == 0)
    def _():
        prev_ref[...] = jnp.zeros_like(prev_ref)

    first_tile = tile % tiles_per_batch == 0
    row8 = lax.broadcasted_iota(jnp.int32, (SUBLANES, 1), 0)
    xn = _rms_norm(x_ref[...], g_ref[...]).astype(BF16)
    zs_ref[...] = jnp.dot(xn, w_ref[:, :C_SGU], preferred_element_type=F32)
    row = lax.broadcasted_iota(jnp.int32, (SGU_CHUNK, SGU_CHUNK), 0)
    col = lax.broadcasted_iota(jnp.int32, (SGU_CHUNK, SGU_CHUNK), 1)
    ws_c = [jnp.where(col <= row, ws_ref[h], 0.0).astype(BF16) for h in range(SGU_HEADS)]
    n_chunks = IN_PROJ_ROWS // SGU_CHUNK
    lane_tiles = C_RWKV // PAIR
    bounds = [C_SGU + PAIR * (lane_tiles * c // n_chunks) for c in range(n_chunks + 1)]
    gate_cols = slice(3 * D_RWKV, 4 * D_RWKV)
    for c in range(n_chunks):
        z = jnp.dot(xn, w_ref[:, bounds[c]:bounds[c + 1]], preferred_element_type=F32)
        cols = slice(bounds[c] - C_SGU, bounds[c + 1] - C_SGU)
        prev = jnp.where(first_tile, 0.0, prev_ref[:, cols])
        rolled = pltpu.roll(z, 1, axis=0)
        z_prev = jnp.concatenate(
            [jnp.where(row8 == 0, prev, rolled[:SUBLANES]), rolled[SUBLANES:]], axis=0)
        prev_ref[:, cols] = z[IN_PROJ_ROWS - 1:IN_PROJ_ROWS, :]
        zs = z + (z_prev - z) * mu_ref[:, cols]
        if cols.start <= gate_cols.start < cols.stop:
            assert gate_cols.stop <= cols.stop
            lo, hi = gate_cols.start - cols.start, gate_cols.stop - cols.start
            pieces = [zs[:, :lo], _silu(zs[:, lo:hi]), zs[:, hi:]]
            zs = jnp.concatenate([p for p in pieces if p.shape[1]], axis=1)
        zr_ref[:, cols] = zs
        rows = slice(c * SGU_CHUNK, (c + 1) * SGU_CHUNK)
        ys_ref[rows, :] = _sgu_chunk(zs_ref[rows, :], lng_ref, lnb_ref, ws_c, bs_ref,
                                     og_ref).astype(ys_ref.dtype)


def _in_proj(x2, g, w_in, ln_g, ln_b, ws, bs_full, out_g, mu, seq):
    m = x2.shape[0]
    tm = IN_PROJ_ROWS
    vec = pl.BlockSpec((1, D_SGU), lambda i: (0, 0))
    return pl.pallas_call(
        functools.partial(_in_proj_kernel, seq // tm),
        grid=(m // tm,),
        in_specs=[
            pl.BlockSpec((tm, D_MODEL), lambda i: (i, 0)),
            pl.BlockSpec((1, D_MODEL), lambda i: (0, 0)),
            pl.BlockSpec((D_MODEL, C_SGU + C_RWKV), lambda i: (0, 0)),
            vec, vec,
            pl.BlockSpec((SGU_HEADS, SGU_CHUNK, SGU_CHUNK), lambda i: (0, 0, 0)),
            pl.BlockSpec((SGU_CHUNK, D_SGU), lambda i: (0, 0)),
            vec,
            pl.BlockSpec((1, C_RWKV), lambda i: (0, 0)),
        ],
        out_specs=[
            pl.BlockSpec((tm, D_SGU), lambda i: (i, 0)),
            pl.BlockSpec((tm, C_RWKV), lambda i: (i, 0)),
        ],
        out_shape=[
            jax.ShapeDtypeStruct((m, D_SGU), BF16),
            jax.ShapeDtypeStruct((m, C_RWKV), F32),
        ],
        scratch_shapes=[
            pltpu.VMEM((tm, C_SGU), F32),
            pltpu.VMEM((1, C_RWKV), F32),
        ],
        compiler_params=pltpu.CompilerParams(
            dimension_semantics=("arbitrary",), vmem_limit_bytes=VMEM_LIMIT_BYTES),
        name="in_proj",
    )(x2, g, w_in, ln_g, ln_b, ws, bs_full, out_g, mu)


RWKV_ROWS = RWKV_CHUNK * RWKV_BATCHES
RWKV_MATMUL_STAGES = 16
RWKV_PREPARE_PIECES = 3
RWKV_BUFFERS = (
    ("ar", 2 * RWKV_ROWS, BF16),
    ("bk_h", 2 * RWKV_ROWS, BF16),
    ("b_t", RWKV_ROWS, BF16),
    ("k_t", RWKV_ROWS, BF16),
    ("v", RWKV_ROWS, BF16),
    ("g_col", RWKV_ROWS, F32),
    ("bonus", RWKV_ROWS, F32),
    ("gate", RWKV_ROWS, F32),
)


def _block_diag(x, keep_a, keep_b):
    return jnp.concatenate([x * keep_a, x * keep_b], axis=0)


def _dots_paired(lhs, rhs):
    out = []
    for i in range(0, len(lhs), 2):
        m, n = lhs[i].shape[0], rhs[i].shape[1]
        both = jnp.dot(jnp.concatenate([lhs[i], lhs[i + 1]], axis=0),
                       jnp.concatenate([rhs[i], rhs[i + 1]], axis=1),
                       preferred_element_type=F32)
        out += [both[:m, :n], both[m:, n:]]
    return out


def _dots_paired_tn(lhs, rhs):
    out = []
    for i in range(0, len(lhs), 2):
        m, n = lhs[i].shape[1], rhs[i].shape[1]
        both = _dot_tn(jnp.concatenate([lhs[i], lhs[i + 1]], axis=1),
                       jnp.concatenate([rhs[i], rhs[i + 1]], axis=1))
        out += [both[:m, :n], both[m:, n:]]
    return out


def _head_sum(x, lane_a):
    outs = []
    for p in range(RWKV_PAIRS):
        xp = x[:, p * PAIR:(p + 1) * PAIR]
        sum_a = jnp.sum(jnp.where(lane_a, xp, 0.0), axis=-1, keepdims=True)
        sum_b = jnp.sum(jnp.where(lane_a, 0.0, xp), axis=-1, keepdims=True)
        outs.append(jnp.where(lane_a, sum_a, sum_b))
    return jnp.concatenate(outs, axis=-1)


def _rwkv_prepare(z_ref, par, out):
    C, N = RWKV_CHUNK, RWKV_HEAD_DIM
    w0_ref, w2_ref, a0_ref, a2_ref, kk_ref, ka_ref, rk_ref = par
    lane_c = lax.broadcasted_iota(jnp.int32, (C, PAIR), 1)
    lane_a = lane_c < N
    eye = (lane_c & (N - 1)) == lax.broadcasted_iota(jnp.int32, (C, PAIR), 0)
    eye_full = jnp.concatenate([eye] * RWKV_PAIRS, axis=1)
    row_c = lax.broadcasted_iota(jnp.int32, (C, C), 0)
    col_c = lax.broadcasted_iota(jnp.int32, (C, C), 1)
    tril = (col_c <= row_c).astype(F32).astype(BF16)
    tril3 = jnp.concatenate([tril, tril, tril], axis=1)

    def one_batch(b):
        r = z_ref[b, :, 0:D_RWKV]
        k = z_ref[b, :, D_RWKV:2 * D_RWKV]
        v = z_ref[b, :, 2 * D_RWKV:3 * D_RWKV]
        gate_act = z_ref[b, :, 3 * D_RWKV:4 * D_RWKV]
        wd = z_ref[b, :, 4 * D_RWKV:4 * D_RWKV + LORA]
        ad = z_ref[b, :, 4 * D_RWKV + LORA:4 * D_RWKV + 2 * LORA]

        lw = w0_ref[...] + _dot(jnp.tanh(wd), w2_ref[...])
        ld = -(math.exp(-0.5) * LOG2_E) * jax.nn.sigmoid(lw)
        icl = jax.nn.sigmoid(a0_ref[...] + _dot(ad, a2_ref[...]))
        yield
        k2 = k * ((1.0 - ka_ref[...]) + ka_ref[...] * icl)
        kk = k * kk_ref[...]
        kk = kk * lax.rsqrt(jnp.maximum(_head_sum(kk * kk, lane_a), 1e-24))
        bvec = kk * icl

        p1 = ld.astype(BF16)
        r1 = ld - p1.astype(F32)
        p2 = r1.astype(BF16)
        p3 = (r1 - p2.astype(F32)).astype(BF16)
        cum = jnp.dot(tril3, jnp.concatenate([p1, p2, p3], axis=0), preferred_element_type=F32)
        last = cum[C - 1:C, :]
        yield
        e_neg = jnp.exp2(-cum)
        g_last = jnp.exp2(last)
        a_t = -kk * jnp.exp2(cum - ld)
        r_t = r * jnp.exp2(cum)
        b_t = bvec * e_neg
        k_t = k2 * e_neg

        stacked = {
            "ar": (a_t, r_t),
            "bk_h": (b_t * g_last, k_t * g_last),
        }
        for name, (top, bot) in stacked.items():
            out[name][2 * b * C:(2 * b + 1) * C, :] = top.astype(BF16)
            out[name][(2 * b + 1) * C:(2 * b + 2) * C, :] = bot.astype(BF16)
        rows = slice(b * C, (b + 1) * C)
        out["b_t"][rows, :] = b_t.astype(BF16)
        out["k_t"][rows, :] = k_t.astype(BF16)
        out["v"][rows, :] = v.astype(BF16)
        out["g_col"][rows, :] = _head_sum(jnp.where(eye_full, g_last, 0.0), lane_a)
        out["bonus"][rows, :] = _head_sum(r * k2 * rk_ref[...], lane_a) * v
        out["gate"][rows, :] = gate_act
        yield

    batches = [one_batch(b) for b in range(RWKV_BATCHES)]
    for _ in range(RWKV_PREPARE_PIECES):
        for gen in batches:
            next(gen)
            yield


def _rwkv_chunks(buf, state_ref, first_chunk, gng_ref, gnb_ref, o_ref):
    C, N = RWKV_CHUNK, RWKV_HEAD_DIM
    row = lax.broadcasted_iota(jnp.int32, (C, PAIR), 0)
    lane = lax.broadcasted_iota(jnp.int32, (C, PAIR), 1)
    col = lane & (N - 1)
    lane_a = lane < N
    incl = col <= row
    strict = col < row

    units = [(b, p) for b in range(RWKV_BATCHES) for p in range(RWKV_PAIRS)]
    nu = len(units)
    lanes = lambda u: slice(u[1] * PAIR, (u[1] + 1) * PAIR)
    two = lambda name, u: buf[name][2 * u[0] * C:(2 * u[0] + 2) * C, lanes(u)]
    one = lambda name, u: buf[name][u[0] * C:(u[0] + 1) * C, lanes(u)]
    as_keep = lambda mask: jnp.where(mask, 1.0, 0.0).astype(BF16)
    keep_a, keep_b = as_keep(lane_a), as_keep(jnp.logical_not(lane_a))
    bd = lambda x: _block_diag(x, keep_a, keep_b)

    s_bk = [_dot_nt(two("ar", u), jnp.concatenate([bd(one("b_t", u)), bd(one("k_t", u))], axis=0))
            for u in units]
    yield
    s_b = [s[:, :PAIR] for s in s_bk]
    s_k = [s[:, PAIR:] for s in s_bk]
    n_mat = [jnp.where(strict, s[:C], 0.0) for s in s_b]
    n_bf = [n.astype(BF16) for n in n_mat]
    a_ak = [jnp.where(strict, s[:C], 0.0).astype(BF16) for s in s_k]
    a_rb = [jnp.where(incl, s[C:], 0.0).astype(BF16) for s in s_b]
    a_rk = [jnp.where(incl, s[C:], 0.0).astype(BF16) for s in s_k]
    yield

    lvl = ((row >> 1) == (col >> 1)) & ((row & 1) == 1) & ((col & 1) == 0)
    eye_f = (col == row).astype(F32)
    x = [eye_f + jnp.where(lvl, n, 0.0) for n in n_mat]
    s = 2
    while s < C:
        shift = s.bit_length()
        lvl = ((row >> shift) == (col >> shift)) & ((row & s) != 0) & ((col & s) == 0)
        lvl_a, lvl_b = as_keep(lvl & lane_a), as_keep(lvl & jnp.logical_not(lane_a))
        x_bf = [xi.astype(BF16) for xi in x]
        xn = _dots_paired(x_bf, [_block_diag(n, lvl_a, lvl_b) for n in n_bf])
        yield
        xnx = _dots_paired([t.astype(BF16) for t in xn], [bd(t) for t in x_bf])
        x = [x[i] + xnx[i] for i in range(nu)]
        yield
        s *= 2

    x_bf = [xi.astype(BF16) for xi in x]
    v_bd = [bd(one("v", u)) for u in units]
    av = [t.astype(BF16) for t in _dots_paired(a_ak, v_bd)]
    yield
    p_mat = [t.astype(BF16) for t in _dots_paired(x_bf, [bd(two("ar", u)[:C]) for u in units])]
    yield

    t_old = [jnp.where(first_chunk, 0.0, state_ref[u[0], :, lanes(u)]) for u in units]
    t_bd = [bd(t.astype(BF16)) for t in t_old]
    u_mat = _dots_paired(
        [jnp.concatenate([p_mat[i], x_bf[i]], axis=1) for i in range(nu)],
        [jnp.concatenate([t_bd[i], bd(av[i])], axis=0) for i in range(nu)])
    u_bf = [um.astype(BF16) for um in u_mat]
    yield
    y = _dots_paired(
        [jnp.concatenate([two("ar", u)[C:], a_rb[i], a_rk[i]], axis=1)
         for i, u in enumerate(units)],
        [jnp.concatenate([t_bd[i], bd(u_bf[i]), v_bd[i]], axis=0) for i in range(nu)])
    yield
    full = _dots_paired_tn(
        [two("bk_h", u) for u in units],
        [jnp.concatenate([u_bf[i], one("v", u)], axis=0) for i, u in enumerate(units)])
    for i, u in enumerate(units):
        state_ref[u[0], :, lanes(u)] = (one("g_col", u) * t_old[i]
                                        + jnp.where(lane_a, full[i][:N], full[i][N:]))
    yield

    for b in range(RWKV_BATCHES):
        rows = slice(b * C, (b + 1) * C)
        yb = jnp.concatenate(y[b * RWKV_PAIRS:(b + 1) * RWKV_PAIRS], axis=1)
        yc = yb - _head_sum(yb, lane_a) * (1.0 / N)
        var = _head_sum(yc * yc, lane_a) * (1.0 / N)
        yn = yc * lax.rsqrt(var + GN_EPS) * gng_ref[...] + gnb_ref[...]
        o_ref[b] = ((yn + buf["bonus"][rows, :]) * buf["gate"][rows, :]).astype(o_ref.dtype)
        yield


def _rwkv_kernel(chunks, z_ref, w0_ref, w2_ref, a0_ref, a2_ref, kk_ref, ka_ref, rk_ref,
                 gng_ref, gnb_ref, o_ref, state_ref, *bufs):
    names = [name for name, _, _ in RWKV_BUFFERS]
    sets = (dict(zip(names, bufs[:len(names)])), dict(zip(names, bufs[len(names):])))
    par = (w0_ref, w2_ref, a0_ref, a2_ref, kk_ref, ka_ref, rk_ref)
    step = pl.program_id(0)

    @pl.when(step == 0)
    def _():
        for ref in sets[1].values():
            ref[...] = jnp.zeros_like(ref)
        state_ref[...] = jnp.zeros_like(state_ref)

    first_consumed = step % chunks == 1

    def run(nxt, cur):
        prepare = _rwkv_prepare(z_ref, par, nxt)
        pieces = RWKV_PREPARE_PIECES * RWKV_BATCHES
        for stage, _ in enumerate(
                _rwkv_chunks(cur, state_ref, first_consumed, gng_ref, gnb_ref, o_ref)):
            due = min(stage + 1, RWKV_MATMUL_STAGES) * pieces // RWKV_MATMUL_STAGES
            done = min(stage, RWKV_MATMUL_STAGES) * pieces // RWKV_MATMUL_STAGES
            for _ in range(due - done):
                next(prepare)

    @pl.when(step % 2 == 0)
    def _():
        run(sets[0], sets[1])

    @pl.when(step % 2 == 1)
    def _():
        run(sets[1], sets[0])


def _rwkv(z_rwkv, w0, w2, a0, a2, k_k, k_a, r_k, gn_g, gn_b, batch, seq):
    C, NB = RWKV_CHUNK, RWKV_BATCHES
    chunks = seq // C
    n_blocks = (batch // NB) * chunks
    z3 = z_rwkv.reshape(batch, seq, C_RWKV)
    vec = lambda n: pl.BlockSpec((1, n), lambda s: (0, 0))
    lora = pl.BlockSpec((LORA, D_RWKV), lambda s: (0, 0))

    def block_of(s):
        return (s // chunks, s % chunks, 0)

    out = pl.pallas_call(
        functools.partial(_rwkv_kernel, chunks),
        grid=(n_blocks + 1,),
        in_specs=[
            pl.BlockSpec((NB, C, C_RWKV), lambda s: block_of(jnp.minimum(s, n_blocks - 1))),
            vec(D_RWKV), lora, vec(D_RWKV), lora,
            vec(D_RWKV), vec(D_RWKV), vec(D_RWKV), vec(D_RWKV), vec(D_RWKV),
        ],
        out_specs=pl.BlockSpec((NB, C, D_RWKV), lambda s: block_of(jnp.maximum(s - 1, 0))),
        out_shape=jax.ShapeDtypeStruct((batch, seq, D_RWKV), BF16),
        scratch_shapes=[
            pltpu.VMEM((NB, RWKV_HEAD_DIM, D_RWKV), F32),
        ] + 2 * [pltpu.VMEM((n, D_RWKV), dt) for _, n, dt in RWKV_BUFFERS],
        compiler_params=pltpu.CompilerParams(
            dimension_semantics=("arbitrary",), vmem_limit_bytes=VMEM_LIMIT_BYTES),
        name="rwkv",
    )(z3, w0, w2, a0, a2, k_k, k_a, r_k, gn_g, gn_b)
    return out.reshape(batch * seq, D_RWKV)


def _kv_proj_kernel(m_ref, g_ref, w_ref, k_ref, v_ref):
    mn = _rms_norm(m_ref[...], g_ref[...]).astype(BF16)
    k = jnp.dot(mn, w_ref[:, :D_MODEL], preferred_element_type=F32) * (XATTN_HEAD_DIM ** -0.5)
    k_ref[...] = k.astype(k_ref.dtype)
    v_ref[...] = jnp.dot(mn, w_ref[:, D_MODEL:], preferred_element_type=F32).astype(v_ref.dtype)


def _kv_proj(mem2, g, w_kv):
    m = mem2.shape[0]
    tm = MEM_LEN
    return pl.pallas_call(
        _kv_proj_kernel,
        grid=(m // tm,),
        in_specs=[
            pl.BlockSpec((tm, D_MODEL), lambda i: (i, 0)),
            pl.BlockSpec((1, D_MODEL), lambda i: (0, 0)),
            pl.BlockSpec((D_MODEL, 2 * D_MODEL), lambda i: (0, 0)),
        ],
        out_specs=[
            pl.BlockSpec((tm, D_MODEL), lambda i: (i, 0)),
            pl.BlockSpec((tm, D_MODEL), lambda i: (i, 0)),
        ],
        out_shape=[
            jax.ShapeDtypeStruct((m, D_MODEL), BF16),
            jax.ShapeDtypeStruct((m, D_MODEL), BF16),
        ],
        compiler_params=pltpu.CompilerParams(
            dimension_semantics=("arbitrary",), vmem_limit_bytes=VMEM_LIMIT_BYTES),
        name="kv_proj",
    )(mem2, g, w_kv)


def _out_attn_front(x_ref, ys_ref, yr_ref, wout_ref, gx_ref, wq_ref, h1_ref, q_ref):
    y = jnp.concatenate([ys_ref[...], yr_ref[...]], axis=1)
    h1 = x_ref[...] + jnp.dot(y, wout_ref[...], preferred_element_type=F32)
    h1_ref[...] = h1
    yield
    hn = _rms_norm(h1, gx_ref[...]).astype(BF16)
    yield
    q_ref[...] = jnp.dot(hn, wq_ref[...], preferred_element_type=F32).astype(BF16)
    yield


def _out_attn_back(h1_ref, q_ref, k_ref, v_ref, wo_ref, gf_ref, o_ref):
    cols = [slice(h * XATTN_HEAD_DIM, (h + 1) * XATTN_HEAD_DIM) for h in range(XATTN_HEADS)]
    scores = [_dot_nt(q_ref[:, sl], k_ref[:, sl]) for sl in cols]
    yield
    probs = []
    for s in scores:
        e = jnp.exp(s - jnp.max(s, axis=-1, keepdims=True))
        probs.append((e / jnp.sum(e, axis=-1, keepdims=True)).astype(BF16))
    yield
    o = jnp.concatenate(
        [jnp.dot(p, v_ref[:, sl], preferred_element_type=F32).astype(BF16)
         for p, sl in zip(probs, cols)], axis=-1)
    yield
    h2 = h1_ref[...] + jnp.dot(o, wo_ref[...], preferred_element_type=F32)
    yield
    o_ref[...] = _rms_norm(h2, gf_ref[...])
    yield


OUT_ATTN_ORDER = "BFBBFBBF"


def _out_attn_kernel(x_ref, ys_ref, yr_ref, wout_ref, gx_ref, wq_ref, k_ref, v_ref,
                     wo_ref, gf_ref, o_ref, h1_a, q_a, h1_b, q_b):
    step = pl.program_id(0)

    @pl.when(step == 0)
    def _():
        h1_b[...] = jnp.zeros_like(h1_b)
        q_b[...] = jnp.zeros_like(q_b)

    def run(nxt, cur):
        front = _out_attn_front(x_ref, ys_ref, yr_ref, wout_ref, gx_ref, wq_ref, *nxt)
        back = _out_attn_back(*cur, k_ref, v_ref, wo_ref, gf_ref, o_ref)
        for stage in OUT_ATTN_ORDER:
            next(front if stage == "F" else back)

    @pl.when(step % 2 == 0)
    def _():
        run((h1_a, q_a), (h1_b, q_b))

    @pl.when(step % 2 == 1)
    def _():
        run((h1_b, q_b), (h1_a, q_a))


def _out_attn(x2, y_sgu, y_rwkv, w_out, g_x, w_q, k_mem, v_mem, w_o, g_f, seq):
    m = x2.shape[0]
    tq = ATTN_ROWS
    per_batch = seq // tq
    n_tiles = m // tq
    front_blk = lambda n: pl.BlockSpec((tq, n), lambda s: (jnp.minimum(s, n_tiles - 1), 0))
    back_tile = lambda s: jnp.maximum(s - 1, 0)
    full = lambda a, b: pl.BlockSpec((a, b), lambda s: (0, 0))
    mem_blk = pl.BlockSpec((MEM_LEN, D_MODEL), lambda s: (back_tile(s) // per_batch, 0))
    return pl.pallas_call(
        _out_attn_kernel,
        grid=(n_tiles + 1,),
        in_specs=[
            front_blk(D_MODEL), front_blk(D_SGU), front_blk(D_RWKV),
            full(D_SGU + D_RWKV, D_MODEL), full(1, D_MODEL),
            full(D_MODEL, D_MODEL), mem_blk, mem_blk, full(D_MODEL, D_MODEL), full(1, D_MODEL),
        ],
        out_specs=pl.BlockSpec((tq, D_MODEL), lambda s: (back_tile(s), 0)),
        out_shape=jax.ShapeDtypeStruct((m, D_MODEL), F32),
        scratch_shapes=2 * [pltpu.VMEM((tq, D_MODEL), F32), pltpu.VMEM((tq, D_MODEL), BF16)],
        compiler_params=pltpu.CompilerParams(
            dimension_semantics=("arbitrary",), vmem_limit_bytes=VMEM_LIMIT_BYTES),
        name="out_attn",
    )(x2, y_sgu, y_rwkv, w_out, g_x, w_q, k_mem, v_mem, w_o, g_f)


def kernel(x, mem, ln_mix_g, w_in, sgu_ln_g, sgu_ln_b, sgu_ws, sgu_bs, sgu_out_g, rw_mu, rw_w0, rw_w2, rw_a0, rw_a2, rw_k_k, rw_k_a, rw_r_k, rw_gn_g, rw_gn_b, w_out, ln_x_g, ln_mem_g, w_q, w_kv, w_o, ln_f_g):
    batch, seq, _ = x.shape
    assert w_in.shape[0] == 1, "the final norm is fused into the only layer's last call"
    row = lambda a: a.reshape(1, -1)
    h = x.reshape(batch * seq, D_MODEL)
    mem2 = mem.reshape(batch * MEM_LEN, D_MODEL)
    for l in range(1):
        bs_full = jnp.repeat(sgu_bs[l].T, SGU_HEAD_DIM, axis=1)
        y_sgu, z_rwkv = _in_proj(h, row(ln_mix_g[l]), w_in[l].astype(BF16), row(sgu_ln_g[l]),
                                 row(sgu_ln_b[l]), sgu_ws[l], bs_full, row(sgu_out_g[l]),
                                 row(rw_mu[l]), seq)
        y_rwkv = _rwkv(z_rwkv, row(rw_w0[l]), rw_w2[l].astype(BF16),
                       row(rw_a0[l]), rw_a2[l].astype(BF16), row(rw_k_k[l]), row(rw_k_a[l]),
                       row(rw_r_k[l]), row(rw_gn_g[l]), row(rw_gn_b[l]), batch, seq)
        k_mem, v_mem = _kv_proj(mem2, row(ln_mem_g[l]), w_kv[l].astype(BF16))
        h = _out_attn(h, y_sgu, y_rwkv, w_out[l].astype(BF16), row(ln_x_g[l]),
                      w_q[l].astype(BF16), k_mem, v_mem, w_o[l].astype(BF16), row(ln_f_g), seq)
    return h.reshape(batch, seq, D_MODEL)
```

```python
import functools
import math

import jax
import jax.numpy as jnp
from jax import lax
from jax.experimental import pallas as pl
from jax.experimental.pallas import tpu as pltpu

D_MODEL = 1024
MEM_LEN = 256
D_SGU = 512
SGU_HEADS = 4
SGU_HEAD_DIM = D_SGU // SGU_HEADS
SGU_CHUNK = 128
D_RWKV = 512
RWKV_HEAD_DIM = 64
RWKV_HEADS = D_RWKV // RWKV_HEAD_DIM
LORA = 64
C_SGU = 3 * D_SGU
C_RWKV = 4 * D_RWKV + 2 * LORA
XATTN_HEADS = 4
XATTN_HEAD_DIM = D_MODEL // XATTN_HEADS
RMS_EPS = 1e-6
LN_EPS = 1e-5
GN_EPS = 64e-5

RWKV_CHUNK = 64
RWKV_BATCHES = 8
PAIR = 2 * RWKV_HEAD_DIM
RWKV_PAIRS = RWKV_HEADS // 2
IN_PROJ_ROWS = 512
ATTN_ROWS = 512
VMEM_LIMIT_BYTES = 48 * 1024 * 1024

SUBLANES = 8
LOG2_E = math.log2(math.e)

F32 = jnp.float32
BF16 = jnp.bfloat16


def _dot(a, b):
    return jnp.dot(a.astype(BF16), b.astype(BF16), preferred_element_type=F32)


def _dot_nt(a, b):
    return lax.dot_general(a.astype(BF16), b.astype(BF16), (((1,), (1,)), ((), ())),
                           preferred_element_type=F32)


def _dot_tn(a, b):
    return lax.dot_general(a.astype(BF16), b.astype(BF16), (((0,), (0,)), ((), ())),
                           preferred_element_type=F32)


def _rms_norm(x, g):
    return x * lax.rsqrt(jnp.mean(x * x, axis=-1, keepdims=True) + RMS_EPS) * g


def _gelu(x):
    return 0.5 * x * (1.0 + lax.erf(x * (2.0 ** -0.5)))


def _silu(x):
    return x * jax.nn.sigmoid(x)


def _sgu_chunk(z, lng_ref, lnb_ref, ws_c, bs_ref, og_ref):
    u = _gelu(z[:, 0:D_SGU])
    v = _gelu(z[:, D_SGU:2 * D_SGU])
    gate = z[:, 2 * D_SGU:3 * D_SGU]
    vc = v - jnp.mean(v, axis=-1, keepdims=True)
    var = jnp.mean(vc * vc, axis=-1, keepdims=True)
    vn = (vc * lax.rsqrt(var + LN_EPS) * lng_ref[...] + lnb_ref[...]).astype(BF16)
    sv = jnp.concatenate(
        [jnp.dot(ws_c[h], vn[:, h * SGU_HEAD_DIM:(h + 1) * SGU_HEAD_DIM],
                 preferred_element_type=F32) for h in range(SGU_HEADS)], axis=1) + bs_ref[...]
    return _rms_norm(u * sv, og_ref[...]) * _silu(gate)


def _in_proj_kernel(tiles_per_batch, x_ref, g_ref, w_ref, lng_ref, lnb_ref, ws_ref, bs_ref, og_ref,
                    mu_ref, ys_ref, zr_ref, zs_ref, prev_ref):
    tile = pl.program_id(0)

    @pl.when(tile == 0)
    def _():
        prev_ref[...] = jnp.zeros_like(prev_ref)

    first_tile = tile % tiles_per_batch == 0
    row8 = lax.broadcasted_iota(jnp.int32, (SUBLANES, 1), 0)
    xn = _rms_norm(x_ref[...], g_ref[...]).astype(BF16)
    zs_ref[...] = jnp.dot(xn, w_ref[:, :C_SGU], preferred_element_type=F32)
    row = lax.broadcasted_iota(jnp.int32, (SGU_CHUNK, SGU_CHUNK), 0)
    col = lax.broadcasted_iota(jnp.int32, (SGU_CHUNK, SGU_CHUNK), 1)
    ws_c = [jnp.where(col <= row, ws_ref[h], 0.0).astype(BF16) for h in range(SGU_HEADS)]
    n_chunks = IN_PROJ_ROWS // SGU_CHUNK
    lane_tiles = C_RWKV // PAIR
    bounds = [C_SGU + PAIR * (lane_tiles * c // n_chunks) for c in range(n_chunks + 1)]
    gate_cols = slice(3 * D_RWKV, 4 * D_RWKV)
    for c in range(n_chunks):
        z = jnp.dot(xn, w_ref[:, bounds[c]:bounds[c + 1]], preferred_element_type=F32)
        cols = slice(bounds[c] - C_SGU, bounds[c + 1] - C_SGU)
        prev = jnp.where(first_tile, 0.0, prev_ref[:, cols])
        rolled = pltpu.roll(z, 1, axis=0)
        z_prev = jnp.concatenate(
            [jnp.where(row8 == 0, prev, rolled[:SUBLANES]), rolled[SUBLANES:]], axis=0)
        prev_ref[:, cols] = z[IN_PROJ_ROWS - 1:IN_PROJ_ROWS, :]
        zs = z + (z_prev - z) * mu_ref[:, cols]
        if cols.start <= gate_cols.start < cols.stop:
            assert gate_cols.stop <= cols.stop
            lo, hi = gate_cols.start - cols.start, gate_cols.stop - cols.start
            pieces = [zs[:, :lo], _silu(zs[:, lo:hi]), zs[:, hi:]]
            zs = jnp.concatenate([p for p in pieces if p.shape[1]], axis=1)
        zr_ref[:, cols] = zs
        rows = slice(c * SGU_CHUNK, (c + 1) * SGU_CHUNK)
        ys_ref[rows, :] = _sgu_chunk(zs_ref[rows, :], lng_ref, lnb_ref, ws_c, bs_ref,
                                     og_ref).astype(ys_ref.dtype)


def _in_proj(x2, g, w_in, ln_g, ln_b, ws, bs_full, out_g, mu, seq):
    m = x2.shape[0]
    tm = IN_PROJ_ROWS
    vec = pl.BlockSpec((1, D_SGU), lambda i: (0, 0))
    return pl.pallas_call(
        functools.partial(_in_proj_kernel, seq // tm),
        grid=(m // tm,),
        in_specs=[
            pl.BlockSpec((tm, D_MODEL), lambda i: (i, 0)),
            pl.BlockSpec((1, D_MODEL), lambda i: (0, 0)),
            pl.BlockSpec((D_MODEL, C_SGU + C_RWKV), lambda i: (0, 0)),
            vec, vec,
            pl.BlockSpec((SGU_HEADS, SGU_CHUNK, SGU_CHUNK), lambda i: (0, 0, 0)),
            pl.BlockSpec((SGU_CHUNK, D_SGU), lambda i: (0, 0)),
            vec,
            pl.BlockSpec((1, C_RWKV), lambda i: (0, 0)),
        ],
        out_specs=[
            pl.BlockSpec((tm, D_SGU), lambda i: (i, 0)),
            pl.BlockSpec((tm, C_RWKV), lambda i: (i, 0)),
        ],
        out_shape=[
            jax.ShapeDtypeStruct((m, D_SGU), BF16),
            jax.ShapeDtypeStruct((m, C_RWKV), F32),
        ],
        scratch_shapes=[
            pltpu.VMEM((tm, C_SGU), F32),
            pltpu.VMEM((1, C_RWKV), F32),
        ],
        compiler_params=pltpu.CompilerParams(
            dimension_semantics=("arbitrary",), vmem_limit_bytes=VMEM_LIMIT_BYTES),
        name="in_proj",
    )(x2, g, w_in, ln_g, ln_b, ws, bs_full, out_g, mu)


RWKV_ROWS = RWKV_CHUNK * RWKV_BATCHES
RWKV_MATMUL_STAGES = 16
RWKV_PREPARE_PIECES = 3
RWKV_BUFFERS = (
    ("ar", 2 * RWKV_ROWS, BF16),
    ("bk_h", 2 * RWKV_ROWS, BF16),
    ("b_t", RWKV_ROWS, BF16),
    ("k_t", RWKV_ROWS, BF16),
    ("v", RWKV_ROWS, BF16),
    ("g_col", RWKV_ROWS, F32),
    ("bonus", RWKV_ROWS, F32),
    ("gate", RWKV_ROWS, F32),
)


def _block_diag(x, keep_a, keep_b):
    return jnp.concatenate([x * keep_a, x * keep_b], axis=0)


def _dots_paired(lhs, rhs):
    out = []
    for i in range(0, len(lhs), 2):
        m, n = lhs[i].shape[0], rhs[i].shape[1]
        both = jnp.dot(jnp.concatenate([lhs[i], lhs[i + 1]], axis=0),
                       jnp.concatenate([rhs[i], rhs[i + 1]], axis=1),
                       preferred_element_type=F32)
        out += [both[:m, :n], both[m:, n:]]
    return out


def _dots_paired_tn(lhs, rhs):
    out = []
    for i in range(0, len(lhs), 2):
        m, n = lhs[i].shape[1], rhs[i].shape[1]
        both = _dot_tn(jnp.concatenate([lhs[i], lhs[i + 1]], axis=1),
                       jnp.concatenate([rhs[i], rhs[i + 1]], axis=1))
        out += [both[:m, :n], both[m:, n:]]
    return out


def _head_sum(x, lane_a):
    outs = []
    for p in range(RWKV_PAIRS):
        xp = x[:, p * PAIR:(p + 1) * PAIR]
        sum_a = jnp.sum(jnp.where(lane_a, xp, 0.0), axis=-1, keepdims=True)
        sum_b = jnp.sum(jnp.where(lane_a, 0.0, xp), axis=-1, keepdims=True)
        outs.append(jnp.where(lane_a, sum_a, sum_b))
    return jnp.concatenate(outs, axis=-1)


def _rwkv_prepare(z_ref, par, out):
    C, N = RWKV_CHUNK, RWKV_HEAD_DIM
    w0_ref, w2_ref, a0_ref, a2_ref, kk_ref, ka_ref, rk_ref = par
    lane_c = lax.broadcasted_iota(jnp.int32, (C, PAIR), 1)
    lane_a = lane_c < N
    eye = (lane_c & (N - 1)) == lax.broadcasted_iota(jnp.int32, (C, PAIR), 0)
    eye_full = jnp.concatenate([eye] * RWKV_PAIRS, axis=1)
    row_c = lax.broadcasted_iota(jnp.int32, (C, C), 0)
    col_c = lax.broadcasted_iota(jnp.int32, (C, C), 1)
    tril = (col_c <= row_c).astype(F32).astype(BF16)
    tril3 = jnp.concatenate([tril, tril, tril], axis=1)

    def one_batch(b):
        r = z_ref[b, :, 0:D_RWKV]
        k = z_ref[b, :, D_RWKV:2 * D_RWKV]
        v = z_ref[b, :, 2 * D_RWKV:3 * D_RWKV]
        gate_act = z_ref[b, :, 3 * D_RWKV:4 * D_RWKV]
        wd = z_ref[b, :, 4 * D_RWKV:4 * D_RWKV + LORA]
        ad = z_ref[b, :, 4 * D_RWKV + LORA:4 * D_RWKV + 2 * LORA]

        lw = w0_ref[...] + _dot(jnp.tanh(wd), w2_ref[...])
        ld = -(math.exp(-0.5) * LOG2_E) * jax.nn.sigmoid(lw)
        icl = jax.nn.sigmoid(a0_ref[...] + _dot(ad, a2_ref[...]))
        yield
        k2 = k * ((1.0 - ka_ref[...]) + ka_ref[...] * icl)
        kk = k * kk_ref[...]
        kk = kk * lax.rsqrt(jnp.maximum(_head_sum(kk * kk, lane_a), 1e-24))
        bvec = kk * icl

        p1 = ld.astype(BF16)
        r1 = ld - p1.astype(F32)
        p2 = r1.astype(BF16)
        p3 = (r1 - p2.astype(F32)).astype(BF16)
        cum = jnp.dot(tril3, jnp.concatenate([p1, p2, p3], axis=0), preferred_element_type=F32)
        last = cum[C - 1:C, :]
        yield
        e_neg = jnp.exp2(-cum)
        g_last = jnp.exp2(last)
        a_t = -kk * jnp.exp2(cum - ld)
        r_t = r * jnp.exp2(cum)
        b_t = bvec * e_neg
        k_t = k2 * e_neg

        stacked = {
            "ar": (a_t, r_t),
            "bk_h": (b_t * g_last, k_t * g_last),
        }
        for name, (top, bot) in stacked.items():
            out[name][2 * b * C:(2 * b + 1) * C, :] = top.astype(BF16)
            out[name][(2 * b + 1) * C:(2 * b + 2) * C, :] = bot.astype(BF16)
        rows = slice(b * C, (b + 1) * C)
        out["b_t"][rows, :] = b_t.astype(BF16)
        out["k_t"][rows, :] = k_t.astype(BF16)
        out["v"][rows, :] = v.astype(BF16)
        out["g_col"][rows, :] = _head_sum(jnp.where(eye_full, g_last, 0.0), lane_a)
        out["bonus"][rows, :] = _head_sum(r * k2 * rk_ref[...], lane_a) * v
        out["gate"][rows, :] = gate_act
        yield

    batches = [one_batch(b) for b in range(RWKV_BATCHES)]
    for _ in range(RWKV_PREPARE_PIECES):
        for gen in batches:
            next(gen)
            yield


def _rwkv_chunks(buf, state_ref, first_chunk, gng_ref, gnb_ref, o_ref):
    C, N = RWKV_CHUNK, RWKV_HEAD_DIM
    row = lax.broadcasted_iota(jnp.int32, (C, PAIR), 0)
    lane = lax.broadcasted_iota(jnp.int32, (C, PAIR), 1)
    col = lane & (N - 1)
    lane_a = lane < N
    incl = col <= row
    strict = col < row

    units = [(b, p) for b in range(RWKV_BATCHES) for p in range(RWKV_PAIRS)]
    nu = len(units)
    lanes = lambda u: slice(u[1] * PAIR, (u[1] + 1) * PAIR)
    two = lambda name, u: buf[name][2 * u[0] * C:(2 * u[0] + 2) * C, lanes(u)]
    one = lambda name, u: buf[name][u[0] * C:(u[0] + 1) * C, lanes(u)]
    as_keep = lambda mask: jnp.where(mask, 1.0, 0.0).astype(BF16)
    keep_a, keep_b = as_keep(lane_a), as_keep(jnp.logical_not(lane_a))
    bd = lambda x: _block_diag(x, keep_a, keep_b)

    s_bk = [_dot_nt(two("ar", u), jnp.concatenate([bd(one("b_t", u)), bd(one("k_t", u))], axis=0))
            for u in units]
    yield
    s_b = [s[:, :PAIR] for s in s_bk]
    s_k = [s[:, PAIR:] for s in s_bk]
    n_mat = [jnp.where(strict, s[:C], 0.0) for s in s_b]
    n_bf = [n.astype(BF16) for n in n_mat]
    a_ak = [jnp.where(strict, s[:C], 0.0).astype(BF16) for s in s_k]
    a_rb = [jnp.where(incl, s[C:], 0.0).astype(BF16) for s in s_b]
    a_rk = [jnp.where(incl, s[C:], 0.0).astype(BF16) for s in s_k]
    yield

    lvl = ((row >> 1) == (col >> 1)) & ((row & 1) == 1) & ((col & 1) == 0)
    eye_f = (col == row).astype(F32)
    x = [eye_f + jnp.where(lvl, n, 0.0) for n in n_mat]
    s = 2
    while s < C:
        shift = s.bit_length()
        lvl = ((row >> shift) == (col >> shift)) & ((row & s) != 0) & ((col & s) == 0)
        lvl_a, lvl_b = as_keep(lvl & lane_a), as_keep(lvl & jnp.logical_not(lane_a))
        x_bf = [xi.astype(BF16) for xi in x]
        xn = _dots_paired(x_bf, [_block_diag(n, lvl_a, lvl_b) for n in n_bf])
        yield
        xnx = _dots_paired([t.astype(BF16) for t in xn], [bd(t) for t in x_bf])
        x = [x[i] + xnx[i] for i in range(nu)]
        yield
        s *= 2

    x_bf = [xi.astype(BF16) for xi in x]
    v_bd = [bd(one("v", u)) for u in units]
    av = [t.astype(BF16) for t in _dots_paired(a_ak, v_bd)]
    yield
    p_mat = [t.astype(BF16) for t in _dots_paired(x_bf, [bd(two("ar", u)[:C]) for u in units])]
    yield

    t_old = [jnp.where(first_chunk, 0.0, state_ref[u[0], :, lanes(u)]) for u in units]
    t_bd = [bd(t.astype(BF16)) for t in t_old]
    u_mat = _dots_paired(
        [jnp.concatenate([p_mat[i], x_bf[i]], axis=1) for i in range(nu)],
        [jnp.concatenate([t_bd[i], bd(av[i])], axis=0) for i in range(nu)])
    u_bf = [um.astype(BF16) for um in u_mat]
    yield
    y = _dots_paired(
        [jnp.concatenate([two("ar", u)[C:], a_rb[i], a_rk[i]], axis=1)
         for i, u in enumerate(units)],
        [jnp.concatenate([t_bd[i], bd(u_bf[i]), v_bd[i]], axis=0) for i in range(nu)])
    yield
    full = _dots_paired_tn(
        [two("bk_h", u) for u in units],
        [jnp.concatenate([u_bf[i], one("v", u)], axis=0) for i, u in enumerate(units)])
    for i, u in enumerate(units):
        state_ref[u[0], :, lanes(u)] = (one("g_col", u) * t_old[i]
                                        + jnp.where(lane_a, full[i][:N], full[i][N:]))
    yield

    for b in range(RWKV_BATCHES):
        rows = slice(b * C, (b + 1) * C)
        yb = jnp.concatenate(y[b * RWKV_PAIRS:(b + 1) * RWKV_PAIRS], axis=1)
        yc = yb - _head_sum(yb, lane_a) * (1.0 / N)
        var = _head_sum(yc * yc, lane_a) * (1.0 / N)
        yn = yc * lax.rsqrt(var + GN_EPS) * gng_ref[...] + gnb_ref[...]
        o_ref[b] = ((yn + buf["bonus"][rows, :]) * buf["gate"][rows, :]).astype(o_ref.dtype)
        yield


def _rwkv_kernel(chunks, z_ref, w0_ref, w2_ref, a0_ref, a2_ref, kk_ref, ka_ref, rk_ref,
                 gng_ref, gnb_ref, o_ref, state_ref, *bufs):
    names = [name for name, _, _ in RWKV_BUFFERS]
    sets = (dict(zip(names, bufs[:len(names)])), dict(zip(names, bufs[len(names):])))
    par = (w0_ref, w2_ref, a0_ref, a2_ref, kk_ref, ka_ref, rk_ref)
    step = pl.program_id(0)

    @pl.when(step == 0)
    def _():
        for ref in sets[1].values():
            ref[...] = jnp.zeros_like(ref)
        state_ref[...] = jnp.zeros_like(state_ref)

    first_consumed = step % chunks == 1

    def run(nxt, cur):
        prepare = _rwkv_prepare(z_ref, par, nxt)
        pieces = RWKV_PREPARE_PIECES * RWKV_BATCHES
        for stage, _ in enumerate(
                _rwkv_chunks(cur, state_ref, first_consumed, gng_ref, gnb_ref, o_ref)):
            due = min(stage + 1, RWKV_MATMUL_STAGES) * pieces // RWKV_MATMUL_STAGES
            done = min(stage, RWKV_MATMUL_STAGES) * pieces // RWKV_MATMUL_STAGES
            for _ in range(due - done):
                next(prepare)

    @pl.when(step % 2 == 0)
    def _():
        run(sets[0], sets[1])

    @pl.when(step % 2 == 1)
    def _():
        run(sets[1], sets[0])


def _rwkv(z_rwkv, w0, w2, a0, a2, k_k, k_a, r_k, gn_g, gn_b, batch, seq):
    C, NB = RWKV_CHUNK, RWKV_BATCHES
    chunks = seq // C
    n_blocks = (batch // NB) * chunks
    z3 = z_rwkv.reshape(batch, seq, C_RWKV)
    vec = lambda n: pl.BlockSpec((1, n), lambda s: (0, 0))
    lora = pl.BlockSpec((LORA, D_RWKV), lambda s: (0, 0))

    def block_of(s):
        return (s // chunks, s % chunks, 0)

    out = pl.pallas_call(
        functools.partial(_rwkv_kernel, chunks),
        grid=(n_blocks + 1,),
        in_specs=[
            pl.BlockSpec((NB, C, C_RWKV), lambda s: block_of(jnp.minimum(s, n_blocks - 1))),
            vec(D_RWKV), lora, vec(D_RWKV), lora,
            vec(D_RWKV), vec(D_RWKV), vec(D_RWKV), vec(D_RWKV), vec(D_RWKV),
        ],
        out_specs=pl.BlockSpec((NB, C, D_RWKV), lambda s: block_of(jnp.maximum(s - 1, 0))),
        out_shape=jax.ShapeDtypeStruct((batch, seq, D_RWKV), BF16),
        scratch_shapes=[
            pltpu.VMEM((NB, RWKV_HEAD_DIM, D_RWKV), F32),
        ] + 2 * [pltpu.VMEM((n, D_RWKV), dt) for _, n, dt in RWKV_BUFFERS],
        compiler_params=pltpu.CompilerParams(
            dimension_semantics=("arbitrary",), vmem_limit_bytes=VMEM_LIMIT_BYTES),
        name="rwkv",
    )(z3, w0, w2, a0, a2, k_k, k_a, r_k, gn_g, gn_b)
    return out.reshape(batch * seq, D_RWKV)


def _out_attn_front(x_ref, ys_ref, yr_ref, wout_ref, gx_ref, wq_ref, h1_ref, q_ref):
    y = jnp.concatenate([ys_ref[...], yr_ref[...]], axis=1)
    h1 = x_ref[...] + jnp.dot(y, wout_ref[...], preferred_element_type=F32)
    h1_ref[...] = h1
    yield
    hn = _rms_norm(h1, gx_ref[...]).astype(BF16)
    yield
    q_ref[...] = jnp.dot(hn, wq_ref[...], preferred_element_type=F32).astype(BF16)
    yield


def _out_attn_back(h1_ref, q_ref, k_ref, v_ref, wo_ref, gf_ref, o_ref):
    cols = [slice(h * XATTN_HEAD_DIM, (h + 1) * XATTN_HEAD_DIM) for h in range(XATTN_HEADS)]
    scores = [_dot_nt(q_ref[:, sl], k_ref[:, sl]) for sl in cols]
    yield
    probs = []
    for s in scores:
        e = jnp.exp(s - jnp.max(s, axis=-1, keepdims=True))
        probs.append((e / jnp.sum(e, axis=-1, keepdims=True)).astype(BF16))
    yield
    o = jnp.concatenate(
        [jnp.dot(p, v_ref[:, sl], preferred_element_type=F32).astype(BF16)
         for p, sl in zip(probs, cols)], axis=-1)
    yield
    h2 = h1_ref[...] + jnp.dot(o, wo_ref[...], preferred_element_type=F32)
    yield
    o_ref[...] = _rms_norm(h2, gf_ref[...])
    yield


OUT_ATTN_ORDER = "BFBBFBBF"


def _out_attn_kernel(tiles_per_batch, n_tiles, x_ref, ys_ref, yr_ref, wout_ref, gx_ref, wq_ref,
                     mem_ref, gm_ref, wkv_ref, wo_ref, gf_ref, o_ref,
                     h1_a, q_a, h1_b, q_b, k_scr, v_scr):
    step = pl.program_id(0)

    @pl.when(step == 0)
    def _():
        h1_b[...] = jnp.zeros_like(h1_b)
        q_b[...] = jnp.zeros_like(q_b)

    @pl.when((step % tiles_per_batch == 0) & (step < n_tiles))
    def _():
        slot = (step // tiles_per_batch) % 2
        mn = _rms_norm(mem_ref[...], gm_ref[...]).astype(BF16)
        k = jnp.dot(mn, wkv_ref[:, :D_MODEL], preferred_element_type=F32)
        k_scr[slot] = (k * (XATTN_HEAD_DIM ** -0.5)).astype(BF16)
        v_scr[slot] = jnp.dot(mn, wkv_ref[:, D_MODEL:], preferred_element_type=F32).astype(BF16)

    back_slot = (jnp.maximum(step - 1, 0) // tiles_per_batch) % 2
    k_ref, v_ref = k_scr.at[back_slot], v_scr.at[back_slot]

    def run(nxt, cur):
        front = _out_attn_front(x_ref, ys_ref, yr_ref, wout_ref, gx_ref, wq_ref, *nxt)
        back = _out_attn_back(*cur, k_ref, v_ref, wo_ref, gf_ref, o_ref)
        for stage in OUT_ATTN_ORDER:
            next(front if stage == "F" else back)

    @pl.when(step % 2 == 0)
    def _():
        run((h1_a, q_a), (h1_b, q_b))

    @pl.when(step % 2 == 1)
    def _():
        run((h1_b, q_b), (h1_a, q_a))


def _out_attn(x2, y_sgu, y_rwkv, w_out, g_x, w_q, mem2, g_mem, w_kv, w_o, g_f, seq):
    m = x2.shape[0]
    tq = ATTN_ROWS
    per_batch = seq // tq
    n_tiles = m // tq
    front_tile = lambda s: jnp.minimum(s, n_tiles - 1)
    front_blk = lambda n: pl.BlockSpec((tq, n), lambda s: (front_tile(s), 0))
    full = lambda a, b: pl.BlockSpec((a, b), lambda s: (0, 0))
    mem_blk = pl.BlockSpec((MEM_LEN, D_MODEL), lambda s: (front_tile(s) // per_batch, 0))
    return pl.pallas_call(
        functools.partial(_out_attn_kernel, per_batch, n_tiles),
        grid=(n_tiles + 1,),
        in_specs=[
            front_blk(D_MODEL), front_blk(D_SGU), front_blk(D_RWKV),
            full(D_SGU + D_RWKV, D_MODEL), full(1, D_MODEL), full(D_MODEL, D_MODEL),
            mem_blk, full(1, D_MODEL), full(D_MODEL, 2 * D_MODEL),
            full(D_MODEL, D_MODEL), full(1, D_MODEL),
        ],
        out_specs=pl.BlockSpec((tq, D_MODEL), lambda s: (jnp.maximum(s - 1, 0), 0)),
        out_shape=jax.ShapeDtypeStruct((m, D_MODEL), F32),
        scratch_shapes=2 * [pltpu.VMEM((tq, D_MODEL), F32), pltpu.VMEM((tq, D_MODEL), BF16)]
        + 2 * [pltpu.VMEM((2, MEM_LEN, D_MODEL), BF16)],
        compiler_params=pltpu.CompilerParams(
            dimension_semantics=("arbitrary",), vmem_limit_bytes=VMEM_LIMIT_BYTES),
        name="out_attn",
    )(x2, y_sgu, y_rwkv, w_out, g_x, w_q, mem2, g_mem, w_kv, w_o, g_f)


def kernel(x, mem, ln_mix_g, w_in, sgu_ln_g, sgu_ln_b, sgu_ws, sgu_bs, sgu_out_g, rw_mu, rw_w0, rw_w2, rw_a0, rw_a2, rw_k_k, rw_k_a, rw_r_k, rw_gn_g, rw_gn_b, w_out, ln_x_g, ln_mem_g, w_q, w_kv, w_o, ln_f_g):
    batch, seq, _ = x.shape
    assert w_in.shape[0] == 1, "the final norm is fused into the only layer's last call"
    row = lambda a: a.reshape(1, -1)
    h = x.reshape(batch * seq, D_MODEL)
    mem2 = mem.reshape(batch * MEM_LEN, D_MODEL)
    for l in range(1):
        bs_full = jnp.repeat(sgu_bs[l].T, SGU_HEAD_DIM, axis=1)
        y_sgu, z_rwkv = _in_proj(h, row(ln_mix_g[l]), w_in[l].astype(BF16), row(sgu_ln_g[l]),
                                 row(sgu_ln_b[l]), sgu_ws[l], bs_full, row(sgu_out_g[l]),
                                 row(rw_mu[l]), seq)
        y_rwkv = _rwkv(z_rwkv, row(rw_w0[l]), rw_w2[l].astype(BF16),
                       row(rw_a0[l]), rw_a2[l].astype(BF16), row(rw_k_k[l]), row(rw_k_a[l]),
                       row(rw_r_k[l]), row(rw_gn_g[l]), row(rw_gn_b[l]), batch, seq)
        h = _out_attn(h, y_sgu, y_rwkv, w_out[l].astype(BF16), row(ln_x_g[l]),
                      w_q[l].astype(BF16), mem2, row(ln_mem_g[l]), w_kv[l].astype(BF16),
                      w_o[l].astype(BF16), row(ln_f_g), seq)
    return h.reshape(batch, seq, D_MODEL)
```

```python
import functools
import math

import jax
import jax.numpy as jnp
from jax import lax
from jax.experimental import pallas as pl
from jax.experimental.pallas import tpu as pltpu

D_MODEL = 1024
MEM_LEN = 256
D_SGU = 512
SGU_HEADS = 4
SGU_HEAD_DIM = D_SGU // SGU_HEADS
SGU_CHUNK = 128
D_RWKV = 512
RWKV_HEAD_DIM = 64
RWKV_HEADS = D_RWKV // RWKV_HEAD_DIM
LORA = 64
C_SGU = 3 * D_SGU
C_RWKV = 4 * D_RWKV + 2 * LORA
XATTN_HEADS = 4
XATTN_HEAD_DIM = D_MODEL // XATTN_HEADS
RMS_EPS = 1e-6
LN_EPS = 1e-5
GN_EPS = 64e-5

RWKV_CHUNK = 64
RWKV_BATCHES = 8
PAIR = 2 * RWKV_HEAD_DIM
RWKV_PAIRS = RWKV_HEADS // 2
IN_PROJ_ROWS = 512
ATTN_ROWS = 512
VMEM_LIMIT_BYTES = 48 * 1024 * 1024

SUBLANES = 8
LOG2_E = math.log2(math.e)

F32 = jnp.float32
BF16 = jnp.bfloat16


def _dot(a, b):
    return jnp.dot(a.astype(BF16), b.astype(BF16), preferred_element_type=F32)


def _dot_nt(a, b):
    return lax.dot_general(a.astype(BF16), b.astype(BF16), (((1,), (1,)), ((), ())),
                           preferred_element_type=F32)


def _dot_tn(a, b):
    return lax.dot_general(a.astype(BF16), b.astype(BF16), (((0,), (0,)), ((), ())),
                           preferred_element_type=F32)


def _rms_norm(x, g):
    return x * lax.rsqrt(jnp.mean(x * x, axis=-1, keepdims=True) + RMS_EPS) * g


def _gelu(x):
    return 0.5 * x * (1.0 + lax.erf(x * (2.0 ** -0.5)))


def _silu(x):
    return x * jax.nn.sigmoid(x)


def _sgu_chunk(z, lng_ref, lnb_ref, ws_c, bs_ref, og_ref):
    u = _gelu(z[:, 0:D_SGU])
    v = _gelu(z[:, D_SGU:2 * D_SGU])
    gate = z[:, 2 * D_SGU:3 * D_SGU]
    vc = v - jnp.mean(v, axis=-1, keepdims=True)
    var = jnp.mean(vc * vc, axis=-1, keepdims=True)
    vn = (vc * lax.rsqrt(var + LN_EPS) * lng_ref[...] + lnb_ref[...]).astype(BF16)
    sv = jnp.concatenate(
        [jnp.dot(ws_c[h], vn[:, h * SGU_HEAD_DIM:(h + 1) * SGU_HEAD_DIM],
                 preferred_element_type=F32) for h in range(SGU_HEADS)], axis=1) + bs_ref[...]
    return _rms_norm(u * sv, og_ref[...]) * _silu(gate)


def _in_proj_kernel(tiles_per_batch, x_ref, g_ref, w_ref, lng_ref, lnb_ref, ws_ref, bs_ref, og_ref,
                    mu_ref, ys_ref, zr_ref, zs_ref, prev_ref):
    tile = pl.program_id(0)

    @pl.when(tile == 0)
    def _():
        prev_ref[...] = jnp.zeros_like(prev_ref)

    first_tile = tile % tiles_per_batch == 0
    row8 = lax.broadcasted_iota(jnp.int32, (SUBLANES, 1), 0)
    xn = _rms_norm(x_ref[...], g_ref[...]).astype(BF16)
    zs_ref[...] = jnp.dot(xn, w_ref[:, :C_SGU], preferred_element_type=F32)
    row = lax.broadcasted_iota(jnp.int32, (SGU_CHUNK, SGU_CHUNK), 0)
    col = lax.broadcasted_iota(jnp.int32, (SGU_CHUNK, SGU_CHUNK), 1)
    ws_c = [jnp.where(col <= row, ws_ref[h], 0.0).astype(BF16) for h in range(SGU_HEADS)]
    n_chunks = IN_PROJ_ROWS // SGU_CHUNK
    lane_tiles = C_RWKV // PAIR
    bounds = [C_SGU + PAIR * (lane_tiles * c // n_chunks) for c in range(n_chunks + 1)]
    gate_cols = slice(3 * D_RWKV, 4 * D_RWKV)
    for c in range(n_chunks):
        z = jnp.dot(xn, w_ref[:, bounds[c]:bounds[c + 1]], preferred_element_type=F32)
        cols = slice(bounds[c] - C_SGU, bounds[c + 1] - C_SGU)
        prev = jnp.where(first_tile, 0.0, prev_ref[:, cols])
        rolled = pltpu.roll(z, 1, axis=0)
        z_prev = jnp.concatenate(
            [jnp.where(row8 == 0, prev, rolled[:SUBLANES]), rolled[SUBLANES:]], axis=0)
        prev_ref[:, cols] = z[IN_PROJ_ROWS - 1:IN_PROJ_ROWS, :]
        zs = z + (z_prev - z) * mu_ref[:, cols]
        if cols.start <= gate_cols.start < cols.stop:
            assert gate_cols.stop <= cols.stop
            lo, hi = gate_cols.start - cols.start, gate_cols.stop - cols.start
            pieces = [zs[:, :lo], _silu(zs[:, lo:hi]), zs[:, hi:]]
            zs = jnp.concatenate([p for p in pieces if p.shape[1]], axis=1)
        zr_ref[:, cols] = zs
        rows = slice(c * SGU_CHUNK, (c + 1) * SGU_CHUNK)
        ys_ref[rows, :] = _sgu_chunk(zs_ref[rows, :], lng_ref, lnb_ref, ws_c, bs_ref,
                                     og_ref).astype(ys_ref.dtype)


def _in_proj(x2, g, w_in, ln_g, ln_b, ws, bs_full, out_g, mu, seq):
    m = x2.shape[0]
    tm = IN_PROJ_ROWS
    vec = pl.BlockSpec((1, D_SGU), lambda i: (0, 0))
    return pl.pallas_call(
        functools.partial(_in_proj_kernel, seq // tm),
        grid=(m // tm,),
        in_specs=[
            pl.BlockSpec((tm, D_MODEL), lambda i: (i, 0)),
            pl.BlockSpec((1, D_MODEL), lambda i: (0, 0)),
            pl.BlockSpec((D_MODEL, C_SGU + C_RWKV), lambda i: (0, 0)),
            vec, vec,
            pl.BlockSpec((SGU_HEADS, SGU_CHUNK, SGU_CHUNK), lambda i: (0, 0, 0)),
            pl.BlockSpec((SGU_CHUNK, D_SGU), lambda i: (0, 0)),
            vec,
            pl.BlockSpec((1, C_RWKV), lambda i: (0, 0)),
        ],
        out_specs=[
            pl.BlockSpec((tm, D_SGU), lambda i: (i, 0)),
            pl.BlockSpec((tm, C_RWKV), lambda i: (i, 0)),
        ],
        out_shape=[
            jax.ShapeDtypeStruct((m, D_SGU), BF16),
            jax.ShapeDtypeStruct((m, C_RWKV), F32),
        ],
        scratch_shapes=[
            pltpu.VMEM((tm, C_SGU), F32),
            pltpu.VMEM((1, C_RWKV), F32),
        ],
        compiler_params=pltpu.CompilerParams(
            dimension_semantics=("arbitrary",), vmem_limit_bytes=VMEM_LIMIT_BYTES),
        name="in_proj",
    )(x2, g, w_in, ln_g, ln_b, ws, bs_full, out_g, mu)


RWKV_ROWS = RWKV_CHUNK * RWKV_BATCHES
RWKV_MATMUL_STAGES = 16
RWKV_PREPARE_PIECES = 3
RWKV_BUFFERS = (
    ("ar", 2 * RWKV_ROWS, BF16),
    ("bk_h", 2 * RWKV_ROWS, BF16),
    ("b_t", RWKV_ROWS, BF16),
    ("k_t", RWKV_ROWS, BF16),
    ("v", RWKV_ROWS, BF16),
    ("g_col", RWKV_ROWS, F32),
    ("bonus", RWKV_ROWS, F32),
    ("gate", RWKV_ROWS, F32),
)


def _block_diag(x, keep_a, keep_b):
    return jnp.concatenate([x * keep_a, x * keep_b], axis=0)


def _dots_paired(lhs, rhs):
    out = []
    for i in range(0, len(lhs), 2):
        m, n = lhs[i].shape[0], rhs[i].shape[1]
        both = jnp.dot(jnp.concatenate([lhs[i], lhs[i + 1]], axis=0),
                       jnp.concatenate([rhs[i], rhs[i + 1]], axis=1),
                       preferred_element_type=F32)
        out += [both[:m, :n], both[m:, n:]]
    return out


def _dots_paired_tn(lhs, rhs):
    out = []
    for i in range(0, len(lhs), 2):
        m, n = lhs[i].shape[1], rhs[i].shape[1]
        both = _dot_tn(jnp.concatenate([lhs[i], lhs[i + 1]], axis=1),
                       jnp.concatenate([rhs[i], rhs[i + 1]], axis=1))
        out += [both[:m, :n], both[m:, n:]]
    return out


def _head_sum(x, lane_a):
    outs = []
    for p in range(RWKV_PAIRS):
        xp = x[:, p * PAIR:(p + 1) * PAIR]
        sum_a = jnp.sum(jnp.where(lane_a, xp, 0.0), axis=-1, keepdims=True)
        sum_b = jnp.sum(jnp.where(lane_a, 0.0, xp), axis=-1, keepdims=True)
        outs.append(jnp.where(lane_a, sum_a, sum_b))
    return jnp.concatenate(outs, axis=-1)


def _rwkv_prepare(z_ref, par, out):
    C, N = RWKV_CHUNK, RWKV_HEAD_DIM
    w0_ref, w2_ref, a0_ref, a2_ref, kk_ref, ka_ref, rk_ref = par
    lane_c = lax.broadcasted_iota(jnp.int32, (C, PAIR), 1)
    lane_a = lane_c < N
    eye = (lane_c & (N - 1)) == lax.broadcasted_iota(jnp.int32, (C, PAIR), 0)
    eye_full = jnp.concatenate([eye] * RWKV_PAIRS, axis=1)
    row_c = lax.broadcasted_iota(jnp.int32, (C, C), 0)
    col_c = lax.broadcasted_iota(jnp.int32, (C, C), 1)
    tril = (col_c <= row_c).astype(F32).astype(BF16)
    tril3 = jnp.concatenate([tril, tril, tril], axis=1)

    def one_batch(b):
        r = z_ref[b, :, 0:D_RWKV]
        k = z_ref[b, :, D_RWKV:2 * D_RWKV]
        v = z_ref[b, :, 2 * D_RWKV:3 * D_RWKV]
        gate_act = z_ref[b, :, 3 * D_RWKV:4 * D_RWKV]
        wd = z_ref[b, :, 4 * D_RWKV:4 * D_RWKV + LORA]
        ad = z_ref[b, :, 4 * D_RWKV + LORA:4 * D_RWKV + 2 * LORA]

        lw = w0_ref[...] + _dot(jnp.tanh(wd), w2_ref[...])
        ld = -(math.exp(-0.5) * LOG2_E) * jax.nn.sigmoid(lw)
        icl = jax.nn.sigmoid(a0_ref[...] + _dot(ad, a2_ref[...]))
        yield
        k2 = k * ((1.0 - ka_ref[...]) + ka_ref[...] * icl)
        kk = k * kk_ref[...]
        kk = kk * lax.rsqrt(jnp.maximum(_head_sum(kk * kk, lane_a), 1e-24))
        bvec = kk * icl

        p1 = ld.astype(BF16)
        r1 = ld - p1.astype(F32)
        p2 = r1.astype(BF16)
        p3 = (r1 - p2.astype(F32)).astype(BF16)
        cum = jnp.dot(tril3, jnp.concatenate([p1, p2, p3], axis=0), preferred_element_type=F32)
        last = cum[C - 1:C, :]
        yield
        e_neg = jnp.exp2(-cum)
        g_last = jnp.exp2(last)
        a_t = -kk * jnp.exp2(cum - ld)
        r_t = r * jnp.exp2(cum)
        b_t = bvec * e_neg
        k_t = k2 * e_neg

        stacked = {
            "ar": (a_t, r_t),
            "bk_h": (b_t * g_last, k_t * g_last),
        }
        for name, (top, bot) in stacked.items():
            out[name][2 * b * C:(2 * b + 1) * C, :] = top.astype(BF16)
            out[name][(2 * b + 1) * C:(2 * b + 2) * C, :] = bot.astype(BF16)
        rows = slice(b * C, (b + 1) * C)
        out["b_t"][rows, :] = b_t.astype(BF16)
        out["k_t"][rows, :] = k_t.astype(BF16)
        out["v"][rows, :] = v.astype(BF16)
        out["g_col"][rows, :] = _head_sum(jnp.where(eye_full, g_last, 0.0), lane_a)
        out["bonus"][rows, :] = _head_sum(r * k2 * rk_ref[...], lane_a) * v
        out["gate"][rows, :] = gate_act
        yield

    batches = [one_batch(b) for b in range(RWKV_BATCHES)]
    for _ in range(RWKV_PREPARE_PIECES):
        for gen in batches:
            next(gen)
            yield


def _rwkv_chunks(buf, state_ref, first_chunk, gng_ref, gnb_ref, o_ref):
    C, N = RWKV_CHUNK, RWKV_HEAD_DIM
    row = lax.broadcasted_iota(jnp.int32, (C, PAIR), 0)
    lane = lax.broadcasted_iota(jnp.int32, (C, PAIR), 1)
    col = lane & (N - 1)
    lane_a = lane < N
    incl = col <= row
    strict = col < row

    units = [(b, p) for b in range(RWKV_BATCHES) for p in range(RWKV_PAIRS)]
    nu = len(units)
    lanes = lambda u: slice(u[1] * PAIR, (u[1] + 1) * PAIR)
    two = lambda name, u: buf[name][2 * u[0] * C:(2 * u[0] + 2) * C, lanes(u)]
    one = lambda name, u: buf[name][u[0] * C:(u[0] + 1) * C, lanes(u)]
    as_keep = lambda mask: jnp.where(mask, 1.0, 0.0).astype(BF16)
    keep_a, keep_b = as_keep(lane_a), as_keep(jnp.logical_not(lane_a))
    bd = lambda x: _block_diag(x, keep_a, keep_b)

    s_bk = [_dot_nt(two("ar", u), jnp.concatenate([bd(one("b_t", u)), bd(one("k_t", u))], axis=0))
            for u in units]
    yield
    s_b = [s[:, :PAIR] for s in s_bk]
    s_k = [s[:, PAIR:] for s in s_bk]
    n_mat = [jnp.where(strict, s[:C], 0.0) for s in s_b]
    n_bf = [n.astype(BF16) for n in n_mat]
    a_ak = [jnp.where(strict, s[:C], 0.0).astype(BF16) for s in s_k]
    a_rb = [jnp.where(incl, s[C:], 0.0).astype(BF16) for s in s_b]
    a_rk = [jnp.where(incl, s[C:], 0.0).astype(BF16) for s in s_k]
    yield

    lvl = ((row >> 1) == (col >> 1)) & ((row & 1) == 1) & ((col & 1) == 0)
    eye_f = (col == row).astype(F32)
    x = [eye_f + jnp.where(lvl, n, 0.0) for n in n_mat]
    s = 2
    while s < C:
        shift = s.bit_length()
        lvl = ((row >> shift) == (col >> shift)) & ((row & s) != 0) & ((col & s) == 0)
        lvl_a, lvl_b = as_keep(lvl & lane_a), as_keep(lvl & jnp.logical_not(lane_a))
        x_bf = [xi.astype(BF16) for xi in x]
        if s < SUBLANES:
            xn = _dots_paired(x_bf, [_block_diag(n, lvl_a, lvl_b) for n in n_bf])
            yield
            xnx = _dots_paired([t.astype(BF16) for t in xn], [bd(t) for t in x_bf])
            x = [x[i] + xnx[i] for i in range(nu)]
            yield
        else:
            groups = [g for g in range(C // SUBLANES) if (g * SUBLANES) & s]
            rows_of = lambda m, g: m[g * SUBLANES:(g + 1) * SUBLANES]
            moving = [jnp.concatenate([rows_of(xi, g) for g in groups], axis=0).astype(BF16)
                      for xi in x]
            xn = _dots_paired(moving, [_block_diag(n, lvl_a, lvl_b) for n in n_bf])
            yield
            xnx = _dots_paired([t.astype(BF16) for t in xn], [bd(t) for t in x_bf])
            x = [jnp.concatenate(
                [rows_of(x[i], g) + rows_of(xnx[i], groups.index(g)) if g in groups
                 else rows_of(x[i], g) for g in range(C // SUBLANES)], axis=0)
                 for i in range(nu)]
            yield
        s *= 2

    x_bf = [xi.astype(BF16) for xi in x]
    v_bd = [bd(one("v", u)) for u in units]
    av = [t.astype(BF16) for t in _dots_paired(a_ak, v_bd)]
    yield
    p_mat = [t.astype(BF16) for t in _dots_paired(x_bf, [bd(two("ar", u)[:C]) for u in units])]
    yield

    t_old = [jnp.where(first_chunk, 0.0, state_ref[u[0], :, lanes(u)]) for u in units]
    t_bd = [bd(t.astype(BF16)) for t in t_old]
    u_mat = _dots_paired(
        [jnp.concatenate([p_mat[i], x_bf[i]], axis=1) for i in range(nu)],
        [jnp.concatenate([t_bd[i], bd(av[i])], axis=0) for i in range(nu)])
    u_bf = [um.astype(BF16) for um in u_mat]
    yield
    y = _dots_paired(
        [jnp.concatenate([two("ar", u)[C:], a_rb[i], a_rk[i]], axis=1)
         for i, u in enumerate(units)],
        [jnp.concatenate([t_bd[i], bd(u_bf[i]), v_bd[i]], axis=0) for i in range(nu)])
    yield
    full = _dots_paired_tn(
        [two("bk_h", u) for u in units],
        [jnp.concatenate([u_bf[i], one("v", u)], axis=0) for i, u in enumerate(units)])
    for i, u in enumerate(units):
        state_ref[u[0], :, lanes(u)] = (one("g_col", u) * t_old[i]
                                        + jnp.where(lane_a, full[i][:N], full[i][N:]))
    yield

    for b in range(RWKV_BATCHES):
        rows = slice(b * C, (b + 1) * C)
        yb = jnp.concatenate(y[b * RWKV_PAIRS:(b + 1) * RWKV_PAIRS], axis=1)
        yc = yb - _head_sum(yb, lane_a) * (1.0 / N)
        var = _head_sum(yc * yc, lane_a) * (1.0 / N)
        yn = yc * lax.rsqrt(var + GN_EPS) * gng_ref[...] + gnb_ref[...]
        o_ref[b] = ((yn + buf["bonus"][rows, :]) * buf["gate"][rows, :]).astype(o_ref.dtype)
        yield


def _rwkv_kernel(chunks, z_ref, w0_ref, w2_ref, a0_ref, a2_ref, kk_ref, ka_ref, rk_ref,
                 gng_ref, gnb_ref, o_ref, state_ref, *bufs):
    names = [name for name, _, _ in RWKV_BUFFERS]
    sets = (dict(zip(names, bufs[:len(names)])), dict(zip(names, bufs[len(names):])))
    par = (w0_ref, w2_ref, a0_ref, a2_ref, kk_ref, ka_ref, rk_ref)
    step = pl.program_id(0)

    @pl.when(step == 0)
    def _():
        for ref in sets[1].values():
            ref[...] = jnp.zeros_like(ref)
        state_ref[...] = jnp.zeros_like(state_ref)

    first_consumed = step % chunks == 1

    def run(nxt, cur):
        prepare = _rwkv_prepare(z_ref, par, nxt)
        pieces = RWKV_PREPARE_PIECES * RWKV_BATCHES
        for stage, _ in enumerate(
                _rwkv_chunks(cur, state_ref, first_consumed, gng_ref, gnb_ref, o_ref)):
            due = min(stage + 1, RWKV_MATMUL_STAGES) * pieces // RWKV_MATMUL_STAGES
            done = min(stage, RWKV_MATMUL_STAGES) * pieces // RWKV_MATMUL_STAGES
            for _ in range(due - done):
                next(prepare)

    @pl.when(step % 2 == 0)
    def _():
        run(sets[0], sets[1])

    @pl.when(step % 2 == 1)
    def _():
        run(sets[1], sets[0])


def _rwkv(z_rwkv, w0, w2, a0, a2, k_k, k_a, r_k, gn_g, gn_b, batch, seq):
    C, NB = RWKV_CHUNK, RWKV_BATCHES
    chunks = seq // C
    n_blocks = (batch // NB) * chunks
    z3 = z_rwkv.reshape(batch, seq, C_RWKV)
    vec = lambda n: pl.BlockSpec((1, n), lambda s: (0, 0))
    lora = pl.BlockSpec((LORA, D_RWKV), lambda s: (0, 0))

    def block_of(s):
        return (s // chunks, s % chunks, 0)

    out = pl.pallas_call(
        functools.partial(_rwkv_kernel, chunks),
        grid=(n_blocks + 1,),
        in_specs=[
            pl.BlockSpec((NB, C, C_RWKV), lambda s: block_of(jnp.minimum(s, n_blocks - 1))),
            vec(D_RWKV), lora, vec(D_RWKV), lora,
            vec(D_RWKV), vec(D_RWKV), vec(D_RWKV), vec(D_RWKV), vec(D_RWKV),
        ],
        out_specs=pl.BlockSpec((NB, C, D_RWKV), lambda s: block_of(jnp.maximum(s - 1, 0))),
        out_shape=jax.ShapeDtypeStruct((batch, seq, D_RWKV), BF16),
        scratch_shapes=[
            pltpu.VMEM((NB, RWKV_HEAD_DIM, D_RWKV), F32),
        ] + 2 * [pltpu.VMEM((n, D_RWKV), dt) for _, n, dt in RWKV_BUFFERS],
        compiler_params=pltpu.CompilerParams(
            dimension_semantics=("arbitrary",), vmem_limit_bytes=VMEM_LIMIT_BYTES),
        name="rwkv",
    )(z3, w0, w2, a0, a2, k_k, k_a, r_k, gn_g, gn_b)
    return out.reshape(batch * seq, D_RWKV)


def _out_attn_front(x_ref, ys_ref, yr_ref, wout_ref, gx_ref, wq_ref, h1_ref, q_ref):
    y = jnp.concatenate([ys_ref[...], yr_ref[...]], axis=1)
    h1 = x_ref[...] + jnp.dot(y, wout_ref[...], preferred_element_type=F32)
    h1_ref[...] = h1
    yield
    hn = _rms_norm(h1, gx_ref[...]).astype(BF16)
    yield
    q_ref[...] = jnp.dot(hn, wq_ref[...], preferred_element_type=F32).astype(BF16)
    yield


def _out_attn_back(h1_ref, q_ref, k_ref, v_ref, wo_ref, gf_ref, o_ref):
    cols = [slice(h * XATTN_HEAD_DIM, (h + 1) * XATTN_HEAD_DIM) for h in range(XATTN_HEADS)]
    scores = [_dot_nt(q_ref[:, sl], k_ref[:, sl]) for sl in cols]
    yield
    probs = []
    for s in scores:
        e = jnp.exp(s - jnp.max(s, axis=-1, keepdims=True))
        probs.append((e / jnp.sum(e, axis=-1, keepdims=True)).astype(BF16))
    yield
    o = jnp.concatenate(
        [jnp.dot(p, v_ref[:, sl], preferred_element_type=F32).astype(BF16)
         for p, sl in zip(probs, cols)], axis=-1)
    yield
    h2 = h1_ref[...] + jnp.dot(o, wo_ref[...], preferred_element_type=F32)
    yield
    o_ref[...] = _rms_norm(h2, gf_ref[...])
    yield


OUT_ATTN_ORDER = "BFBBFBBF"


def _out_attn_kernel(tiles_per_batch, n_tiles, x_ref, ys_ref, yr_ref, wout_ref, gx_ref, wq_ref,
                     mem_ref, gm_ref, wkv_ref, wo_ref, gf_ref, o_ref,
                     h1_a, q_a, h1_b, q_b, k_scr, v_scr):
    step = pl.program_id(0)

    @pl.when(step == 0)
    def _():
        h1_b[...] = jnp.zeros_like(h1_b)
        q_b[...] = jnp.zeros_like(q_b)

    @pl.when((step % tiles_per_batch == 0) & (step < n_tiles))
    def _():
        slot = (step // tiles_per_batch) % 2
        mn = _rms_norm(mem_ref[...], gm_ref[...]).astype(BF16)
        k = jnp.dot(mn, wkv_ref[:, :D_MODEL], preferred_element_type=F32)
        k_scr[slot] = (k * (XATTN_HEAD_DIM ** -0.5)).astype(BF16)
        v_scr[slot] = jnp.dot(mn, wkv_ref[:, D_MODEL:], preferred_element_type=F32).astype(BF16)

    back_slot = (jnp.maximum(step - 1, 0) // tiles_per_batch) % 2
    k_ref, v_ref = k_scr.at[back_slot], v_scr.at[back_slot]

    def run(nxt, cur):
        front = _out_attn_front(x_ref, ys_ref, yr_ref, wout_ref, gx_ref, wq_ref, *nxt)
        back = _out_attn_back(*cur, k_ref, v_ref, wo_ref, gf_ref, o_ref)
        for stage in OUT_ATTN_ORDER:
            next(front if stage == "F" else back)

    @pl.when(step % 2 == 0)
    def _():
        run((h1_a, q_a), (h1_b, q_b))

    @pl.when(step % 2 == 1)
    def _():
        run((h1_b, q_b), (h1_a, q_a))


def _out_attn(x2, y_sgu, y_rwkv, w_out, g_x, w_q, mem2, g_mem, w_kv, w_o, g_f, seq):
    m = x2.shape[0]
    tq = ATTN_ROWS
    per_batch = seq // tq
    n_tiles = m // tq
    front_tile = lambda s: jnp.minimum(s, n_tiles - 1)
    front_blk = lambda n: pl.BlockSpec((tq, n), lambda s: (front_tile(s), 0))
    full = lambda a, b: pl.BlockSpec((a, b), lambda s: (0, 0))
    mem_blk = pl.BlockSpec((MEM_LEN, D_MODEL), lambda s: (front_tile(s) // per_batch, 0))
    return pl.pallas_call(
        functools.partial(_out_attn_kernel, per_batch, n_tiles),
        grid=(n_tiles + 1,),
        in_specs=[
            front_blk(D_MODEL), front_blk(D_SGU), front_blk(D_RWKV),
            full(D_SGU + D_RWKV, D_MODEL), full(1, D_MODEL), full(D_MODEL, D_MODEL),
            mem_blk, full(1, D_MODEL), full(D_MODEL, 2 * D_MODEL),
            full(D_MODEL, D_MODEL), full(1, D_MODEL),
        ],
        out_specs=pl.BlockSpec((tq, D_MODEL), lambda s: (jnp.maximum(s - 1, 0), 0)),
        out_shape=jax.ShapeDtypeStruct((m, D_MODEL), F32),
        scratch_shapes=2 * [pltpu.VMEM((tq, D_MODEL), F32), pltpu.VMEM((tq, D_MODEL), BF16)]
        + 2 * [pltpu.VMEM((2, MEM_LEN, D_MODEL), BF16)],
        compiler_params=pltpu.CompilerParams(
            dimension_semantics=("arbitrary",), vmem_limit_bytes=VMEM_LIMIT_BYTES),
        name="out_attn",
    )(x2, y_sgu, y_rwkv, w_out, g_x, w_q, mem2, g_mem, w_kv, w_o, g_f)


def kernel(x, mem, ln_mix_g, w_in, sgu_ln_g, sgu_ln_b, sgu_ws, sgu_bs, sgu_out_g, rw_mu, rw_w0, rw_w2, rw_a0, rw_a2, rw_k_k, rw_k_a, rw_r_k, rw_gn_g, rw_gn_b, w_out, ln_x_g, ln_mem_g, w_q, w_kv, w_o, ln_f_g):
    batch, seq, _ = x.shape
    assert w_in.shape[0] == 1, "the final norm is fused into the only layer's last call"
    row = lambda a: a.reshape(1, -1)
    h = x.reshape(batch * seq, D_MODEL)
    mem2 = mem.reshape(batch * MEM_LEN, D_MODEL)
    for l in range(1):
        bs_full = jnp.repeat(sgu_bs[l].T, SGU_HEAD_DIM, axis=1)
        y_sgu, z_rwkv = _in_proj(h, row(ln_mix_g[l]), w_in[l].astype(BF16), row(sgu_ln_g[l]),
                                 row(sgu_ln_b[l]), sgu_ws[l], bs_full, row(sgu_out_g[l]),
                                 row(rw_mu[l]), seq)
        y_rwkv = _rwkv(z_rwkv, row(rw_w0[l]), rw_w2[l].astype(BF16),
                       row(rw_a0[l]), rw_a2[l].astype(BF16), row(rw_k_k[l]), row(rw_k_a[l]),
                       row(rw_r_k[l]), row(rw_gn_g[l]), row(rw_gn_b[l]), batch, seq)
        h = _out_attn(h, y_sgu, y_rwkv, w_out[l].astype(BF16), row(ln_x_g[l]),
                      w_q[l].astype(BF16), mem2, row(ln_mem_g[l]), w_kv[l].astype(BF16),
                      w_o[l].astype(BF16), row(ln_f_g), seq)
    return h.reshape(batch, seq, D_MODEL)
```

```python
import functools
import math

import jax
import jax.numpy as jnp
from jax import lax
from jax.experimental import pallas as pl
from jax.experimental.pallas import tpu as pltpu

D_MODEL = 1024
MEM_LEN = 256
D_SGU = 512
SGU_HEADS = 4
SGU_HEAD_DIM = D_SGU // SGU_HEADS
SGU_CHUNK = 128
D_RWKV = 512
RWKV_HEAD_DIM = 64
RWKV_HEADS = D_RWKV // RWKV_HEAD_DIM
LORA = 64
C_SGU = 3 * D_SGU
C_RWKV = 4 * D_RWKV + 2 * LORA
XATTN_HEADS = 4
XATTN_HEAD_DIM = D_MODEL // XATTN_HEADS
RMS_EPS = 1e-6
LN_EPS = 1e-5
GN_EPS = 64e-5

RWKV_CHUNK = 64
RWKV_BATCHES = 8
PAIR = 2 * RWKV_HEAD_DIM
RWKV_PAIRS = RWKV_HEADS // 2
IN_PROJ_ROWS = 1024
ATTN_ROWS = 512
VMEM_LIMIT_BYTES = 48 * 1024 * 1024
IN_PROJ_VMEM_LIMIT_BYTES = 58 * 1024 * 1024

SUBLANES = 8
LOG2_E = math.log2(math.e)

F32 = jnp.float32
BF16 = jnp.bfloat16


def _dot(a, b):
    return jnp.dot(a.astype(BF16), b.astype(BF16), preferred_element_type=F32)


def _dot_nt(a, b):
    return lax.dot_general(a.astype(BF16), b.astype(BF16), (((1,), (1,)), ((), ())),
                           preferred_element_type=F32)


def _dot_tn(a, b):
    return lax.dot_general(a.astype(BF16), b.astype(BF16), (((0,), (0,)), ((), ())),
                           preferred_element_type=F32)


def _rms_norm(x, g):
    return x * lax.rsqrt(jnp.mean(x * x, axis=-1, keepdims=True) + RMS_EPS) * g


def _gelu(x):
    return 0.5 * x * (1.0 + lax.erf(x * (2.0 ** -0.5)))


def _silu(x):
    return x * jax.nn.sigmoid(x)


def _sgu_chunk(z, lng_ref, lnb_ref, ws_c, bs_ref, og_ref):
    u = _gelu(z[:, 0:D_SGU])
    v = _gelu(z[:, D_SGU:2 * D_SGU])
    gate = z[:, 2 * D_SGU:3 * D_SGU]
    vc = v - jnp.mean(v, axis=-1, keepdims=True)
    var = jnp.mean(vc * vc, axis=-1, keepdims=True)
    vn = (vc * lax.rsqrt(var + LN_EPS) * lng_ref[...] + lnb_ref[...]).astype(BF16)
    sv = jnp.concatenate(
        [jnp.dot(ws_c[h], vn[:, h * SGU_HEAD_DIM:(h + 1) * SGU_HEAD_DIM],
                 preferred_element_type=F32) for h in range(SGU_HEADS)], axis=1) + bs_ref[...]
    return _rms_norm(u * sv, og_ref[...]) * _silu(gate)


def _in_proj_kernel(tiles_per_batch, x_ref, g_ref, w_ref, lng_ref, lnb_ref, ws_ref, bs_ref, og_ref,
                    mu_ref, ys_ref, zr_ref, zs_ref, prev_ref):
    tile = pl.program_id(0)

    @pl.when(tile == 0)
    def _():
        prev_ref[...] = jnp.zeros_like(prev_ref)

    first_tile = tile % tiles_per_batch == 0
    row8 = lax.broadcasted_iota(jnp.int32, (SUBLANES, 1), 0)
    xn = _rms_norm(x_ref[...], g_ref[...]).astype(BF16)
    zs_ref[...] = jnp.dot(xn, w_ref[:, :C_SGU], preferred_element_type=F32)
    row = lax.broadcasted_iota(jnp.int32, (SGU_CHUNK, SGU_CHUNK), 0)
    col = lax.broadcasted_iota(jnp.int32, (SGU_CHUNK, SGU_CHUNK), 1)
    ws_c = [jnp.where(col <= row, ws_ref[h], 0.0).astype(BF16) for h in range(SGU_HEADS)]
    n_chunks = IN_PROJ_ROWS // SGU_CHUNK
    lane_tiles = C_RWKV // PAIR
    bounds = [C_SGU + PAIR * (lane_tiles * c // n_chunks) for c in range(n_chunks + 1)]
    gate_cols = slice(3 * D_RWKV, 4 * D_RWKV)
    for c in range(n_chunks):
        z = jnp.dot(xn, w_ref[:, bounds[c]:bounds[c + 1]], preferred_element_type=F32)
        cols = slice(bounds[c] - C_SGU, bounds[c + 1] - C_SGU)
        prev = jnp.where(first_tile, 0.0, prev_ref[:, cols])
        rolled = pltpu.roll(z, 1, axis=0)
        z_prev = jnp.concatenate(
            [jnp.where(row8 == 0, prev, rolled[:SUBLANES]), rolled[SUBLANES:]], axis=0)
        prev_ref[:, cols] = z[IN_PROJ_ROWS - 1:IN_PROJ_ROWS, :]
        zs = z + (z_prev - z) * mu_ref[:, cols]
        lo = max(gate_cols.start, cols.start) - cols.start
        hi = min(gate_cols.stop, cols.stop) - cols.start
        if lo < hi:
            pieces = [zs[:, :lo], _silu(zs[:, lo:hi]), zs[:, hi:]]
            zs = jnp.concatenate([p for p in pieces if p.shape[1]], axis=1)
        zr_ref[:, cols] = zs
        rows = slice(c * SGU_CHUNK, (c + 1) * SGU_CHUNK)
        ys_ref[rows, :] = _sgu_chunk(zs_ref[rows, :], lng_ref, lnb_ref, ws_c, bs_ref,
                                     og_ref).astype(ys_ref.dtype)


def _in_proj(x2, g, w_in, ln_g, ln_b, ws, bs_full, out_g, mu, seq):
    m = x2.shape[0]
    tm = IN_PROJ_ROWS
    vec = pl.BlockSpec((1, D_SGU), lambda i: (0, 0))
    return pl.pallas_call(
        functools.partial(_in_proj_kernel, seq // tm),
        grid=(m // tm,),
        in_specs=[
            pl.BlockSpec((tm, D_MODEL), lambda i: (i, 0)),
            pl.BlockSpec((1, D_MODEL), lambda i: (0, 0)),
            pl.BlockSpec((D_MODEL, C_SGU + C_RWKV), lambda i: (0, 0)),
            vec, vec,
            pl.BlockSpec((SGU_HEADS, SGU_CHUNK, SGU_CHUNK), lambda i: (0, 0, 0)),
            pl.BlockSpec((SGU_CHUNK, D_SGU), lambda i: (0, 0)),
            vec,
            pl.BlockSpec((1, C_RWKV), lambda i: (0, 0)),
        ],
        out_specs=[
            pl.BlockSpec((tm, D_SGU), lambda i: (i, 0)),
            pl.BlockSpec((tm, C_RWKV), lambda i: (i, 0)),
        ],
        out_shape=[
            jax.ShapeDtypeStruct((m, D_SGU), BF16),
            jax.ShapeDtypeStruct((m, C_RWKV), F32),
        ],
        scratch_shapes=[
            pltpu.VMEM((tm, C_SGU), F32),
            pltpu.VMEM((1, C_RWKV), F32),
        ],
        compiler_params=pltpu.CompilerParams(
            dimension_semantics=("arbitrary",), vmem_limit_bytes=IN_PROJ_VMEM_LIMIT_BYTES),
        name="in_proj",
    )(x2, g, w_in, ln_g, ln_b, ws, bs_full, out_g, mu)


RWKV_ROWS = RWKV_CHUNK * RWKV_BATCHES
RWKV_MATMUL_STAGES = 16
RWKV_PREPARE_PIECES = 3
RWKV_BUFFERS = (
    ("ar", 2 * RWKV_ROWS, BF16),
    ("bk_h", 2 * RWKV_ROWS, BF16),
    ("b_t", RWKV_ROWS, BF16),
    ("k_t", RWKV_ROWS, BF16),
    ("v", RWKV_ROWS, BF16),
    ("g_col", RWKV_ROWS, F32),
    ("bonus", RWKV_ROWS, F32),
    ("gate", RWKV_ROWS, F32),
)


def _block_diag(x, keep_a, keep_b):
    return jnp.concatenate([x * keep_a, x * keep_b], axis=0)


def _dots_paired(lhs, rhs):
    out = []
    for i in range(0, len(lhs), 2):
        m, n = lhs[i].shape[0], rhs[i].shape[1]
        both = jnp.dot(jnp.concatenate([lhs[i], lhs[i + 1]], axis=0),
                       jnp.concatenate([rhs[i], rhs[i + 1]], axis=1),
                       preferred_element_type=F32)
        out += [both[:m, :n], both[m:, n:]]
    return out


def _dots_paired_tn(lhs, rhs):
    out = []
    for i in range(0, len(lhs), 2):
        m, n = lhs[i].shape[1], rhs[i].shape[1]
        both = _dot_tn(jnp.concatenate([lhs[i], lhs[i + 1]], axis=1),
                       jnp.concatenate([rhs[i], rhs[i + 1]], axis=1))
        out += [both[:m, :n], both[m:, n:]]
    return out


def _head_sum(x, lane_a):
    outs = []
    for p in range(RWKV_PAIRS):
        xp = x[:, p * PAIR:(p + 1) * PAIR]
        sum_a = jnp.sum(jnp.where(lane_a, xp, 0.0), axis=-1, keepdims=True)
        sum_b = jnp.sum(jnp.where(lane_a, 0.0, xp), axis=-1, keepdims=True)
        outs.append(jnp.where(lane_a, sum_a, sum_b))
    return jnp.concatenate(outs, axis=-1)


def _rwkv_prepare(z_ref, par, out):
    C, N = RWKV_CHUNK, RWKV_HEAD_DIM
    w0_ref, w2_ref, a0_ref, a2_ref, kk_ref, ka_ref, rk_ref = par
    lane_c = lax.broadcasted_iota(jnp.int32, (C, PAIR), 1)
    lane_a = lane_c < N
    eye = (lane_c & (N - 1)) == lax.broadcasted_iota(jnp.int32, (C, PAIR), 0)
    eye_full = jnp.concatenate([eye] * RWKV_PAIRS, axis=1)
    row_c = lax.broadcasted_iota(jnp.int32, (C, C), 0)
    col_c = lax.broadcasted_iota(jnp.int32, (C, C), 1)
    tril = (col_c <= row_c).astype(F32).astype(BF16)
    tril3 = jnp.concatenate([tril, tril, tril], axis=1)

    def one_batch(b):
        r = z_ref[b, :, 0:D_RWKV]
        k = z_ref[b, :, D_RWKV:2 * D_RWKV]
        v = z_ref[b, :, 2 * D_RWKV:3 * D_RWKV]
        gate_act = z_ref[b, :, 3 * D_RWKV:4 * D_RWKV]
        wd = z_ref[b, :, 4 * D_RWKV:4 * D_RWKV + LORA]
        ad = z_ref[b, :, 4 * D_RWKV + LORA:4 * D_RWKV + 2 * LORA]

        lw = w0_ref[...] + _dot(jnp.tanh(wd), w2_ref[...])
        ld = -(math.exp(-0.5) * LOG2_E) * jax.nn.sigmoid(lw)
        icl = jax.nn.sigmoid(a0_ref[...] + _dot(ad, a2_ref[...]))
        yield
        k2 = k * ((1.0 - ka_ref[...]) + ka_ref[...] * icl)
        kk = k * kk_ref[...]
        kk = kk * lax.rsqrt(jnp.maximum(_head_sum(kk * kk, lane_a), 1e-24))
        bvec = kk * icl

        p1 = ld.astype(BF16)
        r1 = ld - p1.astype(F32)
        p2 = r1.astype(BF16)
        p3 = (r1 - p2.astype(F32)).astype(BF16)
        cum = jnp.dot(tril3, jnp.concatenate([p1, p2, p3], axis=0), preferred_element_type=F32)
        last = cum[C - 1:C, :]
        yield
        e_neg = jnp.exp2(-cum)
        g_last = jnp.exp2(last)
        a_t = -kk * jnp.exp2(cum - ld)
        r_t = r * jnp.exp2(cum)
        b_t = bvec * e_neg
        k_t = k2 * e_neg

        stacked = {
            "ar": (a_t, r_t),
            "bk_h": (b_t * g_last, k_t * g_last),
        }
        for name, (top, bot) in stacked.items():
            out[name][2 * b * C:(2 * b + 1) * C, :] = top.astype(BF16)
            out[name][(2 * b + 1) * C:(2 * b + 2) * C, :] = bot.astype(BF16)
        rows = slice(b * C, (b + 1) * C)
        out["b_t"][rows, :] = b_t.astype(BF16)
        out["k_t"][rows, :] = k_t.astype(BF16)
        out["v"][rows, :] = v.astype(BF16)
        out["g_col"][rows, :] = _head_sum(jnp.where(eye_full, g_last, 0.0), lane_a)
        out["bonus"][rows, :] = _head_sum(r * k2 * rk_ref[...], lane_a) * v
        out["gate"][rows, :] = gate_act
        yield

    batches = [one_batch(b) for b in range(RWKV_BATCHES)]
    for _ in range(RWKV_PREPARE_PIECES):
        for gen in batches:
            next(gen)
            yield


def _rwkv_chunks(buf, state_ref, first_chunk, gng_ref, gnb_ref, o_ref):
    C, N = RWKV_CHUNK, RWKV_HEAD_DIM
    row = lax.broadcasted_iota(jnp.int32, (C, PAIR), 0)
    lane = lax.broadcasted_iota(jnp.int32, (C, PAIR), 1)
    col = lane & (N - 1)
    lane_a = lane < N
    incl = col <= row
    strict = col < row

    units = [(b, p) for b in range(RWKV_BATCHES) for p in range(RWKV_PAIRS)]
    nu = len(units)
    lanes = lambda u: slice(u[1] * PAIR, (u[1] + 1) * PAIR)
    two = lambda name, u: buf[name][2 * u[0] * C:(2 * u[0] + 2) * C, lanes(u)]
    one = lambda name, u: buf[name][u[0] * C:(u[0] + 1) * C, lanes(u)]
    as_keep = lambda mask: jnp.where(mask, 1.0, 0.0).astype(BF16)
    keep_a, keep_b = as_keep(lane_a), as_keep(jnp.logical_not(lane_a))
    bd = lambda x: _block_diag(x, keep_a, keep_b)

    s_bk = [_dot_nt(two("ar", u), jnp.concatenate([bd(one("b_t", u)), bd(one("k_t", u))], axis=0))
            for u in units]
    yield
    s_b = [s[:, :PAIR] for s in s_bk]
    s_k = [s[:, PAIR:] for s in s_bk]
    n_mat = [jnp.where(strict, s[:C], 0.0) for s in s_b]
    n_bf = [n.astype(BF16) for n in n_mat]
    a_ak = [jnp.where(strict, s[:C], 0.0).astype(BF16) for s in s_k]
    a_rb = [jnp.where(incl, s[C:], 0.0).astype(BF16) for s in s_b]
    a_rk = [jnp.where(incl, s[C:], 0.0).astype(BF16) for s in s_k]
    yield

    lvl = ((row >> 1) == (col >> 1)) & ((row & 1) == 1) & ((col & 1) == 0)
    eye_f = (col == row).astype(F32)
    x = [eye_f + jnp.where(lvl, n, 0.0) for n in n_mat]
    s = 2
    while s < C:
        shift = s.bit_length()
        lvl = ((row >> shift) == (col >> shift)) & ((row & s) != 0) & ((col & s) == 0)
        lvl_a, lvl_b = as_keep(lvl & lane_a), as_keep(lvl & jnp.logical_not(lane_a))
        x_bf = [xi.astype(BF16) for xi in x]
        if s < SUBLANES:
            xn = _dots_paired(x_bf, [_block_diag(n, lvl_a, lvl_b) for n in n_bf])
            yield
            xnx = _dots_paired([t.astype(BF16) for t in xn], [bd(t) for t in x_bf])
            x = [x[i] + xnx[i] for i in range(nu)]
            yield
        else:
            groups = [g for g in range(C // SUBLANES) if (g * SUBLANES) & s]
            rows_of = lambda m, g: m[g * SUBLANES:(g + 1) * SUBLANES]
            moving = [jnp.concatenate([rows_of(xi, g) for g in groups], axis=0).astype(BF16)
                      for xi in x]
            xn = _dots_paired(moving, [_block_diag(n, lvl_a, lvl_b) for n in n_bf])
            yield
            xnx = _dots_paired([t.astype(BF16) for t in xn], [bd(t) for t in x_bf])
            x = [jnp.concatenate(
                [rows_of(x[i], g) + rows_of(xnx[i], groups.index(g)) if g in groups
                 else rows_of(x[i], g) for g in range(C // SUBLANES)], axis=0)
                 for i in range(nu)]
            yield
        s *= 2

    x_bf = [xi.astype(BF16) for xi in x]
    v_bd = [bd(one("v", u)) for u in units]
    av = [t.astype(BF16) for t in _dots_paired(a_ak, v_bd)]
    yield
    p_mat = [t.astype(BF16) for t in _dots_paired(x_bf, [bd(two("ar", u)[:C]) for u in units])]
    yield

    t_old = [jnp.where(first_chunk, 0.0, state_ref[u[0], :, lanes(u)]) for u in units]
    t_bd = [bd(t.astype(BF16)) for t in t_old]
    u_mat = _dots_paired(
        [jnp.concatenate([p_mat[i], x_bf[i]], axis=1) for i in range(nu)],
        [jnp.concatenate([t_bd[i], bd(av[i])], axis=0) for i in range(nu)])
    u_bf = [um.astype(BF16) for um in u_mat]
    yield
    y = _dots_paired(
        [jnp.concatenate([two("ar", u)[C:], a_rb[i], a_rk[i]], axis=1)
         for i, u in enumerate(units)],
        [jnp.concatenate([t_bd[i], bd(u_bf[i]), v_bd[i]], axis=0) for i in range(nu)])
    yield
    full = _dots_paired_tn(
        [two("bk_h", u) for u in units],
        [jnp.concatenate([u_bf[i], one("v", u)], axis=0) for i, u in enumerate(units)])
    for i, u in enumerate(units):
        state_ref[u[0], :, lanes(u)] = (one("g_col", u) * t_old[i]
                                        + jnp.where(lane_a, full[i][:N], full[i][N:]))
    yield

    for b in range(RWKV_BATCHES):
        rows = slice(b * C, (b + 1) * C)
        yb = jnp.concatenate(y[b * RWKV_PAIRS:(b + 1) * RWKV_PAIRS], axis=1)
        yc = yb - _head_sum(yb, lane_a) * (1.0 / N)
        var = _head_sum(yc * yc, lane_a) * (1.0 / N)
        yn = yc * lax.rsqrt(var + GN_EPS) * gng_ref[...] + gnb_ref[...]
        o_ref[b] = ((yn + buf["bonus"][rows, :]) * buf["gate"][rows, :]).astype(o_ref.dtype)
        yield


def _rwkv_kernel(chunks, z_ref, w0_ref, w2_ref, a0_ref, a2_ref, kk_ref, ka_ref, rk_ref,
                 gng_ref, gnb_ref, o_ref, state_ref, *bufs):
    names = [name for name, _, _ in RWKV_BUFFERS]
    sets = (dict(zip(names, bufs[:len(names)])), dict(zip(names, bufs[len(names):])))
    par = (w0_ref, w2_ref, a0_ref, a2_ref, kk_ref, ka_ref, rk_ref)
    step = pl.program_id(0)

    @pl.when(step == 0)
    def _():
        for ref in sets[1].values():
            ref[...] = jnp.zeros_like(ref)
        state_ref[...] = jnp.zeros_like(state_ref)

    first_consumed = step % chunks == 1

    def run(nxt, cur):
        prepare = _rwkv_prepare(z_ref, par, nxt)
        pieces = RWKV_PREPARE_PIECES * RWKV_BATCHES
        for stage, _ in enumerate(
                _rwkv_chunks(cur, state_ref, first_consumed, gng_ref, gnb_ref, o_ref)):
            due = min(stage + 1, RWKV_MATMUL_STAGES) * pieces // RWKV_MATMUL_STAGES
            done = min(stage, RWKV_MATMUL_STAGES) * pieces // RWKV_MATMUL_STAGES
            for _ in range(due - done):
                next(prepare)

    @pl.when(step % 2 == 0)
    def _():
        run(sets[0], sets[1])

    @pl.when(step % 2 == 1)
    def _():
        run(sets[1], sets[0])


def _rwkv(z_rwkv, w0, w2, a0, a2, k_k, k_a, r_k, gn_g, gn_b, batch, seq):
    C, NB = RWKV_CHUNK, RWKV_BATCHES
    chunks = seq // C
    n_blocks = (batch // NB) * chunks
    z3 = z_rwkv.reshape(batch, seq, C_RWKV)
    vec = lambda n: pl.BlockSpec((1, n), lambda s: (0, 0))
    lora = pl.BlockSpec((LORA, D_RWKV), lambda s: (0, 0))

    def block_of(s):
        return (s // chunks, s % chunks, 0)

    out = pl.pallas_call(
        functools.partial(_rwkv_kernel, chunks),
        grid=(n_blocks + 1,),
        in_specs=[
            pl.BlockSpec((NB, C, C_RWKV), lambda s: block_of(jnp.minimum(s, n_blocks - 1))),
            vec(D_RWKV), lora, vec(D_RWKV), lora,
            vec(D_RWKV), vec(D_RWKV), vec(D_RWKV), vec(D_RWKV), vec(D_RWKV),
        ],
        out_specs=pl.BlockSpec((NB, C, D_RWKV), lambda s: block_of(jnp.maximum(s - 1, 0))),
        out_shape=jax.ShapeDtypeStruct((batch, seq, D_RWKV), BF16),
        scratch_shapes=[
            pltpu.VMEM((NB, RWKV_HEAD_DIM, D_RWKV), F32),
        ] + 2 * [pltpu.VMEM((n, D_RWKV), dt) for _, n, dt in RWKV_BUFFERS],
        compiler_params=pltpu.CompilerParams(
            dimension_semantics=("arbitrary",), vmem_limit_bytes=VMEM_LIMIT_BYTES),
        name="rwkv",
    )(z3, w0, w2, a0, a2, k_k, k_a, r_k, gn_g, gn_b)
    return out.reshape(batch * seq, D_RWKV)


def _out_attn_front(x_ref, ys_ref, yr_ref, wout_ref, gx_ref, wq_ref, h1_ref, q_ref):
    y = jnp.concatenate([ys_ref[...], yr_ref[...]], axis=1)
    h1 = x_ref[...] + jnp.dot(y, wout_ref[...], preferred_element_type=F32)
    h1_ref[...] = h1
    yield
    hn = _rms_norm(h1, gx_ref[...]).astype(BF16)
    yield
    q_ref[...] = jnp.dot(hn, wq_ref[...], preferred_element_type=F32).astype(BF16)
    yield


def _out_attn_back(h1_ref, q_ref, k_ref, v_ref, wo_ref, gf_ref, o_ref):
    cols = [slice(h * XATTN_HEAD_DIM, (h + 1) * XATTN_HEAD_DIM) for h in range(XATTN_HEADS)]
    scores = [_dot_nt(q_ref[:, sl], k_ref[:, sl]) for sl in cols]
    yield
    probs = []
    for s in scores:
        e = jnp.exp(s - jnp.max(s, axis=-1, keepdims=True))
        probs.append((e / jnp.sum(e, axis=-1, keepdims=True)).astype(BF16))
    yield
    o = jnp.concatenate(
        [jnp.dot(p, v_ref[:, sl], preferred_element_type=F32).astype(BF16)
         for p, sl in zip(probs, cols)], axis=-1)
    yield
    h2 = h1_ref[...] + jnp.dot(o, wo_ref[...], preferred_element_type=F32)
    yield
    o_ref[...] = _rms_norm(h2, gf_ref[...])
    yield


OUT_ATTN_ORDER = "BFBBFBBF"


def _out_attn_kernel(tiles_per_batch, n_tiles, x_ref, ys_ref, yr_ref, wout_ref, gx_ref, wq_ref,
                     mem_ref, gm_ref, wkv_ref, wo_ref, gf_ref, o_ref,
                     h1_a, q_a, h1_b, q_b, k_scr, v_scr):
    step = pl.program_id(0)

    @pl.when(step == 0)
    def _():
        h1_b[...] = jnp.zeros_like(h1_b)
        q_b[...] = jnp.zeros_like(q_b)

    @pl.when((step % tiles_per_batch == 0) & (step < n_tiles))
    def _():
        slot = (step // tiles_per_batch) % 2
        mn = _rms_norm(mem_ref[...], gm_ref[...]).astype(BF16)
        k = jnp.dot(mn, wkv_ref[:, :D_MODEL], preferred_element_type=F32)
        k_scr[slot] = (k * (XATTN_HEAD_DIM ** -0.5)).astype(BF16)
        v_scr[slot] = jnp.dot(mn, wkv_ref[:, D_MODEL:], preferred_element_type=F32).astype(BF16)

    back_slot = (jnp.maximum(step - 1, 0) // tiles_per_batch) % 2
    k_ref, v_ref = k_scr.at[back_slot], v_scr.at[back_slot]

    def run(nxt, cur):
        front = _out_attn_front(x_ref, ys_ref, yr_ref, wout_ref, gx_ref, wq_ref, *nxt)
        back = _out_attn_back(*cur, k_ref, v_ref, wo_ref, gf_ref, o_ref)
        for stage in OUT_ATTN_ORDER:
            next(front if stage == "F" else back)

    @pl.when(step % 2 == 0)
    def _():
        run((h1_a, q_a), (h1_b, q_b))

    @pl.when(step % 2 == 1)
    def _():
        run((h1_b, q_b), (h1_a, q_a))


def _out_attn(x2, y_sgu, y_rwkv, w_out, g_x, w_q, mem2, g_mem, w_kv, w_o, g_f, seq):
    m = x2.shape[0]
    tq = ATTN_ROWS
    per_batch = seq // tq
    n_tiles = m // tq
    front_tile = lambda s: jnp.minimum(s, n_tiles - 1)
    front_blk = lambda n: pl.BlockSpec((tq, n), lambda s: (front_tile(s), 0))
    full = lambda a, b: pl.BlockSpec((a, b), lambda s: (0, 0))
    mem_blk = pl.BlockSpec((MEM_LEN, D_MODEL), lambda s: (front_tile(s) // per_batch, 0))
    return pl.pallas_call(
        functools.partial(_out_attn_kernel, per_batch, n_tiles),
        grid=(n_tiles + 1,),
        in_specs=[
            front_blk(D_MODEL), front_blk(D_SGU), front_blk(D_RWKV),
            full(D_SGU + D_RWKV, D_MODEL), full(1, D_MODEL), full(D_MODEL, D_MODEL),
            mem_blk, full(1, D_MODEL), full(D_MODEL, 2 * D_MODEL),
            full(D_MODEL, D_MODEL), full(1, D_MODEL),
        ],
        out_specs=pl.BlockSpec((tq, D_MODEL), lambda s: (jnp.maximum(s - 1, 0), 0)),
        out_shape=jax.ShapeDtypeStruct((m, D_MODEL), F32),
        scratch_shapes=2 * [pltpu.VMEM((tq, D_MODEL), F32), pltpu.VMEM((tq, D_MODEL), BF16)]
        + 2 * [pltpu.VMEM((2, MEM_LEN, D_MODEL), BF16)],
        compiler_params=pltpu.CompilerParams(
            dimension_semantics=("arbitrary",), vmem_limit_bytes=VMEM_LIMIT_BYTES),
        name="out_attn",
    )(x2, y_sgu, y_rwkv, w_out, g_x, w_q, mem2, g_mem, w_kv, w_o, g_f)


def kernel(x, mem, ln_mix_g, w_in, sgu_ln_g, sgu_ln_b, sgu_ws, sgu_bs, sgu_out_g, rw_mu, rw_w0, rw_w2, rw_a0, rw_a2, rw_k_k, rw_k_a, rw_r_k, rw_gn_g, rw_gn_b, w_out, ln_x_g, ln_mem_g, w_q, w_kv, w_o, ln_f_g):
    batch, seq, _ = x.shape
    assert w_in.shape[0] == 1, "the final norm is fused into the only layer's last call"
    row = lambda a: a.reshape(1, -1)
    h = x.reshape(batch * seq, D_MODEL)
    mem2 = mem.reshape(batch * MEM_LEN, D_MODEL)
    for l in range(1):
        bs_full = jnp.repeat(sgu_bs[l].T, SGU_HEAD_DIM, axis=1)
        y_sgu, z_rwkv = _in_proj(h, row(ln_mix_g[l]), w_in[l].astype(BF16), row(sgu_ln_g[l]),
                                 row(sgu_ln_b[l]), sgu_ws[l], bs_full, row(sgu_out_g[l]),
                                 row(rw_mu[l]), seq)
        y_rwkv = _rwkv(z_rwkv, row(rw_w0[l]), rw_w2[l].astype(BF16),
                       row(rw_a0[l]), rw_a2[l].astype(BF16), row(rw_k_k[l]), row(rw_k_a[l]),
                       row(rw_r_k[l]), row(rw_gn_g[l]), row(rw_gn_b[l]), batch, seq)
        h = _out_attn(h, y_sgu, y_rwkv, w_out[l].astype(BF16), row(ln_x_g[l]),
                      w_q[l].astype(BF16), mem2, row(ln_mem_g[l]), w_kv[l].astype(BF16),
                      w_o[l].astype(BF16), row(ln_f_g), seq)
    return h.reshape(batch, seq, D_MODEL)
```

```python
import functools
import math

import jax
import jax.numpy as jnp
from jax import lax
from jax.experimental import pallas as pl
from jax.experimental.pallas import tpu as pltpu

D_MODEL = 1024
MEM_LEN = 256
D_SGU = 512
SGU_HEADS = 4
SGU_HEAD_DIM = D_SGU // SGU_HEADS
SGU_CHUNK = 128
D_RWKV = 512
RWKV_HEAD_DIM = 64
RWKV_HEADS = D_RWKV // RWKV_HEAD_DIM
LORA = 64
C_SGU = 3 * D_SGU
C_RWKV = 4 * D_RWKV + 2 * LORA
XATTN_HEADS = 4
XATTN_HEAD_DIM = D_MODEL // XATTN_HEADS
RMS_EPS = 1e-6
LN_EPS = 1e-5
GN_EPS = 64e-5
KK_NORM_FLOOR = 1e-24

RWKV_CHUNK = 64
RWKV_BATCHES = 8
PAIR = 2 * RWKV_HEAD_DIM
RWKV_PAIRS = RWKV_HEADS // 2
IN_PROJ_ROWS = 512
ATTN_ROWS = 512
VMEM_LIMIT_BYTES = 48 * 1024 * 1024

SUBLANES = 8
LOG2_E = math.log2(math.e)

F32 = jnp.float32
BF16 = jnp.bfloat16


def _dot(a, b):
    return jnp.dot(a.astype(BF16), b.astype(BF16), preferred_element_type=F32)


def _dot_nt(a, b):
    return lax.dot_general(a.astype(BF16), b.astype(BF16), (((1,), (1,)), ((), ())),
                           preferred_element_type=F32)


def _dot_tn(a, b):
    return lax.dot_general(a.astype(BF16), b.astype(BF16), (((0,), (0,)), ((), ())),
                           preferred_element_type=F32)


def _rms_norm(x, g):
    return x * lax.rsqrt(jnp.mean(x * x, axis=-1, keepdims=True) + RMS_EPS) * g


def _gelu(x):
    return 0.5 * x * (1.0 + lax.erf(x * (2.0 ** -0.5)))


def _silu(x):
    return x * jax.nn.sigmoid(x)


def _head_sum(x, lane_a):
    outs = []
    for p in range(RWKV_PAIRS):
        xp = x[:, p * PAIR:(p + 1) * PAIR]
        sum_a = jnp.sum(jnp.where(lane_a, xp, 0.0), axis=-1, keepdims=True)
        sum_b = jnp.sum(jnp.where(lane_a, 0.0, xp), axis=-1, keepdims=True)
        outs.append(jnp.where(lane_a, sum_a, sum_b))
    return jnp.concatenate(outs, axis=-1)


RWKV_OPERANDS = ("a_t", "r_t", "b_t", "k_t", "b_h", "k_h", "v")
RWKV_EPILOGUE = ("bonus", "gate")
RWKV_PREPARE_PIECES = 3


def _sgu_chunk(z, lng_ref, lnb_ref, ws_c, bs_ref, og_ref):
    u = _gelu(z[:, 0:D_SGU])
    v = _gelu(z[:, D_SGU:2 * D_SGU])
    gate = z[:, 2 * D_SGU:3 * D_SGU]
    vc = v - jnp.mean(v, axis=-1, keepdims=True)
    var = jnp.mean(vc * vc, axis=-1, keepdims=True)
    vn = (vc * lax.rsqrt(var + LN_EPS) * lng_ref[...] + lnb_ref[...]).astype(BF16)
    sv = jnp.concatenate(
        [jnp.dot(ws_c[h], vn[:, h * SGU_HEAD_DIM:(h + 1) * SGU_HEAD_DIM],
                 preferred_element_type=F32) for h in range(SGU_HEADS)], axis=1) + bs_ref[...]
    return _rms_norm(u * sv, og_ref[...]) * _silu(gate)


def _rwkv_prepare(zr_ref, par, out, glast_ref):
    C, N = RWKV_CHUNK, RWKV_HEAD_DIM
    w0_ref, w2_ref, a0_ref, a2_ref, kk_ref, ka_ref, rk_ref = par
    lane_a = lax.broadcasted_iota(jnp.int32, (C, PAIR), 1) < N
    row_c = lax.broadcasted_iota(jnp.int32, (C, C), 0)
    col_c = lax.broadcasted_iota(jnp.int32, (C, C), 1)
    tril = (col_c <= row_c).astype(F32).astype(BF16)
    tril3 = jnp.concatenate([tril, tril, tril], axis=1)

    def one_chunk(c):
        rows = slice(c * C, (c + 1) * C)
        r = zr_ref[rows, 0:D_RWKV]
        k = zr_ref[rows, D_RWKV:2 * D_RWKV]
        v = zr_ref[rows, 2 * D_RWKV:3 * D_RWKV]
        gate_act = zr_ref[rows, 3 * D_RWKV:4 * D_RWKV]
        wd = zr_ref[rows, 4 * D_RWKV:4 * D_RWKV + LORA]
        ad = zr_ref[rows, 4 * D_RWKV + LORA:4 * D_RWKV + 2 * LORA]

        lw = w0_ref[...] + _dot(jnp.tanh(wd), w2_ref[...])
        ld = -(math.exp(-0.5) * LOG2_E) * jax.nn.sigmoid(lw)
        icl = jax.nn.sigmoid(a0_ref[...] + _dot(ad, a2_ref[...]))
        yield
        k2 = k * ((1.0 - ka_ref[...]) + ka_ref[...] * icl)
        kk = k * kk_ref[...]
        kk = kk * lax.rsqrt(jnp.maximum(_head_sum(kk * kk, lane_a), KK_NORM_FLOOR))
        bvec = kk * icl

        p1 = ld.astype(BF16)
        r1 = ld - p1.astype(F32)
        p2 = r1.astype(BF16)
        p3 = (r1 - p2.astype(F32)).astype(BF16)
        cum = jnp.dot(tril3, jnp.concatenate([p1, p2, p3], axis=0), preferred_element_type=F32)
        last = cum[C - 1:C, :]
        yield
        e_neg = jnp.exp2(-cum)
        g_last = jnp.exp2(last)
        b_t = bvec * e_neg
        k_t = k2 * e_neg
        out["a_t"][rows, :] = (-kk * jnp.exp2(cum - ld)).astype(BF16)
        out["r_t"][rows, :] = (r * jnp.exp2(cum)).astype(BF16)
        out["b_t"][rows, :] = b_t.astype(BF16)
        out["k_t"][rows, :] = k_t.astype(BF16)
        out["b_h"][rows, :] = (b_t * g_last).astype(BF16)
        out["k_h"][rows, :] = (k_t * g_last).astype(BF16)
        out["v"][rows, :] = v.astype(BF16)
        out["bonus"][rows, :] = _head_sum(r * k2 * rk_ref[...], lane_a) * v
        out["gate"][rows, :] = gate_act
        glast_ref[c] = g_last
        yield

    chunks = [one_chunk(c) for c in range(IN_PROJ_ROWS // C)]
    for _ in range(RWKV_PREPARE_PIECES):
        for gen in chunks:
            next(gen)
            yield


def _in_proj_kernel(tiles_per_batch, n_tiles, x_ref, g_ref, w_ref, lng_ref, lnb_ref, ws_ref,
                    bs_ref, og_ref, mu_ref, w0_ref, w2_ref, a0_ref, a2_ref, kk_ref, ka_ref, rk_ref,
                    ys_ref, at_ref, rt_ref, bt_ref, kt_ref, bh_ref, kh_ref, v_ref, bonus_ref,
                    gate_ref, glast_ref, zs_ref, zr_ref, prev_ref):
    tile = pl.program_id(0)
    par = (w0_ref, w2_ref, a0_ref, a2_ref, kk_ref, ka_ref, rk_ref)
    out = dict(zip(RWKV_OPERANDS + RWKV_EPILOGUE,
                   (at_ref, rt_ref, bt_ref, kt_ref, bh_ref, kh_ref, v_ref, bonus_ref, gate_ref)))

    @pl.when(tile == 0)
    def _():
        prev_ref[...] = jnp.zeros_like(prev_ref)
        zr_ref[...] = jnp.zeros_like(zr_ref)

    @pl.when(tile < n_tiles)
    def _():
        prepare = _rwkv_prepare(zr_ref, par, out, glast_ref)
        first_tile = tile % tiles_per_batch == 0
        row8 = lax.broadcasted_iota(jnp.int32, (SUBLANES, 1), 0)
        xn = _rms_norm(x_ref[...], g_ref[...]).astype(BF16)
        per_dot = RWKV_PREPARE_PIECES * (IN_PROJ_ROWS // RWKV_CHUNK) // 3
        for j in range(3):
            for _ in range(per_dot):
                next(prepare)
            cols = slice(j * D_SGU, (j + 1) * D_SGU)
            zs_ref[:, cols] = jnp.dot(xn, w_ref[:, cols], preferred_element_type=F32)
        for _ in prepare:
            pass
        row = lax.broadcasted_iota(jnp.int32, (SGU_CHUNK, SGU_CHUNK), 0)
        col = lax.broadcasted_iota(jnp.int32, (SGU_CHUNK, SGU_CHUNK), 1)
        ws_c = [jnp.where(col <= row, ws_ref[h], 0.0).astype(BF16) for h in range(SGU_HEADS)]
        n_chunks = IN_PROJ_ROWS // SGU_CHUNK
        lane_tiles = C_RWKV // PAIR
        bounds = [C_SGU + PAIR * (lane_tiles * c // n_chunks) for c in range(n_chunks + 1)]
        gate_cols = slice(3 * D_RWKV, 4 * D_RWKV)
        for c in range(n_chunks):
            z = jnp.dot(xn, w_ref[:, bounds[c]:bounds[c + 1]], preferred_element_type=F32)
            cols = slice(bounds[c] - C_SGU, bounds[c + 1] - C_SGU)
            prev = jnp.where(first_tile, 0.0, prev_ref[:, cols])
            rolled = pltpu.roll(z, 1, axis=0)
            z_prev = jnp.concatenate(
                [jnp.where(row8 == 0, prev, rolled[:SUBLANES]), rolled[SUBLANES:]], axis=0)
            prev_ref[:, cols] = z[IN_PROJ_ROWS - 1:IN_PROJ_ROWS, :]
            zs = z + (z_prev - z) * mu_ref[:, cols]
            if cols.start <= gate_cols.start < cols.stop:
                assert gate_cols.stop <= cols.stop
                lo, hi = gate_cols.start - cols.start, gate_cols.stop - cols.start
                pieces = [zs[:, :lo], _silu(zs[:, lo:hi]), zs[:, hi:]]
                zs = jnp.concatenate([p for p in pieces if p.shape[1]], axis=1)
            zr_ref[:, cols] = zs
            rows = slice(c * SGU_CHUNK, (c + 1) * SGU_CHUNK)
            ys_ref[rows, :] = _sgu_chunk(zs_ref[rows, :], lng_ref, lnb_ref, ws_c, bs_ref,
                                         og_ref).astype(ys_ref.dtype)

    @pl.when(tile == n_tiles)
    def _():
        for _ in _rwkv_prepare(zr_ref, par, out, glast_ref):
            pass


def _in_proj(x2, g, w_in, sgu_par, mu, rwkv_par, seq):
    m = x2.shape[0]
    tm = IN_PROJ_ROWS
    n_tiles = m // tm
    chunks = tm // RWKV_CHUNK
    assert (RWKV_PREPARE_PIECES * chunks) % 3 == 0
    const = lambda a: pl.BlockSpec(a.shape, lambda i: (0,) * a.ndim)
    cur = lambda i: jnp.minimum(i, n_tiles - 1)
    last = lambda i: jnp.maximum(i - 1, 0)
    tokens = lambda dt: jax.ShapeDtypeStruct((m, D_RWKV), dt)
    outs = pl.pallas_call(
        functools.partial(_in_proj_kernel, seq // tm, n_tiles),
        grid=(n_tiles + 1,),
        in_specs=[pl.BlockSpec((tm, D_MODEL), lambda i: (cur(i), 0)), const(g), const(w_in)]
        + [const(a) for a in sgu_par] + [const(mu)] + [const(a) for a in rwkv_par],
        out_specs=[pl.BlockSpec((tm, D_SGU), lambda i: (cur(i), 0))]
        + [pl.BlockSpec((tm, D_RWKV), lambda i: (last(i), 0))
           for _ in RWKV_OPERANDS + RWKV_EPILOGUE]
        + [pl.BlockSpec((chunks, 1, D_RWKV), lambda i: (last(i), 0, 0))],
        out_shape=[jax.ShapeDtypeStruct((m, D_SGU), BF16)]
        + [tokens(BF16) for _ in RWKV_OPERANDS] + [tokens(F32) for _ in RWKV_EPILOGUE]
        + [jax.ShapeDtypeStruct((m // RWKV_CHUNK, 1, D_RWKV), F32)],
        scratch_shapes=[
            pltpu.VMEM((tm, C_SGU), F32),
            pltpu.VMEM((tm, C_RWKV), F32),
            pltpu.VMEM((1, C_RWKV), F32),
        ],
        compiler_params=pltpu.CompilerParams(
            dimension_semantics=("arbitrary",), vmem_limit_bytes=VMEM_LIMIT_BYTES),
        name="in_proj",
    )(x2, g, w_in, *sgu_par, mu, *rwkv_par)
    y_sgu, prepared, g_last = outs[0], outs[1:-1], outs[-1]
    return y_sgu, dict(zip(RWKV_OPERANDS + RWKV_EPILOGUE, prepared)), g_last


def _block_diag(x, keep_a, keep_b):
    return jnp.concatenate([x * keep_a, x * keep_b], axis=0)


def _dots_paired(lhs, rhs):
    out = []
    for i in range(0, len(lhs), 2):
        m, n = lhs[i].shape[0], rhs[i].shape[1]
        both = jnp.dot(jnp.concatenate([lhs[i], lhs[i + 1]], axis=0),
                       jnp.concatenate([rhs[i], rhs[i + 1]], axis=1),
                       preferred_element_type=F32)
        out += [both[:m, :n], both[m:, n:]]
    return out


def _dots_paired_tn(lhs, rhs):
    out = []
    for i in range(0, len(lhs), 2):
        m, n = lhs[i].shape[1], rhs[i].shape[1]
        both = _dot_tn(jnp.concatenate([lhs[i], lhs[i + 1]], axis=1),
                       jnp.concatenate([rhs[i], rhs[i + 1]], axis=1))
        out += [both[:m, :n], both[m:, n:]]
    return out


def _rwkv_kernel(at_ref, rt_ref, bt_ref, kt_ref, bh_ref, kh_ref, v_ref, bonus_ref, gate_ref,
                 glast_ref, gng_ref, gnb_ref, o_ref, state_ref):
    C, N = RWKV_CHUNK, RWKV_HEAD_DIM

    @pl.when(pl.program_id(1) == 0)
    def _():
        state_ref[...] = jnp.zeros_like(state_ref)

    row = lax.broadcasted_iota(jnp.int32, (C, PAIR), 0)
    lane = lax.broadcasted_iota(jnp.int32, (C, PAIR), 1)
    col = lane & (N - 1)
    lane_a = lane < N
    incl = col <= row
    strict = col < row
    eye = col == row

    units = [(b, p) for b in range(RWKV_BATCHES) for p in range(RWKV_PAIRS)]
    nu = len(units)
    lanes = lambda u: slice(u[1] * PAIR, (u[1] + 1) * PAIR)
    of = lambda ref, u: ref[u[0], :, lanes(u)]
    as_keep = lambda mask: jnp.where(mask, 1.0, 0.0).astype(BF16)
    keep_a, keep_b = as_keep(lane_a), as_keep(jnp.logical_not(lane_a))
    bd = lambda x: _block_diag(x, keep_a, keep_b)

    s_bk = [_dot_nt(jnp.concatenate([of(at_ref, u), of(rt_ref, u)], axis=0),
                    jnp.concatenate([bd(of(bt_ref, u)), bd(of(kt_ref, u))], axis=0))
            for u in units]
    s_b = [s[:, :PAIR] for s in s_bk]
    s_k = [s[:, PAIR:] for s in s_bk]
    n_mat = [jnp.where(strict, s[:C], 0.0) for s in s_b]
    n_bf = [n.astype(BF16) for n in n_mat]
    a_ak = [jnp.where(strict, s[:C], 0.0).astype(BF16) for s in s_k]
    a_rb = [jnp.where(incl, s[C:], 0.0).astype(BF16) for s in s_b]
    a_rk = [jnp.where(incl, s[C:], 0.0).astype(BF16) for s in s_k]

    lvl = ((row >> 1) == (col >> 1)) & ((row & 1) == 1) & ((col & 1) == 0)
    x = [eye.astype(F32) + jnp.where(lvl, n, 0.0) for n in n_mat]
    s = 2
    while s < C:
        shift = s.bit_length()
        lvl = ((row >> shift) == (col >> shift)) & ((row & s) != 0) & ((col & s) == 0)
        lvl_a, lvl_b = as_keep(lvl & lane_a), as_keep(lvl & jnp.logical_not(lane_a))
        x_bf = [xi.astype(BF16) for xi in x]
        if s < SUBLANES:
            xn = _dots_paired(x_bf, [_block_diag(n, lvl_a, lvl_b) for n in n_bf])
            xnx = _dots_paired([t.astype(BF16) for t in xn], [bd(t) for t in x_bf])
            x = [x[i] + xnx[i] for i in range(nu)]
        else:
            groups = [g for g in range(C // SUBLANES) if (g * SUBLANES) & s]
            rows_of = lambda m, g: m[g * SUBLANES:(g + 1) * SUBLANES]
            moving = [jnp.concatenate([rows_of(xi, g) for g in groups], axis=0).astype(BF16)
                      for xi in x]
            xn = _dots_paired(moving, [_block_diag(n, lvl_a, lvl_b) for n in n_bf])
            xnx = _dots_paired([t.astype(BF16) for t in xn], [bd(t) for t in x_bf])
            x = [jnp.concatenate(
                [rows_of(x[i], g) + rows_of(xnx[i], groups.index(g)) if g in groups
                 else rows_of(x[i], g) for g in range(C // SUBLANES)], axis=0)
                 for i in range(nu)]
        s *= 2

    x_bf = [xi.astype(BF16) for xi in x]
    v_bd = [bd(of(v_ref, u)) for u in units]
    av = [t.astype(BF16) for t in _dots_paired(a_ak, v_bd)]
    p_mat = [t.astype(BF16) for t in _dots_paired(x_bf, [bd(of(at_ref, u)) for u in units])]

    t_old = [of(state_ref, u) for u in units]
    t_bd = [bd(t.astype(BF16)) for t in t_old]
    u_mat = _dots_paired(
        [jnp.concatenate([p_mat[i], x_bf[i]], axis=1) for i in range(nu)],
        [jnp.concatenate([t_bd[i], bd(av[i])], axis=0) for i in range(nu)])
    u_bf = [um.astype(BF16) for um in u_mat]
    y = _dots_paired(
        [jnp.concatenate([of(rt_ref, u), a_rb[i], a_rk[i]], axis=1) for i, u in enumerate(units)],
        [jnp.concatenate([t_bd[i], bd(u_bf[i]), v_bd[i]], axis=0) for i in range(nu)])
    full = _dots_paired_tn(
        [jnp.concatenate([of(bh_ref, u), of(kh_ref, u)], axis=0) for u in units],
        [jnp.concatenate([u_bf[i], of(v_ref, u)], axis=0) for i, u in enumerate(units)])
    eye_full = jnp.concatenate([eye] * RWKV_PAIRS, axis=1)
    g_col = [_head_sum(jnp.where(eye_full, glast_ref[b, 0], 0.0), lane_a)
             for b in range(RWKV_BATCHES)]
    for i, u in enumerate(units):
        state_ref[u[0], :, lanes(u)] = (g_col[u[0]][:, lanes(u)] * t_old[i]
                                        + jnp.where(lane_a, full[i][:N], full[i][N:]))

    for b in range(RWKV_BATCHES):
        yb = jnp.concatenate(y[b * RWKV_PAIRS:(b + 1) * RWKV_PAIRS], axis=1)
        yc = yb - _head_sum(yb, lane_a) * (1.0 / N)
        var = _head_sum(yc * yc, lane_a) * (1.0 / N)
        yn = yc * lax.rsqrt(var + GN_EPS) * gng_ref[...] + gnb_ref[...]
        o_ref[b] = ((yn + bonus_ref[b]) * gate_ref[b]).astype(o_ref.dtype)


def _rwkv(prepared, g_last, gn_g, gn_b, batch, seq):
    C, NB = RWKV_CHUNK, RWKV_BATCHES
    blk = pl.BlockSpec((NB, C, D_RWKV), lambda g, c: (g, c, 0))
    vec = pl.BlockSpec((1, D_RWKV), lambda g, c: (0, 0))
    arrays = [prepared[name].reshape(batch, seq, D_RWKV) for name in RWKV_OPERANDS + RWKV_EPILOGUE]
    out = pl.pallas_call(
        _rwkv_kernel,
        grid=(batch // NB, seq // C),
        in_specs=[blk for _ in arrays]
        + [pl.BlockSpec((NB, 1, 1, D_RWKV), lambda g, c: (g, c, 0, 0)), vec, vec],
        out_specs=blk,
        out_shape=jax.ShapeDtypeStruct((batch, seq, D_RWKV), BF16),
        scratch_shapes=[
            pltpu.VMEM((NB, RWKV_HEAD_DIM, D_RWKV), F32),
        ],
        compiler_params=pltpu.CompilerParams(
            dimension_semantics=("arbitrary", "arbitrary"), vmem_limit_bytes=VMEM_LIMIT_BYTES),
        name="rwkv",
    )(*arrays, g_last.reshape(batch, seq // C, 1, D_RWKV), gn_g, gn_b)
    return out.reshape(batch * seq, D_RWKV)


def _out_attn_front(x_ref, ys_ref, yr_ref, wout_ref, gx_ref, wq_ref, h1_ref, q_ref):
    y = jnp.concatenate([ys_ref[...], yr_ref[...]], axis=1)
    h1 = x_ref[...] + jnp.dot(y, wout_ref[...], preferred_element_type=F32)
    h1_ref[...] = h1
    yield
    hn = _rms_norm(h1, gx_ref[...]).astype(BF16)
    yield
    q_ref[...] = jnp.dot(hn, wq_ref[...], preferred_element_type=F32).astype(BF16)
    yield


def _out_attn_back(h1_ref, q_ref, k_ref, v_ref, wo_ref, gf_ref, o_ref):
    cols = [slice(h * XATTN_HEAD_DIM, (h + 1) * XATTN_HEAD_DIM) for h in range(XATTN_HEADS)]
    scores = [_dot_nt(q_ref[:, sl], k_ref[:, sl]) for sl in cols]
    yield
    probs = []
    for s in scores:
        e = jnp.exp(s - jnp.max(s, axis=-1, keepdims=True))
        probs.append((e / jnp.sum(e, axis=-1, keepdims=True)).astype(BF16))
    yield
    o = jnp.concatenate(
        [jnp.dot(p, v_ref[:, sl], preferred_element_type=F32).astype(BF16)
         for p, sl in zip(probs, cols)], axis=-1)
    yield
    h2 = h1_ref[...] + jnp.dot(o, wo_ref[...], preferred_element_type=F32)
    yield
    o_ref[...] = _rms_norm(h2, gf_ref[...])
    yield


OUT_ATTN_ORDER = "BFBBFBBF"


def _out_attn_kernel(tiles_per_batch, n_tiles, x_ref, ys_ref, yr_ref, wout_ref, gx_ref, wq_ref,
                     mem_ref, gm_ref, wkv_ref, wo_ref, gf_ref, o_ref,
                     h1_a, q_a, h1_b, q_b, k_scr, v_scr):
    step = pl.program_id(0)

    @pl.when(step == 0)
    def _():
        h1_b[...] = jnp.zeros_like(h1_b)
        q_b[...] = jnp.zeros_like(q_b)

    @pl.when((step % tiles_per_batch == 0) & (step < n_tiles))
    def _():
        slot = (step // tiles_per_batch) % 2
        mn = _rms_norm(mem_ref[...], gm_ref[...]).astype(BF16)
        k = jnp.dot(mn, wkv_ref[:, :D_MODEL], preferred_element_type=F32)
        k_scr[slot] = (k * (XATTN_HEAD_DIM ** -0.5)).astype(BF16)
        v_scr[slot] = jnp.dot(mn, wkv_ref[:, D_MODEL:], preferred_element_type=F32).astype(BF16)

    back_slot = (jnp.maximum(step - 1, 0) // tiles_per_batch) % 2
    k_ref, v_ref = k_scr.at[back_slot], v_scr.at[back_slot]

    def run(nxt, cur):
        front = _out_attn_front(x_ref, ys_ref, yr_ref, wout_ref, gx_ref, wq_ref, *nxt)
        back = _out_attn_back(*cur, k_ref, v_ref, wo_ref, gf_ref, o_ref)
        for stage in OUT_ATTN_ORDER:
            next(front if stage == "F" else back)

    @pl.when(step % 2 == 0)
    def _():
        run((h1_a, q_a), (h1_b, q_b))

    @pl.when(step % 2 == 1)
    def _():
        run((h1_b, q_b), (h1_a, q_a))


def _out_attn(x2, y_sgu, y_rwkv, w_out, g_x, w_q, mem2, g_mem, w_kv, w_o, g_f, seq):
    m = x2.shape[0]
    tq = ATTN_ROWS
    per_batch = seq // tq
    n_tiles = m // tq
    front_tile = lambda s: jnp.minimum(s, n_tiles - 1)
    front_blk = lambda n: pl.BlockSpec((tq, n), lambda s: (front_tile(s), 0))
    full = lambda a, b: pl.BlockSpec((a, b), lambda s: (0, 0))
    mem_blk = pl.BlockSpec((MEM_LEN, D_MODEL), lambda s: (front_tile(s) // per_batch, 0))
    return pl.pallas_call(
        functools.partial(_out_attn_kernel, per_batch, n_tiles),
        grid=(n_tiles + 1,),
        in_specs=[
            front_blk(D_MODEL), front_blk(D_SGU), front_blk(D_RWKV),
            full(D_SGU + D_RWKV, D_MODEL), full(1, D_MODEL), full(D_MODEL, D_MODEL),
            mem_blk, full(1, D_MODEL), full(D_MODEL, 2 * D_MODEL),
            full(D_MODEL, D_MODEL), full(1, D_MODEL),
        ],
        out_specs=pl.BlockSpec((tq, D_MODEL), lambda s: (jnp.maximum(s - 1, 0), 0)),
        out_shape=jax.ShapeDtypeStruct((m, D_MODEL), F32),
        scratch_shapes=2 * [pltpu.VMEM((tq, D_MODEL), F32), pltpu.VMEM((tq, D_MODEL), BF16)]
        + 2 * [pltpu.VMEM((2, MEM_LEN, D_MODEL), BF16)],
        compiler_params=pltpu.CompilerParams(
            dimension_semantics=("arbitrary",), vmem_limit_bytes=VMEM_LIMIT_BYTES),
        name="out_attn",
    )(x2, y_sgu, y_rwkv, w_out, g_x, w_q, mem2, g_mem, w_kv, w_o, g_f)


def kernel(x, mem, ln_mix_g, w_in, sgu_ln_g, sgu_ln_b, sgu_ws, sgu_bs, sgu_out_g, rw_mu, rw_w0, rw_w2, rw_a0, rw_a2, rw_k_k, rw_k_a, rw_r_k, rw_gn_g, rw_gn_b, w_out, ln_x_g, ln_mem_g, w_q, w_kv, w_o, ln_f_g):
    batch, seq, _ = x.shape
    assert w_in.shape[0] == 1, "the final norm is fused into the only layer's last call"
    row = lambda a: a.reshape(1, -1)
    h = x.reshape(batch * seq, D_MODEL)
    mem2 = mem.reshape(batch * MEM_LEN, D_MODEL)
    bs_full = jnp.repeat(sgu_bs[0].T, SGU_HEAD_DIM, axis=1)
    sgu_par = (row(sgu_ln_g[0]), row(sgu_ln_b[0]), sgu_ws[0], bs_full, row(sgu_out_g[0]))
    rwkv_par = (row(rw_w0[0]), rw_w2[0].astype(BF16), row(rw_a0[0]), rw_a2[0].astype(BF16),
                row(rw_k_k[0]), row(rw_k_a[0]), row(rw_r_k[0]))
    y_sgu, prepared, g_last = _in_proj(h, row(ln_mix_g[0]), w_in[0].astype(BF16), sgu_par,
                                       row(rw_mu[0]), rwkv_par, seq)
    y_rwkv = _rwkv(prepared, g_last, row(rw_gn_g[0]), row(rw_gn_b[0]), batch, seq)
    h = _out_attn(h, y_sgu, y_rwkv, w_out[0].astype(BF16), row(ln_x_g[0]), w_q[0].astype(BF16),
                  mem2, row(ln_mem_g[0]), w_kv[0].astype(BF16), w_o[0].astype(BF16),
                  row(ln_f_g), seq)
    return h.reshape(batch, seq, D_MODEL)
```

```python
import functools
import math

import jax
import jax.numpy as jnp
from jax import lax
from jax.experimental import pallas as pl
from jax.experimental.pallas import tpu as pltpu

D_MODEL = 1024
MEM_LEN = 256
D_SGU = 512
SGU_HEADS = 4
SGU_HEAD_DIM = D_SGU // SGU_HEADS
SGU_CHUNK = 128
D_RWKV = 512
RWKV_HEAD_DIM = 64
RWKV_HEADS = D_RWKV // RWKV_HEAD_DIM
LORA = 64
C_SGU = 3 * D_SGU
C_RWKV = 4 * D_RWKV + 2 * LORA
XATTN_HEADS = 4
XATTN_HEAD_DIM = D_MODEL // XATTN_HEADS
RMS_EPS = 1e-6
LN_EPS = 1e-5
GN_EPS = 64e-5
KK_NORM_FLOOR = 1e-24

RWKV_CHUNK = 64
RWKV_BATCHES = 8
PAIR = 2 * RWKV_HEAD_DIM
RWKV_PAIRS = RWKV_HEADS // 2
IN_PROJ_ROWS = 512
ATTN_ROWS = 512
VMEM_LIMIT_BYTES = 48 * 1024 * 1024

SUBLANES = 8
LOG2_E = math.log2(math.e)

F32 = jnp.float32
BF16 = jnp.bfloat16


def _dot(a, b):
    return jnp.dot(a.astype(BF16), b.astype(BF16), preferred_element_type=F32)


def _dot_nt(a, b):
    return lax.dot_general(a.astype(BF16), b.astype(BF16), (((1,), (1,)), ((), ())),
                           preferred_element_type=F32)


def _dot_tn(a, b):
    return lax.dot_general(a.astype(BF16), b.astype(BF16), (((0,), (0,)), ((), ())),
                           preferred_element_type=F32)


def _rms_norm(x, g):
    return x * lax.rsqrt(jnp.mean(x * x, axis=-1, keepdims=True) + RMS_EPS) * g


def _gelu(x):
    return 0.5 * x * (1.0 + lax.erf(x * (2.0 ** -0.5)))


def _silu(x):
    return x * jax.nn.sigmoid(x)


def _head_sum(x, lane_a):
    outs = []
    for p in range(RWKV_PAIRS):
        xp = x[:, p * PAIR:(p + 1) * PAIR]
        sum_a = jnp.sum(jnp.where(lane_a, xp, 0.0), axis=-1, keepdims=True)
        sum_b = jnp.sum(jnp.where(lane_a, 0.0, xp), axis=-1, keepdims=True)
        outs.append(jnp.where(lane_a, sum_a, sum_b))
    return jnp.concatenate(outs, axis=-1)


RWKV_OPERANDS = ("a_t", "r_t", "b_t", "k_t", "v")
RWKV_EPILOGUE = ("bonus", "gate")
RWKV_PREPARE_PIECES = 3


def _sgu_chunk(z, lng_ref, lnb_ref, ws_c, bs_ref, og_ref):
    u = _gelu(z[:, 0:D_SGU])
    v = _gelu(z[:, D_SGU:2 * D_SGU])
    gate = z[:, 2 * D_SGU:3 * D_SGU]
    vc = v - jnp.mean(v, axis=-1, keepdims=True)
    var = jnp.mean(vc * vc, axis=-1, keepdims=True)
    vn = (vc * lax.rsqrt(var + LN_EPS) * lng_ref[...] + lnb_ref[...]).astype(BF16)
    sv = jnp.concatenate(
        [jnp.dot(ws_c[h], vn[:, h * SGU_HEAD_DIM:(h + 1) * SGU_HEAD_DIM],
                 preferred_element_type=F32) for h in range(SGU_HEADS)], axis=1) + bs_ref[...]
    return _rms_norm(u * sv, og_ref[...]) * _silu(gate)


def _rwkv_prepare(zr_ref, par, out, glast_ref):
    C, N = RWKV_CHUNK, RWKV_HEAD_DIM
    w0_ref, w2_ref, a0_ref, a2_ref, kk_ref, ka_ref, rk_ref = par
    lane_a = lax.broadcasted_iota(jnp.int32, (C, PAIR), 1) < N
    row_c = lax.broadcasted_iota(jnp.int32, (C, C), 0)
    col_c = lax.broadcasted_iota(jnp.int32, (C, C), 1)
    tril = (col_c <= row_c).astype(F32).astype(BF16)
    tril3 = jnp.concatenate([tril, tril, tril], axis=1)

    def one_chunk(c):
        rows = slice(c * C, (c + 1) * C)
        r = zr_ref[rows, 0:D_RWKV]
        k = zr_ref[rows, D_RWKV:2 * D_RWKV]
        v = zr_ref[rows, 2 * D_RWKV:3 * D_RWKV]
        gate_act = zr_ref[rows, 3 * D_RWKV:4 * D_RWKV]
        wd = zr_ref[rows, 4 * D_RWKV:4 * D_RWKV + LORA]
        ad = zr_ref[rows, 4 * D_RWKV + LORA:4 * D_RWKV + 2 * LORA]

        lw = w0_ref[...] + _dot(jnp.tanh(wd), w2_ref[...])
        ld = -(math.exp(-0.5) * LOG2_E) * jax.nn.sigmoid(lw)
        icl = jax.nn.sigmoid(a0_ref[...] + _dot(ad, a2_ref[...]))
        yield
        k2 = k * ((1.0 - ka_ref[...]) + ka_ref[...] * icl)
        kk = k * kk_ref[...]
        kk = kk * lax.rsqrt(jnp.maximum(_head_sum(kk * kk, lane_a), KK_NORM_FLOOR))
        bvec = kk * icl

        p1 = ld.astype(BF16)
        r1 = ld - p1.astype(F32)
        p2 = r1.astype(BF16)
        p3 = (r1 - p2.astype(F32)).astype(BF16)
        cum = jnp.dot(tril3, jnp.concatenate([p1, p2, p3], axis=0), preferred_element_type=F32)
        last = cum[C - 1:C, :]
        yield
        e_neg = jnp.exp2(-cum)
        out["a_t"][rows, :] = (-kk * jnp.exp2(cum - ld)).astype(BF16)
        out["r_t"][rows, :] = (r * jnp.exp2(cum)).astype(BF16)
        out["b_t"][rows, :] = (bvec * e_neg).astype(BF16)
        out["k_t"][rows, :] = (k2 * e_neg).astype(BF16)
        out["v"][rows, :] = v.astype(BF16)
        out["bonus"][rows, :] = _head_sum(r * k2 * rk_ref[...], lane_a) * v
        out["gate"][rows, :] = gate_act
        glast_ref[c] = jnp.exp2(last)
        yield

    chunks = [one_chunk(c) for c in range(IN_PROJ_ROWS // C)]
    for _ in range(RWKV_PREPARE_PIECES):
        for gen in chunks:
            next(gen)
            yield


def _in_proj_kernel(tiles_per_batch, n_tiles, x_ref, g_ref, w_ref, lng_ref, lnb_ref, ws_ref,
                    bs_ref, og_ref, mu_ref, w0_ref, w2_ref, a0_ref, a2_ref, kk_ref, ka_ref, rk_ref,
                    ys_ref, at_ref, rt_ref, bt_ref, kt_ref, v_ref, bonus_ref,
                    gate_ref, glast_ref, zs_ref, zr_ref, prev_ref):
    tile = pl.program_id(0)
    par = (w0_ref, w2_ref, a0_ref, a2_ref, kk_ref, ka_ref, rk_ref)
    out = dict(zip(RWKV_OPERANDS + RWKV_EPILOGUE,
                   (at_ref, rt_ref, bt_ref, kt_ref, v_ref, bonus_ref, gate_ref)))

    @pl.when(tile == 0)
    def _():
        prev_ref[...] = jnp.zeros_like(prev_ref)
        zr_ref[...] = jnp.zeros_like(zr_ref)

    @pl.when(tile < n_tiles)
    def _():
        prepare = _rwkv_prepare(zr_ref, par, out, glast_ref)
        first_tile = tile % tiles_per_batch == 0
        row8 = lax.broadcasted_iota(jnp.int32, (SUBLANES, 1), 0)
        xn = _rms_norm(x_ref[...], g_ref[...]).astype(BF16)
        per_dot = RWKV_PREPARE_PIECES * (IN_PROJ_ROWS // RWKV_CHUNK) // 3
        for j in range(3):
            for _ in range(per_dot):
                next(prepare)
            cols = slice(j * D_SGU, (j + 1) * D_SGU)
            zs_ref[:, cols] = jnp.dot(xn, w_ref[:, cols], preferred_element_type=F32)
        for _ in prepare:
            pass
        row = lax.broadcasted_iota(jnp.int32, (SGU_CHUNK, SGU_CHUNK), 0)
        col = lax.broadcasted_iota(jnp.int32, (SGU_CHUNK, SGU_CHUNK), 1)
        ws_c = [jnp.where(col <= row, ws_ref[h], 0.0).astype(BF16) for h in range(SGU_HEADS)]
        n_chunks = IN_PROJ_ROWS // SGU_CHUNK
        lane_tiles = C_RWKV // PAIR
        bounds = [C_SGU + PAIR * (lane_tiles * c // n_chunks) for c in range(n_chunks + 1)]
        gate_cols = slice(3 * D_RWKV, 4 * D_RWKV)
        for c in range(n_chunks):
            z = jnp.dot(xn, w_ref[:, bounds[c]:bounds[c + 1]], preferred_element_type=F32)
            cols = slice(bounds[c] - C_SGU, bounds[c + 1] - C_SGU)
            prev = jnp.where(first_tile, 0.0, prev_ref[:, cols])
            rolled = pltpu.roll(z, 1, axis=0)
            z_prev = jnp.concatenate(
                [jnp.where(row8 == 0, prev, rolled[:SUBLANES]), rolled[SUBLANES:]], axis=0)
            prev_ref[:, cols] = z[IN_PROJ_ROWS - 1:IN_PROJ_ROWS, :]
            zs = z + (z_prev - z) * mu_ref[:, cols]
            if cols.start <= gate_cols.start < cols.stop:
                assert gate_cols.stop <= cols.stop
                lo, hi = gate_cols.start - cols.start, gate_cols.stop - cols.start
                pieces = [zs[:, :lo], _silu(zs[:, lo:hi]), zs[:, hi:]]
                zs = jnp.concatenate([p for p in pieces if p.shape[1]], axis=1)
            zr_ref[:, cols] = zs
            rows = slice(c * SGU_CHUNK, (c + 1) * SGU_CHUNK)
            ys_ref[rows, :] = _sgu_chunk(zs_ref[rows, :], lng_ref, lnb_ref, ws_c, bs_ref,
                                         og_ref).astype(ys_ref.dtype)

    @pl.when(tile == n_tiles)
    def _():
        for _ in _rwkv_prepare(zr_ref, par, out, glast_ref):
            pass


def _in_proj(x2, g, w_in, sgu_par, mu, rwkv_par, seq):
    m = x2.shape[0]
    tm = IN_PROJ_ROWS
    n_tiles = m // tm
    chunks = tm // RWKV_CHUNK
    assert (RWKV_PREPARE_PIECES * chunks) % 3 == 0
    const = lambda a: pl.BlockSpec(a.shape, lambda i: (0,) * a.ndim)
    cur = lambda i: jnp.minimum(i, n_tiles - 1)
    last = lambda i: jnp.maximum(i - 1, 0)
    tokens = lambda dt: jax.ShapeDtypeStruct((m, D_RWKV), dt)
    outs = pl.pallas_call(
        functools.partial(_in_proj_kernel, seq // tm, n_tiles),
        grid=(n_tiles + 1,),
        in_specs=[pl.BlockSpec((tm, D_MODEL), lambda i: (cur(i), 0)), const(g), const(w_in)]
        + [const(a) for a in sgu_par] + [const(mu)] + [const(a) for a in rwkv_par],
        out_specs=[pl.BlockSpec((tm, D_SGU), lambda i: (cur(i), 0))]
        + [pl.BlockSpec((tm, D_RWKV), lambda i: (last(i), 0))
           for _ in RWKV_OPERANDS + RWKV_EPILOGUE]
        + [pl.BlockSpec((chunks, 1, D_RWKV), lambda i: (last(i), 0, 0))],
        out_shape=[jax.ShapeDtypeStruct((m, D_SGU), BF16)]
        + [tokens(BF16) for _ in RWKV_OPERANDS] + [tokens(F32) for _ in RWKV_EPILOGUE]
        + [jax.ShapeDtypeStruct((m // RWKV_CHUNK, 1, D_RWKV), F32)],
        scratch_shapes=[
            pltpu.VMEM((tm, C_SGU), F32),
            pltpu.VMEM((tm, C_RWKV), F32),
            pltpu.VMEM((1, C_RWKV), F32),
        ],
        compiler_params=pltpu.CompilerParams(
            dimension_semantics=("arbitrary",), vmem_limit_bytes=VMEM_LIMIT_BYTES),
        name="in_proj",
    )(x2, g, w_in, *sgu_par, mu, *rwkv_par)
    y_sgu, prepared, g_last = outs[0], outs[1:-1], outs[-1]
    return y_sgu, dict(zip(RWKV_OPERANDS + RWKV_EPILOGUE, prepared)), g_last


def _block_diag(x, keep_a, keep_b):
    return jnp.concatenate([x * keep_a, x * keep_b], axis=0)


def _dots_paired(lhs, rhs):
    out = []
    for i in range(0, len(lhs), 2):
        m, n = lhs[i].shape[0], rhs[i].shape[1]
        both = jnp.dot(jnp.concatenate([lhs[i], lhs[i + 1]], axis=0),
                       jnp.concatenate([rhs[i], rhs[i + 1]], axis=1),
                       preferred_element_type=F32)
        out += [both[:m, :n], both[m:, n:]]
    return out


def _dots_paired_tn(lhs, rhs):
    out = []
    for i in range(0, len(lhs), 2):
        m, n = lhs[i].shape[1], rhs[i].shape[1]
        both = _dot_tn(jnp.concatenate([lhs[i], lhs[i + 1]], axis=1),
                       jnp.concatenate([rhs[i], rhs[i + 1]], axis=1))
        out += [both[:m, :n], both[m:, n:]]
    return out


def _rwkv_kernel(at_ref, rt_ref, bt_ref, kt_ref, v_ref, bonus_ref, gate_ref,
                 glast_ref, gng_ref, gnb_ref, o_ref, state_ref):
    C, N = RWKV_CHUNK, RWKV_HEAD_DIM

    @pl.when(pl.program_id(1) == 0)
    def _():
        state_ref[...] = jnp.zeros_like(state_ref)

    row = lax.broadcasted_iota(jnp.int32, (C, PAIR), 0)
    lane = lax.broadcasted_iota(jnp.int32, (C, PAIR), 1)
    col = lane & (N - 1)
    lane_a = lane < N
    incl = col <= row
    strict = col < row
    eye = col == row

    units = [(b, p) for b in range(RWKV_BATCHES) for p in range(RWKV_PAIRS)]
    nu = len(units)
    lanes = lambda u: slice(u[1] * PAIR, (u[1] + 1) * PAIR)
    of = lambda ref, u: ref[u[0], :, lanes(u)]
    as_keep = lambda mask: jnp.where(mask, 1.0, 0.0).astype(BF16)
    keep_a, keep_b = as_keep(lane_a), as_keep(jnp.logical_not(lane_a))
    bd = lambda x: _block_diag(x, keep_a, keep_b)

    s_bk = [_dot_nt(jnp.concatenate([of(at_ref, u), of(rt_ref, u)], axis=0),
                    jnp.concatenate([bd(of(bt_ref, u)), bd(of(kt_ref, u))], axis=0))
            for u in units]
    s_b = [s[:, :PAIR] for s in s_bk]
    s_k = [s[:, PAIR:] for s in s_bk]
    n_mat = [jnp.where(strict, s[:C], 0.0) for s in s_b]
    n_bf = [n.astype(BF16) for n in n_mat]
    a_ak = [jnp.where(strict, s[:C], 0.0).astype(BF16) for s in s_k]
    a_rb = [jnp.where(incl, s[C:], 0.0).astype(BF16) for s in s_b]
    a_rk = [jnp.where(incl, s[C:], 0.0).astype(BF16) for s in s_k]

    lvl = ((row >> 1) == (col >> 1)) & ((row & 1) == 1) & ((col & 1) == 0)
    x = [eye.astype(F32) + jnp.where(lvl, n, 0.0) for n in n_mat]
    s = 2
    while s < C:
        shift = s.bit_length()
        lvl = ((row >> shift) == (col >> shift)) & ((row & s) != 0) & ((col & s) == 0)
        lvl_a, lvl_b = as_keep(lvl & lane_a), as_keep(lvl & jnp.logical_not(lane_a))
        x_bf = [xi.astype(BF16) for xi in x]
        if s < SUBLANES:
            xn = _dots_paired(x_bf, [_block_diag(n, lvl_a, lvl_b) for n in n_bf])
            xnx = _dots_paired([t.astype(BF16) for t in xn], [bd(t) for t in x_bf])
            x = [x[i] + xnx[i] for i in range(nu)]
        else:
            groups = [g for g in range(C // SUBLANES) if (g * SUBLANES) & s]
            rows_of = lambda m, g: m[g * SUBLANES:(g + 1) * SUBLANES]
            moving = [jnp.concatenate([rows_of(xi, g) for g in groups], axis=0).astype(BF16)
                      for xi in x]
            xn = _dots_paired(moving, [_block_diag(n, lvl_a, lvl_b) for n in n_bf])
            xnx = _dots_paired([t.astype(BF16) for t in xn], [bd(t) for t in x_bf])
            x = [jnp.concatenate(
                [rows_of(x[i], g) + rows_of(xnx[i], groups.index(g)) if g in groups
                 else rows_of(x[i], g) for g in range(C // SUBLANES)], axis=0)
                 for i in range(nu)]
        s *= 2

    x_bf = [xi.astype(BF16) for xi in x]
    v_bd = [bd(of(v_ref, u)) for u in units]
    av = [t.astype(BF16) for t in _dots_paired(a_ak, v_bd)]
    p_mat = [t.astype(BF16) for t in _dots_paired(x_bf, [bd(of(at_ref, u)) for u in units])]

    t_old = [of(state_ref, u) for u in units]
    t_bd = [bd(t.astype(BF16)) for t in t_old]
    u_mat = _dots_paired(
        [jnp.concatenate([p_mat[i], x_bf[i]], axis=1) for i in range(nu)],
        [jnp.concatenate([t_bd[i], bd(av[i])], axis=0) for i in range(nu)])
    u_bf = [um.astype(BF16) for um in u_mat]
    y = _dots_paired(
        [jnp.concatenate([of(rt_ref, u), a_rb[i], a_rk[i]], axis=1) for i, u in enumerate(units)],
        [jnp.concatenate([t_bd[i], bd(u_bf[i]), v_bd[i]], axis=0) for i in range(nu)])
    full = _dots_paired_tn(
        [jnp.concatenate([of(bt_ref, u), of(kt_ref, u)], axis=0) for u in units],
        [jnp.concatenate([u_bf[i], of(v_ref, u)], axis=0) for i, u in enumerate(units)])
    eye_full = jnp.concatenate([eye] * RWKV_PAIRS, axis=1)
    g_col = [_head_sum(jnp.where(eye_full, glast_ref[b, 0], 0.0), lane_a)
             for b in range(RWKV_BATCHES)]
    for i, u in enumerate(units):
        state_ref[u[0], :, lanes(u)] = g_col[u[0]][:, lanes(u)] * (
            t_old[i] + jnp.where(lane_a, full[i][:N], full[i][N:]))

    for b in range(RWKV_BATCHES):
        yb = jnp.concatenate(y[b * RWKV_PAIRS:(b + 1) * RWKV_PAIRS], axis=1)
        yc = yb - _head_sum(yb, lane_a) * (1.0 / N)
        var = _head_sum(yc * yc, lane_a) * (1.0 / N)
        yn = yc * lax.rsqrt(var + GN_EPS) * gng_ref[...] + gnb_ref[...]
        o_ref[b] = ((yn + bonus_ref[b]) * gate_ref[b]).astype(o_ref.dtype)


def _rwkv(prepared, g_last, gn_g, gn_b, batch, seq):
    C, NB = RWKV_CHUNK, RWKV_BATCHES
    blk = pl.BlockSpec((NB, C, D_RWKV), lambda g, c: (g, c, 0))
    vec = pl.BlockSpec((1, D_RWKV), lambda g, c: (0, 0))
    arrays = [prepared[name].reshape(batch, seq, D_RWKV) for name in RWKV_OPERANDS + RWKV_EPILOGUE]
    out = pl.pallas_call(
        _rwkv_kernel,
        grid=(batch // NB, seq // C),
        in_specs=[blk for _ in arrays]
        + [pl.BlockSpec((NB, 1, 1, D_RWKV), lambda g, c: (g, c, 0, 0)), vec, vec],
        out_specs=blk,
        out_shape=jax.ShapeDtypeStruct((batch, seq, D_RWKV), BF16),
        scratch_shapes=[
            pltpu.VMEM((NB, RWKV_HEAD_DIM, D_RWKV), F32),
        ],
        compiler_params=pltpu.CompilerParams(
            dimension_semantics=("arbitrary", "arbitrary"), vmem_limit_bytes=VMEM_LIMIT_BYTES),
        name="rwkv",
    )(*arrays, g_last.reshape(batch, seq // C, 1, D_RWKV), gn_g, gn_b)
    return out.reshape(batch * seq, D_RWKV)


def _out_attn_front(x_ref, ys_ref, yr_ref, wout_ref, gx_ref, wq_ref, h1_ref, q_ref):
    y = jnp.concatenate([ys_ref[...], yr_ref[...]], axis=1)
    h1 = x_ref[...] + jnp.dot(y, wout_ref[...], preferred_element_type=F32)
    h1_ref[...] = h1
    yield
    hn = _rms_norm(h1, gx_ref[...]).astype(BF16)
    yield
    q_ref[...] = jnp.dot(hn, wq_ref[...], preferred_element_type=F32).astype(BF16)
    yield


def _out_attn_back(h1_ref, q_ref, k_ref, v_ref, wo_ref, gf_ref, o_ref):
    cols = [slice(h * XATTN_HEAD_DIM, (h + 1) * XATTN_HEAD_DIM) for h in range(XATTN_HEADS)]
    scores = [_dot_nt(q_ref[:, sl], k_ref[:, sl]) for sl in cols]
    yield
    probs = []
    for s in scores:
        e = jnp.exp(s - jnp.max(s, axis=-1, keepdims=True))
        probs.append((e / jnp.sum(e, axis=-1, keepdims=True)).astype(BF16))
    yield
    o = jnp.concatenate(
        [jnp.dot(p, v_ref[:, sl], preferred_element_type=F32).astype(BF16)
         for p, sl in zip(probs, cols)], axis=-1)
    yield
    h2 = h1_ref[...] + jnp.dot(o, wo_ref[...], preferred_element_type=F32)
    yield
    o_ref[...] = _rms_norm(h2, gf_ref[...])
    yield


OUT_ATTN_ORDER = "BFBBFBBF"


def _out_attn_kernel(tiles_per_batch, n_tiles, x_ref, ys_ref, yr_ref, wout_ref, gx_ref, wq_ref,
                     mem_ref, gm_ref, wkv_ref, wo_ref, gf_ref, o_ref,
                     h1_a, q_a, h1_b, q_b, k_scr, v_scr):
    step = pl.program_id(0)

    @pl.when(step == 0)
    def _():
        h1_b[...] = jnp.zeros_like(h1_b)
        q_b[...] = jnp.zeros_like(q_b)

    @pl.when((step % tiles_per_batch == 0) & (step < n_tiles))
    def _():
        slot = (step // tiles_per_batch) % 2
        mn = _rms_norm(mem_ref[...], gm_ref[...]).astype(BF16)
        k = jnp.dot(mn, wkv_ref[:, :D_MODEL], preferred_element_type=F32)
        k_scr[slot] = (k * (XATTN_HEAD_DIM ** -0.5)).astype(BF16)
        v_scr[slot] = jnp.dot(mn, wkv_ref[:, D_MODEL:], preferred_element_type=F32).astype(BF16)

    back_slot = (jnp.maximum(step - 1, 0) // tiles_per_batch) % 2
    k_ref, v_ref = k_scr.at[back_slot], v_scr.at[back_slot]

    def run(nxt, cur):
        front = _out_attn_front(x_ref, ys_ref, yr_ref, wout_ref, gx_ref, wq_ref, *nxt)
        back = _out_attn_back(*cur, k_ref, v_ref, wo_ref, gf_ref, o_ref)
        for stage in OUT_ATTN_ORDER:
            next(front if stage == "F" else back)

    @pl.when(step % 2 == 0)
    def _():
        run((h1_a, q_a), (h1_b, q_b))

    @pl.when(step % 2 == 1)
    def _():
        run((h1_b, q_b), (h1_a, q_a))


def _out_attn(x2, y_sgu, y_rwkv, w_out, g_x, w_q, mem2, g_mem, w_kv, w_o, g_f, seq):
    m = x2.shape[0]
    tq = ATTN_ROWS
    per_batch = seq // tq
    n_tiles = m // tq
    front_tile = lambda s: jnp.minimum(s, n_tiles - 1)
    front_blk = lambda n: pl.BlockSpec((tq, n), lambda s: (front_tile(s), 0))
    full = lambda a, b: pl.BlockSpec((a, b), lambda s: (0, 0))
    mem_blk = pl.BlockSpec((MEM_LEN, D_MODEL), lambda s: (front_tile(s) // per_batch, 0))
    return pl.pallas_call(
        functools.partial(_out_attn_kernel, per_batch, n_tiles),
        grid=(n_tiles + 1,),
        in_specs=[
            front_blk(D_MODEL), front_blk(D_SGU), front_blk(D_RWKV),
            full(D_SGU + D_RWKV, D_MODEL), full(1, D_MODEL), full(D_MODEL, D_MODEL),
            mem_blk, full(1, D_MODEL), full(D_MODEL, 2 * D_MODEL),
            full(D_MODEL, D_MODEL), full(1, D_MODEL),
        ],
        out_specs=pl.BlockSpec((tq, D_MODEL), lambda s: (jnp.maximum(s - 1, 0), 0)),
        out_shape=jax.ShapeDtypeStruct((m, D_MODEL), F32),
        scratch_shapes=2 * [pltpu.VMEM((tq, D_MODEL), F32), pltpu.VMEM((tq, D_MODEL), BF16)]
        + 2 * [pltpu.VMEM((2, MEM_LEN, D_MODEL), BF16)],
        compiler_params=pltpu.CompilerParams(
            dimension_semantics=("arbitrary",), vmem_limit_bytes=VMEM_LIMIT_BYTES),
        name="out_attn",
    )(x2, y_sgu, y_rwkv, w_out, g_x, w_q, mem2, g_mem, w_kv, w_o, g_f)


def kernel(x, mem, ln_mix_g, w_in, sgu_ln_g, sgu_ln_b, sgu_ws, sgu_bs, sgu_out_g, rw_mu, rw_w0, rw_w2, rw_a0, rw_a2, rw_k_k, rw_k_a, rw_r_k, rw_gn_g, rw_gn_b, w_out, ln_x_g, ln_mem_g, w_q, w_kv, w_o, ln_f_g):
    batch, seq, _ = x.shape
    assert w_in.shape[0] == 1, "the final norm is fused into the only layer's last call"
    row = lambda a: a.reshape(1, -1)
    h = x.reshape(batch * seq, D_MODEL)
    mem2 = mem.reshape(batch * MEM_LEN, D_MODEL)
    bs_full = jnp.repeat(sgu_bs[0].T, SGU_HEAD_DIM, axis=1)
    sgu_par = (row(sgu_ln_g[0]), row(sgu_ln_b[0]), sgu_ws[0], bs_full, row(sgu_out_g[0]))
    rwkv_par = (row(rw_w0[0]), rw_w2[0].astype(BF16), row(rw_a0[0]), rw_a2[0].astype(BF16),
                row(rw_k_k[0]), row(rw_k_a[0]), row(rw_r_k[0]))
    y_sgu, prepared, g_last = _in_proj(h, row(ln_mix_g[0]), w_in[0].astype(BF16), sgu_par,
                                       row(rw_mu[0]), rwkv_par, seq)
    y_rwkv = _rwkv(prepared, g_last, row(rw_gn_g[0]), row(rw_gn_b[0]), batch, seq)
    h = _out_attn(h, y_sgu, y_rwkv, w_out[0].astype(BF16), row(ln_x_g[0]), w_q[0].astype(BF16),
                  mem2, row(ln_mem_g[0]), w_kv[0].astype(BF16), w_o[0].astype(BF16),
                  row(ln_f_g), seq)
    return h.reshape(batch, seq, D_MODEL)
```

```python
import functools
import math

import jax
import jax.numpy as jnp
from jax import lax
from jax.experimental import pallas as pl
from jax.experimental.pallas import tpu as pltpu

D_MODEL = 1024
MEM_LEN = 256
D_SGU = 512
SGU_HEADS = 4
SGU_HEAD_DIM = D_SGU // SGU_HEADS
SGU_CHUNK = 128
D_RWKV = 512
RWKV_HEAD_DIM = 64
RWKV_HEADS = D_RWKV // RWKV_HEAD_DIM
LORA = 64
C_SGU = 3 * D_SGU
C_RWKV = 4 * D_RWKV + 2 * LORA
XATTN_HEADS = 4
XATTN_HEAD_DIM = D_MODEL // XATTN_HEADS
RMS_EPS = 1e-6
LN_EPS = 1e-5
GN_EPS = 64e-5
KK_NORM_FLOOR = 1e-24

RWKV_CHUNK = 64
RWKV_BATCHES = 8
PAIR = 2 * RWKV_HEAD_DIM
RWKV_PAIRS = RWKV_HEADS // 2
IN_PROJ_ROWS = 512
ATTN_ROWS = 512
KV_BATCHES = 2
VMEM_LIMIT_BYTES = 48 * 1024 * 1024

SUBLANES = 8
LOG2_E = math.log2(math.e)

F32 = jnp.float32
BF16 = jnp.bfloat16


def _dot(a, b):
    return jnp.dot(a.astype(BF16), b.astype(BF16), preferred_element_type=F32)


def _dot_nt(a, b):
    return lax.dot_general(a.astype(BF16), b.astype(BF16), (((1,), (1,)), ((), ())),
                           preferred_element_type=F32)


def _dot_tn(a, b):
    return lax.dot_general(a.astype(BF16), b.astype(BF16), (((0,), (0,)), ((), ())),
                           preferred_element_type=F32)


def _rms_norm(x, g):
    return x * lax.rsqrt(jnp.mean(x * x, axis=-1, keepdims=True) + RMS_EPS) * g


def _gelu(x):
    return 0.5 * x * (1.0 + lax.erf(x * (2.0 ** -0.5)))


def _silu(x):
    return x * jax.nn.sigmoid(x)


def _head_sum(x, lane_a):
    outs = []
    for p in range(RWKV_PAIRS):
        xp = x[:, p * PAIR:(p + 1) * PAIR]
        sum_a = jnp.sum(jnp.where(lane_a, xp, 0.0), axis=-1, keepdims=True)
        sum_b = jnp.sum(jnp.where(lane_a, 0.0, xp), axis=-1, keepdims=True)
        outs.append(jnp.where(lane_a, sum_a, sum_b))
    return jnp.concatenate(outs, axis=-1)


RWKV_OPERANDS = ("a_t", "r_t", "b_t", "k_t", "v")
RWKV_EPILOGUE = ("bonus", "gate")
RWKV_PREPARE_PIECES = 3


def _sgu_chunk(z, lng_ref, lnb_ref, ws_c, bs_ref, og_ref):
    u = _gelu(z[:, 0:D_SGU])
    v = _gelu(z[:, D_SGU:2 * D_SGU])
    gate = z[:, 2 * D_SGU:3 * D_SGU]
    vc = v - jnp.mean(v, axis=-1, keepdims=True)
    var = jnp.mean(vc * vc, axis=-1, keepdims=True)
    vn = (vc * lax.rsqrt(var + LN_EPS) * lng_ref[...] + lnb_ref[...]).astype(BF16)
    sv = jnp.concatenate(
        [jnp.dot(ws_c[h], vn[:, h * SGU_HEAD_DIM:(h + 1) * SGU_HEAD_DIM],
                 preferred_element_type=F32) for h in range(SGU_HEADS)], axis=1) + bs_ref[...]
    return _rms_norm(u * sv, og_ref[...]) * _silu(gate)


def _rwkv_prepare(zr_ref, par, out, glast_ref):
    C, N = RWKV_CHUNK, RWKV_HEAD_DIM
    w0_ref, w2_ref, a0_ref, a2_ref, kk_ref, ka_ref, rk_ref = par
    lane_a = lax.broadcasted_iota(jnp.int32, (C, PAIR), 1) < N
    row_c = lax.broadcasted_iota(jnp.int32, (C, C), 0)
    col_c = lax.broadcasted_iota(jnp.int32, (C, C), 1)
    tril = (col_c <= row_c).astype(F32).astype(BF16)
    tril3 = jnp.concatenate([tril, tril, tril], axis=1)

    def one_chunk(c):
        rows = slice(c * C, (c + 1) * C)
        r = zr_ref[rows, 0:D_RWKV]
        k = zr_ref[rows, D_RWKV:2 * D_RWKV]
        v = zr_ref[rows, 2 * D_RWKV:3 * D_RWKV]
        gate_act = zr_ref[rows, 3 * D_RWKV:4 * D_RWKV]
        wd = zr_ref[rows, 4 * D_RWKV:4 * D_RWKV + LORA]
        ad = zr_ref[rows, 4 * D_RWKV + LORA:4 * D_RWKV + 2 * LORA]

        lw = w0_ref[...] + _dot(jnp.tanh(wd), w2_ref[...])
        ld = -(math.exp(-0.5) * LOG2_E) * jax.nn.sigmoid(lw)
        icl = jax.nn.sigmoid(a0_ref[...] + _dot(ad, a2_ref[...]))
        yield
        k2 = k * ((1.0 - ka_ref[...]) + ka_ref[...] * icl)
        kk = k * kk_ref[...]
        kk = kk * lax.rsqrt(jnp.maximum(_head_sum(kk * kk, lane_a), KK_NORM_FLOOR))
        bvec = kk * icl

        p1 = ld.astype(BF16)
        r1 = ld - p1.astype(F32)
        p2 = r1.astype(BF16)
        p3 = (r1 - p2.astype(F32)).astype(BF16)
        cum = jnp.dot(tril3, jnp.concatenate([p1, p2, p3], axis=0), preferred_element_type=F32)
        last = cum[C - 1:C, :]
        yield
        e_neg = jnp.exp2(-cum)
        out["a_t"][rows, :] = (-kk * jnp.exp2(cum - ld)).astype(BF16)
        out["r_t"][rows, :] = (r * jnp.exp2(cum)).astype(BF16)
        out["b_t"][rows, :] = (bvec * e_neg).astype(BF16)
        out["k_t"][rows, :] = (k2 * e_neg).astype(BF16)
        out["v"][rows, :] = v.astype(BF16)
        out["bonus"][rows, :] = _head_sum(r * k2 * rk_ref[...], lane_a) * v
        out["gate"][rows, :] = gate_act
        glast_ref[c] = jnp.exp2(last)
        yield

    chunks = [one_chunk(c) for c in range(IN_PROJ_ROWS // C)]
    for _ in range(RWKV_PREPARE_PIECES):
        for gen in chunks:
            next(gen)
            yield


def _in_proj_kernel(tiles_per_batch, n_tiles, x_ref, g_ref, w_ref, lng_ref, lnb_ref, ws_ref,
                    bs_ref, og_ref, mu_ref, w0_ref, w2_ref, a0_ref, a2_ref, kk_ref, ka_ref, rk_ref,
                    ys_ref, at_ref, rt_ref, bt_ref, kt_ref, v_ref, bonus_ref,
                    gate_ref, glast_ref, zs_ref, zr_ref, prev_ref):
    tile = pl.program_id(0)
    par = (w0_ref, w2_ref, a0_ref, a2_ref, kk_ref, ka_ref, rk_ref)
    out = dict(zip(RWKV_OPERANDS + RWKV_EPILOGUE,
                   (at_ref, rt_ref, bt_ref, kt_ref, v_ref, bonus_ref, gate_ref)))

    @pl.when(tile == 0)
    def _():
        prev_ref[...] = jnp.zeros_like(prev_ref)
        zr_ref[...] = jnp.zeros_like(zr_ref)

    @pl.when(tile < n_tiles)
    def _():
        prepare = _rwkv_prepare(zr_ref, par, out, glast_ref)
        first_tile = tile % tiles_per_batch == 0
        row8 = lax.broadcasted_iota(jnp.int32, (SUBLANES, 1), 0)
        xn = _rms_norm(x_ref[...], g_ref[...]).astype(BF16)
        per_dot = RWKV_PREPARE_PIECES * (IN_PROJ_ROWS // RWKV_CHUNK) // 3
        for j in range(3):
            for _ in range(per_dot):
                next(prepare)
            cols = slice(j * D_SGU, (j + 1) * D_SGU)
            zs_ref[:, cols] = jnp.dot(xn, w_ref[:, cols], preferred_element_type=F32)
        for _ in prepare:
            pass
        row = lax.broadcasted_iota(jnp.int32, (SGU_CHUNK, SGU_CHUNK), 0)
        col = lax.broadcasted_iota(jnp.int32, (SGU_CHUNK, SGU_CHUNK), 1)
        ws_c = [jnp.where(col <= row, ws_ref[h], 0.0).astype(BF16) for h in range(SGU_HEADS)]
        n_chunks = IN_PROJ_ROWS // SGU_CHUNK
        lane_tiles = C_RWKV // PAIR
        bounds = [C_SGU + PAIR * (lane_tiles * c // n_chunks) for c in range(n_chunks + 1)]
        gate_cols = slice(3 * D_RWKV, 4 * D_RWKV)
        for c in range(n_chunks):
            z = jnp.dot(xn, w_ref[:, bounds[c]:bounds[c + 1]], preferred_element_type=F32)
            cols = slice(bounds[c] - C_SGU, bounds[c + 1] - C_SGU)
            prev = jnp.where(first_tile, 0.0, prev_ref[:, cols])
            rolled = pltpu.roll(z, 1, axis=0)
            z_prev = jnp.concatenate(
                [jnp.where(row8 == 0, prev, rolled[:SUBLANES]), rolled[SUBLANES:]], axis=0)
            prev_ref[:, cols] = z[IN_PROJ_ROWS - 1:IN_PROJ_ROWS, :]
            zs = z + (z_prev - z) * mu_ref[:, cols]
            if cols.start <= gate_cols.start < cols.stop:
                assert gate_cols.stop <= cols.stop
                lo, hi = gate_cols.start - cols.start, gate_cols.stop - cols.start
                pieces = [zs[:, :lo], _silu(zs[:, lo:hi]), zs[:, hi:]]
                zs = jnp.concatenate([p for p in pieces if p.shape[1]], axis=1)
            zr_ref[:, cols] = zs
            rows = slice(c * SGU_CHUNK, (c + 1) * SGU_CHUNK)
            ys_ref[rows, :] = _sgu_chunk(zs_ref[rows, :], lng_ref, lnb_ref, ws_c, bs_ref,
                                         og_ref).astype(ys_ref.dtype)

    @pl.when(tile == n_tiles)
    def _():
        for _ in _rwkv_prepare(zr_ref, par, out, glast_ref):
            pass


def _in_proj(x2, g, w_in, sgu_par, mu, rwkv_par, seq):
    m = x2.shape[0]
    tm = IN_PROJ_ROWS
    n_tiles = m // tm
    chunks = tm // RWKV_CHUNK
    assert (RWKV_PREPARE_PIECES * chunks) % 3 == 0
    const = lambda a: pl.BlockSpec(a.shape, lambda i: (0,) * a.ndim)
    cur = lambda i: jnp.minimum(i, n_tiles - 1)
    last = lambda i: jnp.maximum(i - 1, 0)
    tokens = lambda dt: jax.ShapeDtypeStruct((m, D_RWKV), dt)
    outs = pl.pallas_call(
        functools.partial(_in_proj_kernel, seq // tm, n_tiles),
        grid=(n_tiles + 1,),
        in_specs=[pl.BlockSpec((tm, D_MODEL), lambda i: (cur(i), 0)), const(g), const(w_in)]
        + [const(a) for a in sgu_par] + [const(mu)] + [const(a) for a in rwkv_par],
        out_specs=[pl.BlockSpec((tm, D_SGU), lambda i: (cur(i), 0))]
        + [pl.BlockSpec((tm, D_RWKV), lambda i: (last(i), 0))
           for _ in RWKV_OPERANDS + RWKV_EPILOGUE]
        + [pl.BlockSpec((chunks, 1, D_RWKV), lambda i: (last(i), 0, 0))],
        out_shape=[jax.ShapeDtypeStruct((m, D_SGU), BF16)]
        + [tokens(BF16) for _ in RWKV_OPERANDS] + [tokens(F32) for _ in RWKV_EPILOGUE]
        + [jax.ShapeDtypeStruct((m // RWKV_CHUNK, 1, D_RWKV), F32)],
        scratch_shapes=[
            pltpu.VMEM((tm, C_SGU), F32),
            pltpu.VMEM((tm, C_RWKV), F32),
            pltpu.VMEM((1, C_RWKV), F32),
        ],
        compiler_params=pltpu.CompilerParams(
            dimension_semantics=("arbitrary",), vmem_limit_bytes=VMEM_LIMIT_BYTES),
        name="in_proj",
    )(x2, g, w_in, *sgu_par, mu, *rwkv_par)
    y_sgu, prepared, g_last = outs[0], outs[1:-1], outs[-1]
    return y_sgu, dict(zip(RWKV_OPERANDS + RWKV_EPILOGUE, prepared)), g_last


def _block_diag(x, keep_a, keep_b):
    return jnp.concatenate([x * keep_a, x * keep_b], axis=0)


def _dots_paired(lhs, rhs):
    out = []
    for i in range(0, len(lhs), 2):
        m, n = lhs[i].shape[0], rhs[i].shape[1]
        both = jnp.dot(jnp.concatenate([lhs[i], lhs[i + 1]], axis=0),
                       jnp.concatenate([rhs[i], rhs[i + 1]], axis=1),
                       preferred_element_type=F32)
        out += [both[:m, :n], both[m:, n:]]
    return out


def _dots_paired_tn(lhs, rhs):
    out = []
    for i in range(0, len(lhs), 2):
        m, n = lhs[i].shape[1], rhs[i].shape[1]
        both = _dot_tn(jnp.concatenate([lhs[i], lhs[i + 1]], axis=1),
                       jnp.concatenate([rhs[i], rhs[i + 1]], axis=1))
        out += [both[:m, :n], both[m:, n:]]
    return out


def _rwkv_kernel(at_ref, rt_ref, bt_ref, kt_ref, v_ref, bonus_ref, gate_ref,
                 glast_ref, gng_ref, gnb_ref, o_ref, state_ref):
    C, N = RWKV_CHUNK, RWKV_HEAD_DIM

    @pl.when(pl.program_id(1) == 0)
    def _():
        state_ref[...] = jnp.zeros_like(state_ref)

    row = lax.broadcasted_iota(jnp.int32, (C, PAIR), 0)
    lane = lax.broadcasted_iota(jnp.int32, (C, PAIR), 1)
    col = lane & (N - 1)
    lane_a = lane < N
    incl = col <= row
    strict = col < row
    eye = col == row

    units = [(b, p) for b in range(RWKV_BATCHES) for p in range(RWKV_PAIRS)]
    nu = len(units)
    lanes = lambda u: slice(u[1] * PAIR, (u[1] + 1) * PAIR)
    of = lambda ref, u: ref[u[0], :, lanes(u)]
    as_keep = lambda mask: jnp.where(mask, 1.0, 0.0).astype(BF16)
    keep_a, keep_b = as_keep(lane_a), as_keep(jnp.logical_not(lane_a))
    bd = lambda x: _block_diag(x, keep_a, keep_b)

    s_bk = [_dot_nt(jnp.concatenate([of(at_ref, u), of(rt_ref, u)], axis=0),
                    jnp.concatenate([bd(of(bt_ref, u)), bd(of(kt_ref, u))], axis=0))
            for u in units]
    s_b = [s[:, :PAIR] for s in s_bk]
    s_k = [s[:, PAIR:] for s in s_bk]
    n_mat = [jnp.where(strict, s[:C], 0.0) for s in s_b]
    n_bf = [n.astype(BF16) for n in n_mat]
    a_ak = [jnp.where(strict, s[:C], 0.0).astype(BF16) for s in s_k]
    a_rb = [jnp.where(incl, s[C:], 0.0).astype(BF16) for s in s_b]
    a_rk = [jnp.where(incl, s[C:], 0.0).astype(BF16) for s in s_k]

    lvl = ((row >> 1) == (col >> 1)) & ((row & 1) == 1) & ((col & 1) == 0)
    x = [eye.astype(F32) + jnp.where(lvl, n, 0.0) for n in n_mat]
    s = 2
    while s < C:
        shift = s.bit_length()
        lvl = ((row >> shift) == (col >> shift)) & ((row & s) != 0) & ((col & s) == 0)
        lvl_a, lvl_b = as_keep(lvl & lane_a), as_keep(lvl & jnp.logical_not(lane_a))
        x_bf = [xi.astype(BF16) for xi in x]
        if s < SUBLANES:
            xn = _dots_paired(x_bf, [_block_diag(n, lvl_a, lvl_b) for n in n_bf])
            xnx = _dots_paired([t.astype(BF16) for t in xn], [bd(t) for t in x_bf])
            x = [x[i] + xnx[i] for i in range(nu)]
        else:
            groups = [g for g in range(C // SUBLANES) if (g * SUBLANES) & s]
            rows_of = lambda m, g: m[g * SUBLANES:(g + 1) * SUBLANES]
            moving = [jnp.concatenate([rows_of(xi, g) for g in groups], axis=0).astype(BF16)
                      for xi in x]
            xn = _dots_paired(moving, [_block_diag(n, lvl_a, lvl_b) for n in n_bf])
            xnx = _dots_paired([t.astype(BF16) for t in xn], [bd(t) for t in x_bf])
            x = [jnp.concatenate(
                [rows_of(x[i], g) + rows_of(xnx[i], groups.index(g)) if g in groups
                 else rows_of(x[i], g) for g in range(C // SUBLANES)], axis=0)
                 for i in range(nu)]
        s *= 2

    x_bf = [xi.astype(BF16) for xi in x]
    v_bd = [bd(of(v_ref, u)) for u in units]
    av = [t.astype(BF16) for t in _dots_paired(a_ak, v_bd)]
    p_mat = [t.astype(BF16) for t in _dots_paired(x_bf, [bd(of(at_ref, u)) for u in units])]

    t_old = [of(state_ref, u) for u in units]
    t_bd = [bd(t.astype(BF16)) for t in t_old]
    u_mat = _dots_paired(
        [jnp.concatenate([p_mat[i], x_bf[i]], axis=1) for i in range(nu)],
        [jnp.concatenate([t_bd[i], bd(av[i])], axis=0) for i in range(nu)])
    u_bf = [um.astype(BF16) for um in u_mat]
    y = _dots_paired(
        [jnp.concatenate([of(rt_ref, u), a_rb[i], a_rk[i]], axis=1) for i, u in enumerate(units)],
        [jnp.concatenate([t_bd[i], bd(u_bf[i]), v_bd[i]], axis=0) for i in range(nu)])
    full = _dots_paired_tn(
        [jnp.concatenate([of(bt_ref, u), of(kt_ref, u)], axis=0) for u in units],
        [jnp.concatenate([u_bf[i], of(v_ref, u)], axis=0) for i, u in enumerate(units)])
    eye_full = jnp.concatenate([eye] * RWKV_PAIRS, axis=1)
    g_col = [_head_sum(jnp.where(eye_full, glast_ref[b, 0], 0.0), lane_a)
             for b in range(RWKV_BATCHES)]
    for i, u in enumerate(units):
        state_ref[u[0], :, lanes(u)] = g_col[u[0]][:, lanes(u)] * (
            t_old[i] + jnp.where(lane_a, full[i][:N], full[i][N:]))

    for b in range(RWKV_BATCHES):
        yb = jnp.concatenate(y[b * RWKV_PAIRS:(b + 1) * RWKV_PAIRS], axis=1)
        yc = yb - _head_sum(yb, lane_a) * (1.0 / N)
        var = _head_sum(yc * yc, lane_a) * (1.0 / N)
        yn = yc * lax.rsqrt(var + GN_EPS) * gng_ref[...] + gnb_ref[...]
        o_ref[b] = ((yn + bonus_ref[b]) * gate_ref[b]).astype(o_ref.dtype)


def _rwkv(prepared, g_last, gn_g, gn_b, batch, seq):
    C, NB = RWKV_CHUNK, RWKV_BATCHES
    blk = pl.BlockSpec((NB, C, D_RWKV), lambda g, c: (g, c, 0))
    vec = pl.BlockSpec((1, D_RWKV), lambda g, c: (0, 0))
    arrays = [prepared[name].reshape(batch, seq, D_RWKV) for name in RWKV_OPERANDS + RWKV_EPILOGUE]
    out = pl.pallas_call(
        _rwkv_kernel,
        grid=(batch // NB, seq // C),
        in_specs=[blk for _ in arrays]
        + [pl.BlockSpec((NB, 1, 1, D_RWKV), lambda g, c: (g, c, 0, 0)), vec, vec],
        out_specs=blk,
        out_shape=jax.ShapeDtypeStruct((batch, seq, D_RWKV), BF16),
        scratch_shapes=[
            pltpu.VMEM((NB, RWKV_HEAD_DIM, D_RWKV), F32),
        ],
        compiler_params=pltpu.CompilerParams(
            dimension_semantics=("arbitrary", "arbitrary"), vmem_limit_bytes=VMEM_LIMIT_BYTES),
        name="rwkv",
    )(*arrays, g_last.reshape(batch, seq // C, 1, D_RWKV), gn_g, gn_b)
    return out.reshape(batch * seq, D_RWKV)


def _out_attn_front(x_ref, ys_ref, yr_ref, wout_ref, gx_ref, wq_ref, h1_ref, q_ref):
    y = jnp.concatenate([ys_ref[...], yr_ref[...]], axis=1)
    h1 = x_ref[...] + jnp.dot(y, wout_ref[...], preferred_element_type=F32)
    h1_ref[...] = h1
    yield
    hn = _rms_norm(h1, gx_ref[...]).astype(BF16)
    yield
    q_ref[...] = jnp.dot(hn, wq_ref[...], preferred_element_type=F32).astype(BF16)
    yield


def _out_attn_back(h1_ref, q_ref, k_ref, v_ref, wo_ref, gf_ref, o_ref):
    cols = [slice(h * XATTN_HEAD_DIM, (h + 1) * XATTN_HEAD_DIM) for h in range(XATTN_HEADS)]
    scores = [_dot_nt(q_ref[:, sl], k_ref[:, sl]) for sl in cols]
    yield
    probs = []
    for s in scores:
        e = jnp.exp(s - jnp.max(s, axis=-1, keepdims=True))
        probs.append((e / jnp.sum(e, axis=-1, keepdims=True)).astype(BF16))
    yield
    o = jnp.concatenate(
        [jnp.dot(p, v_ref[:, sl], preferred_element_type=F32).astype(BF16)
         for p, sl in zip(probs, cols)], axis=-1)
    yield
    h2 = h1_ref[...] + jnp.dot(o, wo_ref[...], preferred_element_type=F32)
    yield
    o_ref[...] = _rms_norm(h2, gf_ref[...])
    yield


OUT_ATTN_ORDER = "BFBBFBBF"


def _out_attn_kernel(tiles_per_batch, n_tiles, x_ref, ys_ref, yr_ref, wout_ref, gx_ref, wq_ref,
                     mem_ref, gm_ref, wkv_ref, wo_ref, gf_ref, o_ref,
                     h1_a, q_a, h1_b, q_b, k_scr, v_scr):
    step = pl.program_id(0)

    @pl.when(step == 0)
    def _():
        h1_b[...] = jnp.zeros_like(h1_b)
        q_b[...] = jnp.zeros_like(q_b)

    pair_tiles = KV_BATCHES * tiles_per_batch

    @pl.when((step % pair_tiles == 0) & (step < n_tiles))
    def _():
        half = (step // pair_tiles) % 2
        mn = _rms_norm(mem_ref[...], gm_ref[...]).astype(BF16)
        k = jnp.dot(mn, wkv_ref[:, :D_MODEL], preferred_element_type=F32)
        k = (k * (XATTN_HEAD_DIM ** -0.5)).astype(BF16)
        v = jnp.dot(mn, wkv_ref[:, D_MODEL:], preferred_element_type=F32).astype(BF16)
        for b in range(KV_BATCHES):
            k_scr[KV_BATCHES * half + b] = k[b * MEM_LEN:(b + 1) * MEM_LEN]
            v_scr[KV_BATCHES * half + b] = v[b * MEM_LEN:(b + 1) * MEM_LEN]

    back_batch = jnp.maximum(step - 1, 0) // tiles_per_batch
    back_slot = KV_BATCHES * ((back_batch // KV_BATCHES) % 2) + back_batch % KV_BATCHES
    k_ref, v_ref = k_scr.at[back_slot], v_scr.at[back_slot]

    def run(nxt, cur):
        front = _out_attn_front(x_ref, ys_ref, yr_ref, wout_ref, gx_ref, wq_ref, *nxt)
        back = _out_attn_back(*cur, k_ref, v_ref, wo_ref, gf_ref, o_ref)
        for stage in OUT_ATTN_ORDER:
            next(front if stage == "F" else back)

    @pl.when(step % 2 == 0)
    def _():
        run((h1_a, q_a), (h1_b, q_b))

    @pl.when(step % 2 == 1)
    def _():
        run((h1_b, q_b), (h1_a, q_a))


def _out_attn(x2, y_sgu, y_rwkv, w_out, g_x, w_q, mem2, g_mem, w_kv, w_o, g_f, seq):
    m = x2.shape[0]
    tq = ATTN_ROWS
    per_batch = seq // tq
    n_tiles = m // tq
    front_tile = lambda s: jnp.minimum(s, n_tiles - 1)
    front_blk = lambda n: pl.BlockSpec((tq, n), lambda s: (front_tile(s), 0))
    full = lambda a, b: pl.BlockSpec((a, b), lambda s: (0, 0))
    mem_blk = pl.BlockSpec((KV_BATCHES * MEM_LEN, D_MODEL),
                           lambda s: (front_tile(s) // (KV_BATCHES * per_batch), 0))
    return pl.pallas_call(
        functools.partial(_out_attn_kernel, per_batch, n_tiles),
        grid=(n_tiles + 1,),
        in_specs=[
            front_blk(D_MODEL), front_blk(D_SGU), front_blk(D_RWKV),
            full(D_SGU + D_RWKV, D_MODEL), full(1, D_MODEL), full(D_MODEL, D_MODEL),
            mem_blk, full(1, D_MODEL), full(D_MODEL, 2 * D_MODEL),
            full(D_MODEL, D_MODEL), full(1, D_MODEL),
        ],
        out_specs=pl.BlockSpec((tq, D_MODEL), lambda s: (jnp.maximum(s - 1, 0), 0)),
        out_shape=jax.ShapeDtypeStruct((m, D_MODEL), F32),
        scratch_shapes=2 * [pltpu.VMEM((tq, D_MODEL), F32), pltpu.VMEM((tq, D_MODEL), BF16)]
        + 2 * [pltpu.VMEM((2 * KV_BATCHES, MEM_LEN, D_MODEL), BF16)],
        compiler_params=pltpu.CompilerParams(
            dimension_semantics=("arbitrary",), vmem_limit_bytes=VMEM_LIMIT_BYTES),
        name="out_attn",
    )(x2, y_sgu, y_rwkv, w_out, g_x, w_q, mem2, g_mem, w_kv, w_o, g_f)


def kernel(x, mem, ln_mix_g, w_in, sgu_ln_g, sgu_ln_b, sgu_ws, sgu_bs, sgu_out_g, rw_mu, rw_w0, rw_w2, rw_a0, rw_a2, rw_k_k, rw_k_a, rw_r_k, rw_gn_g, rw_gn_b, w_out, ln_x_g, ln_mem_g, w_q, w_kv, w_o, ln_f_g):
    batch, seq, _ = x.shape
    assert w_in.shape[0] == 1, "the final norm is fused into the only layer's last call"
    row = lambda a: a.reshape(1, -1)
    h = x.reshape(batch * seq, D_MODEL)
    mem2 = mem.reshape(batch * MEM_LEN, D_MODEL)
    bs_full = jnp.repeat(sgu_bs[0].T, SGU_HEAD_DIM, axis=1)
    sgu_par = (row(sgu_ln_g[0]), row(sgu_ln_b[0]), sgu_ws[0], bs_full, row(sgu_out_g[0]))
    rwkv_par = (row(rw_w0[0]), rw_w2[0].astype(BF16), row(rw_a0[0]), rw_a2[0].astype(BF16),
                row(rw_k_k[0]), row(rw_k_a[0]), row(rw_r_k[0]))
    y_sgu, prepared, g_last = _in_proj(h, row(ln_mix_g[0]), w_in[0].astype(BF16), sgu_par,
                                       row(rw_mu[0]), rwkv_par, seq)
    y_rwkv = _rwkv(prepared, g_last, row(rw_gn_g[0]), row(rw_gn_b[0]), batch, seq)
    h = _out_attn(h, y_sgu, y_rwkv, w_out[0].astype(BF16), row(ln_x_g[0]), w_q[0].astype(BF16),
                  mem2, row(ln_mem_g[0]), w_kv[0].astype(BF16), w_o[0].astype(BF16),
                  row(ln_f_g), seq)
    return h.reshape(batch, seq, D_MODEL)
```

```python
import functools
import math

import jax
import jax.numpy as jnp
from jax import lax
from jax.experimental import pallas as pl
from jax.experimental.pallas import tpu as pltpu

D_MODEL = 1024
MEM_LEN = 256
D_SGU = 512
SGU_HEADS = 4
SGU_HEAD_DIM = D_SGU // SGU_HEADS
SGU_CHUNK = 128
D_RWKV = 512
RWKV_HEAD_DIM = 64
RWKV_HEADS = D_RWKV // RWKV_HEAD_DIM
LORA = 64
C_SGU = 3 * D_SGU
C_RWKV = 4 * D_RWKV + 2 * LORA
XATTN_HEADS = 4
XATTN_HEAD_DIM = D_MODEL // XATTN_HEADS
RMS_EPS = 1e-6
LN_EPS = 1e-5
GN_EPS = 64e-5
KK_NORM_FLOOR = 1e-24

RWKV_CHUNK = 64
RWKV_BATCHES = 8
PAIR = 2 * RWKV_HEAD_DIM
RWKV_PAIRS = RWKV_HEADS // 2
IN_PROJ_ROWS = 512
ATTN_ROWS = 512
KV_BATCHES = 2
W_LOAD_STEPS = 4
VMEM_LIMIT_BYTES = 48 * 1024 * 1024

SUBLANES = 8
LOG2_E = math.log2(math.e)

F32 = jnp.float32
BF16 = jnp.bfloat16


def _dot(a, b):
    return jnp.dot(a.astype(BF16), b.astype(BF16), preferred_element_type=F32)


def _dot_nt(a, b):
    return lax.dot_general(a.astype(BF16), b.astype(BF16), (((1,), (1,)), ((), ())),
                           preferred_element_type=F32)


def _dot_tn(a, b):
    return lax.dot_general(a.astype(BF16), b.astype(BF16), (((0,), (0,)), ((), ())),
                           preferred_element_type=F32)


def _rms_norm(x, g):
    return x * lax.rsqrt(jnp.mean(x * x, axis=-1, keepdims=True) + RMS_EPS) * g


def _gelu(x):
    return 0.5 * x * (1.0 + lax.erf(x * (2.0 ** -0.5)))


def _silu(x):
    return x * jax.nn.sigmoid(x)


def _head_sum(x, lane_a):
    outs = []
    for p in range(RWKV_PAIRS):
        xp = x[:, p * PAIR:(p + 1) * PAIR]
        sum_a = jnp.sum(jnp.where(lane_a, xp, 0.0), axis=-1, keepdims=True)
        sum_b = jnp.sum(jnp.where(lane_a, 0.0, xp), axis=-1, keepdims=True)
        outs.append(jnp.where(lane_a, sum_a, sum_b))
    return jnp.concatenate(outs, axis=-1)


RWKV_OPERANDS = ("a_t", "r_t", "b_t", "k_t", "v")
RWKV_EPILOGUE = ("bonus", "gate")
RWKV_PREPARE_PIECES = 3


def _sgu_chunk(z, lng_ref, lnb_ref, ws_c, bs_ref, og_ref):
    u = _gelu(z[:, 0:D_SGU])
    v = _gelu(z[:, D_SGU:2 * D_SGU])
    gate = z[:, 2 * D_SGU:3 * D_SGU]
    vc = v - jnp.mean(v, axis=-1, keepdims=True)
    var = jnp.mean(vc * vc, axis=-1, keepdims=True)
    vn = (vc * lax.rsqrt(var + LN_EPS) * lng_ref[...] + lnb_ref[...]).astype(BF16)
    sv = jnp.concatenate(
        [jnp.dot(ws_c[h], vn[:, h * SGU_HEAD_DIM:(h + 1) * SGU_HEAD_DIM],
                 preferred_element_type=F32) for h in range(SGU_HEADS)], axis=1) + bs_ref[...]
    return _rms_norm(u * sv, og_ref[...]) * _silu(gate)


def _rwkv_prepare(zr_ref, par, out, glast_ref):
    C, N = RWKV_CHUNK, RWKV_HEAD_DIM
    w0_ref, w2_ref, a0_ref, a2_ref, kk_ref, ka_ref, rk_ref = par
    lane_a = lax.broadcasted_iota(jnp.int32, (C, PAIR), 1) < N
    row_c = lax.broadcasted_iota(jnp.int32, (C, C), 0)
    col_c = lax.broadcasted_iota(jnp.int32, (C, C), 1)
    tril = (col_c <= row_c).astype(F32).astype(BF16)
    tril3 = jnp.concatenate([tril, tril, tril], axis=1)

    def one_chunk(c):
        rows = slice(c * C, (c + 1) * C)
        r = zr_ref[rows, 0:D_RWKV]
        k = zr_ref[rows, D_RWKV:2 * D_RWKV]
        v = zr_ref[rows, 2 * D_RWKV:3 * D_RWKV]
        gate_act = zr_ref[rows, 3 * D_RWKV:4 * D_RWKV]
        wd = zr_ref[rows, 4 * D_RWKV:4 * D_RWKV + LORA]
        ad = zr_ref[rows, 4 * D_RWKV + LORA:4 * D_RWKV + 2 * LORA]

        lw = w0_ref[...] + _dot(jnp.tanh(wd), w2_ref[...])
        ld = -(math.exp(-0.5) * LOG2_E) * jax.nn.sigmoid(lw)
        icl = jax.nn.sigmoid(a0_ref[...] + _dot(ad, a2_ref[...]))
        yield
        k2 = k * ((1.0 - ka_ref[...]) + ka_ref[...] * icl)
        kk = k * kk_ref[...]
        kk = kk * lax.rsqrt(jnp.maximum(_head_sum(kk * kk, lane_a), KK_NORM_FLOOR))
        bvec = kk * icl

        p1 = ld.astype(BF16)
        r1 = ld - p1.astype(F32)
        p2 = r1.astype(BF16)
        p3 = (r1 - p2.astype(F32)).astype(BF16)
        cum = jnp.dot(tril3, jnp.concatenate([p1, p2, p3], axis=0), preferred_element_type=F32)
        last = cum[C - 1:C, :]
        yield
        e_neg = jnp.exp2(-cum)
        out["a_t"][rows, :] = (-kk * jnp.exp2(cum - ld)).astype(BF16)
        out["r_t"][rows, :] = (r * jnp.exp2(cum)).astype(BF16)
        out["b_t"][rows, :] = (bvec * e_neg).astype(BF16)
        out["k_t"][rows, :] = (k2 * e_neg).astype(BF16)
        out["v"][rows, :] = v.astype(BF16)
        out["bonus"][rows, :] = _head_sum(r * k2 * rk_ref[...], lane_a) * v
        out["gate"][rows, :] = gate_act
        glast_ref[c] = jnp.exp2(last)
        yield

    chunks = [one_chunk(c) for c in range(IN_PROJ_ROWS // C)]
    for _ in range(RWKV_PREPARE_PIECES):
        for gen in chunks:
            next(gen)
            yield


def _in_proj_kernel(tiles_per_batch, n_tiles, x_ref, g_ref, wf_ref, lng_ref, lnb_ref, ws_ref,
                    bs_ref, og_ref, mu_ref, w0_ref, w2_ref, a0_ref, a2_ref, kk_ref, ka_ref, rk_ref,
                    ys_ref, at_ref, rt_ref, bt_ref, kt_ref, v_ref, bonus_ref,
                    gate_ref, glast_ref, zs_ref, zr_ref, prev_ref, w_ref):
    step = pl.program_id(0)
    tile = step - W_LOAD_STEPS
    par = (w0_ref, w2_ref, a0_ref, a2_ref, kk_ref, ka_ref, rk_ref)
    out = dict(zip(RWKV_OPERANDS + RWKV_EPILOGUE,
                   (at_ref, rt_ref, bt_ref, kt_ref, v_ref, bonus_ref, gate_ref)))

    @pl.when(step == 0)
    def _():
        prev_ref[...] = jnp.zeros_like(prev_ref)
        zr_ref[...] = jnp.zeros_like(zr_ref)

    @pl.when(step < W_LOAD_STEPS)
    def _():
        rows = D_MODEL // W_LOAD_STEPS
        w_ref[pl.ds(pl.multiple_of(step * rows, rows), rows), :] = wf_ref[...].astype(BF16)

    @pl.when((tile >= 0) & (tile < n_tiles))
    def _():
        prepare = _rwkv_prepare(zr_ref, par, out, glast_ref)
        first_tile = tile % tiles_per_batch == 0
        row8 = lax.broadcasted_iota(jnp.int32, (SUBLANES, 1), 0)
        xn = _rms_norm(x_ref[...], g_ref[...]).astype(BF16)
        per_dot = RWKV_PREPARE_PIECES * (IN_PROJ_ROWS // RWKV_CHUNK) // 3
        for j in range(3):
            for _ in range(per_dot):
                next(prepare)
            cols = slice(j * D_SGU, (j + 1) * D_SGU)
            zs_ref[:, cols] = jnp.dot(xn, w_ref[:, cols], preferred_element_type=F32)
        for _ in prepare:
            pass
        row = lax.broadcasted_iota(jnp.int32, (SGU_CHUNK, SGU_CHUNK), 0)
        col = lax.broadcasted_iota(jnp.int32, (SGU_CHUNK, SGU_CHUNK), 1)
        ws_c = [jnp.where(col <= row, ws_ref[h], 0.0).astype(BF16) for h in range(SGU_HEADS)]
        n_chunks = IN_PROJ_ROWS // SGU_CHUNK
        lane_tiles = C_RWKV // PAIR
        bounds = [C_SGU + PAIR * (lane_tiles * c // n_chunks) for c in range(n_chunks + 1)]
        gate_cols = slice(3 * D_RWKV, 4 * D_RWKV)
        for c in range(n_chunks):
            z = jnp.dot(xn, w_ref[:, bounds[c]:bounds[c + 1]], preferred_element_type=F32)
            cols = slice(bounds[c] - C_SGU, bounds[c + 1] - C_SGU)
            prev = jnp.where(first_tile, 0.0, prev_ref[:, cols])
            rolled = pltpu.roll(z, 1, axis=0)
            z_prev = jnp.concatenate(
                [jnp.where(row8 == 0, prev, rolled[:SUBLANES]), rolled[SUBLANES:]], axis=0)
            prev_ref[:, cols] = z[IN_PROJ_ROWS - 1:IN_PROJ_ROWS, :]
            zs = z + (z_prev - z) * mu_ref[:, cols]
            if cols.start <= gate_cols.start < cols.stop:
                assert gate_cols.stop <= cols.stop
                lo, hi = gate_cols.start - cols.start, gate_cols.stop - cols.start
                pieces = [zs[:, :lo], _silu(zs[:, lo:hi]), zs[:, hi:]]
                zs = jnp.concatenate([p for p in pieces if p.shape[1]], axis=1)
            zr_ref[:, cols] = zs
            rows = slice(c * SGU_CHUNK, (c + 1) * SGU_CHUNK)
            ys_ref[rows, :] = _sgu_chunk(zs_ref[rows, :], lng_ref, lnb_ref, ws_c, bs_ref,
                                         og_ref).astype(ys_ref.dtype)

    @pl.when(tile == n_tiles)
    def _():
        for _ in _rwkv_prepare(zr_ref, par, out, glast_ref):
            pass


def _in_proj(x2, g, w_in, sgu_par, mu, rwkv_par, seq):
    m = x2.shape[0]
    tm = IN_PROJ_ROWS
    n_tiles = m // tm
    chunks = tm // RWKV_CHUNK
    assert (RWKV_PREPARE_PIECES * chunks) % 3 == 0
    const = lambda a: pl.BlockSpec(a.shape, lambda i: (0,) * a.ndim)
    cur = lambda i: jnp.clip(i - W_LOAD_STEPS, 0, n_tiles - 1)
    last = lambda i: jnp.clip(i - W_LOAD_STEPS - 1, 0, n_tiles - 1)
    tokens = lambda dt: jax.ShapeDtypeStruct((m, D_RWKV), dt)
    w_rows = pl.BlockSpec((D_MODEL // W_LOAD_STEPS, C_SGU + C_RWKV),
                          lambda i: (jnp.minimum(i, W_LOAD_STEPS - 1), 0))
    outs = pl.pallas_call(
        functools.partial(_in_proj_kernel, seq // tm, n_tiles),
        grid=(W_LOAD_STEPS + n_tiles + 1,),
        in_specs=[pl.BlockSpec((tm, D_MODEL), lambda i: (cur(i), 0)), const(g), w_rows]
        + [const(a) for a in sgu_par] + [const(mu)] + [const(a) for a in rwkv_par],
        out_specs=[pl.BlockSpec((tm, D_SGU), lambda i: (cur(i), 0))]
        + [pl.BlockSpec((tm, D_RWKV), lambda i: (last(i), 0))
           for _ in RWKV_OPERANDS + RWKV_EPILOGUE]
        + [pl.BlockSpec((chunks, 1, D_RWKV), lambda i: (last(i), 0, 0))],
        out_shape=[jax.ShapeDtypeStruct((m, D_SGU), BF16)]
        + [tokens(BF16) for _ in RWKV_OPERANDS] + [tokens(F32) for _ in RWKV_EPILOGUE]
        + [jax.ShapeDtypeStruct((m // RWKV_CHUNK, 1, D_RWKV), F32)],
        scratch_shapes=[
            pltpu.VMEM((tm, C_SGU), F32),
            pltpu.VMEM((tm, C_RWKV), F32),
            pltpu.VMEM((1, C_RWKV), F32),
            pltpu.VMEM((D_MODEL, C_SGU + C_RWKV), BF16),
        ],
        compiler_params=pltpu.CompilerParams(
            dimension_semantics=("arbitrary",), vmem_limit_bytes=VMEM_LIMIT_BYTES),
        name="in_proj",
    )(x2, g, w_in, *sgu_par, mu, *rwkv_par)
    y_sgu, prepared, g_last = outs[0], outs[1:-1], outs[-1]
    return y_sgu, dict(zip(RWKV_OPERANDS + RWKV_EPILOGUE, prepared)), g_last


def _block_diag(x, keep_a, keep_b):
    return jnp.concatenate([x * keep_a, x * keep_b], axis=0)


def _dots_paired(lhs, rhs):
    out = []
    for i in range(0, len(lhs), 2):
        m, n = lhs[i].shape[0], rhs[i].shape[1]
        both = jnp.dot(jnp.concatenate([lhs[i], lhs[i + 1]], axis=0),
                       jnp.concatenate([rhs[i], rhs[i + 1]], axis=1),
                       preferred_element_type=F32)
        out += [both[:m, :n], both[m:, n:]]
    return out


def _dots_paired_tn(lhs, rhs):
    out = []
    for i in range(0, len(lhs), 2):
        m, n = lhs[i].shape[1], rhs[i].shape[1]
        both = _dot_tn(jnp.concatenate([lhs[i], lhs[i + 1]], axis=1),
                       jnp.concatenate([rhs[i], rhs[i + 1]], axis=1))
        out += [both[:m, :n], both[m:, n:]]
    return out


def _rwkv_kernel(at_ref, rt_ref, bt_ref, kt_ref, v_ref, bonus_ref, gate_ref,
                 glast_ref, gng_ref, gnb_ref, o_ref, state_ref):
    C, N = RWKV_CHUNK, RWKV_HEAD_DIM

    @pl.when(pl.program_id(1) == 0)
    def _():
        state_ref[...] = jnp.zeros_like(state_ref)

    row = lax.broadcasted_iota(jnp.int32, (C, PAIR), 0)
    lane = lax.broadcasted_iota(jnp.int32, (C, PAIR), 1)
    col = lane & (N - 1)
    lane_a = lane < N
    incl = col <= row
    strict = col < row
    eye = col == row

    units = [(b, p) for b in range(RWKV_BATCHES) for p in range(RWKV_PAIRS)]
    nu = len(units)
    lanes = lambda u: slice(u[1] * PAIR, (u[1] + 1) * PAIR)
    of = lambda ref, u: ref[u[0], :, lanes(u)]
    as_keep = lambda mask: jnp.where(mask, 1.0, 0.0).astype(BF16)
    keep_a, keep_b = as_keep(lane_a), as_keep(jnp.logical_not(lane_a))
    bd = lambda x: _block_diag(x, keep_a, keep_b)

    s_bk = [_dot_nt(jnp.concatenate([of(at_ref, u), of(rt_ref, u)], axis=0),
                    jnp.concatenate([bd(of(bt_ref, u)), bd(of(kt_ref, u))], axis=0))
            for u in units]
    s_b = [s[:, :PAIR] for s in s_bk]
    s_k = [s[:, PAIR:] for s in s_bk]
    n_mat = [jnp.where(strict, s[:C], 0.0) for s in s_b]
    n_bf = [n.astype(BF16) for n in n_mat]
    a_ak = [jnp.where(strict, s[:C], 0.0).astype(BF16) for s in s_k]
    a_rb = [jnp.where(incl, s[C:], 0.0).astype(BF16) for s in s_b]
    a_rk = [jnp.where(incl, s[C:], 0.0).astype(BF16) for s in s_k]

    lvl = ((row >> 1) == (col >> 1)) & ((row & 1) == 1) & ((col & 1) == 0)
    x = [eye.astype(F32) + jnp.where(lvl, n, 0.0) for n in n_mat]
    s = 2
    while s < C:
        shift = s.bit_length()
        lvl = ((row >> shift) == (col >> shift)) & ((row & s) != 0) & ((col & s) == 0)
        lvl_a, lvl_b = as_keep(lvl & lane_a), as_keep(lvl & jnp.logical_not(lane_a))
        x_bf = [xi.astype(BF16) for xi in x]
        if s < SUBLANES:
            xn = _dots_paired(x_bf, [_block_diag(n, lvl_a, lvl_b) for n in n_bf])
            xnx = _dots_paired([t.astype(BF16) for t in xn], [bd(t) for t in x_bf])
            x = [x[i] + xnx[i] for i in range(nu)]
        else:
            groups = [g for g in range(C // SUBLANES) if (g * SUBLANES) & s]
            rows_of = lambda m, g: m[g * SUBLANES:(g + 1) * SUBLANES]
            moving = [jnp.concatenate([rows_of(xi, g) for g in groups], axis=0).astype(BF16)
                      for xi in x]
            xn = _dots_paired(moving, [_block_diag(n, lvl_a, lvl_b) for n in n_bf])
            xnx = _dots_paired([t.astype(BF16) for t in xn], [bd(t) for t in x_bf])
            x = [jnp.concatenate(
                [rows_of(x[i], g) + rows_of(xnx[i], groups.index(g)) if g in groups
                 else rows_of(x[i], g) for g in range(C // SUBLANES)], axis=0)
                 for i in range(nu)]
        s *= 2

    x_bf = [xi.astype(BF16) for xi in x]
    v_bd = [bd(of(v_ref, u)) for u in units]
    av = [t.astype(BF16) for t in _dots_paired(a_ak, v_bd)]
    p_mat = [t.astype(BF16) for t in _dots_paired(x_bf, [bd(of(at_ref, u)) for u in units])]

    t_old = [of(state_ref, u) for u in units]
    t_bd = [bd(t.astype(BF16)) for t in t_old]
    u_mat = _dots_paired(
        [jnp.concatenate([p_mat[i], x_bf[i]], axis=1) for i in range(nu)],
        [jnp.concatenate([t_bd[i], bd(av[i])], axis=0) for i in range(nu)])
    u_bf = [um.astype(BF16) for um in u_mat]
    y = _dots_paired(
        [jnp.concatenate([of(rt_ref, u), a_rb[i], a_rk[i]], axis=1) for i, u in enumerate(units)],
        [jnp.concatenate([t_bd[i], bd(u_bf[i]), v_bd[i]], axis=0) for i in range(nu)])
    full = _dots_paired_tn(
        [jnp.concatenate([of(bt_ref, u), of(kt_ref, u)], axis=0) for u in units],
        [jnp.concatenate([u_bf[i], of(v_ref, u)], axis=0) for i, u in enumerate(units)])
    eye_full = jnp.concatenate([eye] * RWKV_PAIRS, axis=1)
    g_col = [_head_sum(jnp.where(eye_full, glast_ref[b, 0], 0.0), lane_a)
             for b in range(RWKV_BATCHES)]
    for i, u in enumerate(units):
        state_ref[u[0], :, lanes(u)] = g_col[u[0]][:, lanes(u)] * (
            t_old[i] + jnp.where(lane_a, full[i][:N], full[i][N:]))

    for b in range(RWKV_BATCHES):
        yb = jnp.concatenate(y[b * RWKV_PAIRS:(b + 1) * RWKV_PAIRS], axis=1)
        yc = yb - _head_sum(yb, lane_a) * (1.0 / N)
        var = _head_sum(yc * yc, lane_a) * (1.0 / N)
        yn = yc * lax.rsqrt(var + GN_EPS) * gng_ref[...] + gnb_ref[...]
        o_ref[b] = ((yn + bonus_ref[b]) * gate_ref[b]).astype(o_ref.dtype)


def _rwkv(prepared, g_last, gn_g, gn_b, batch, seq):
    C, NB = RWKV_CHUNK, RWKV_BATCHES
    blk = pl.BlockSpec((NB, C, D_RWKV), lambda g, c: (g, c, 0))
    vec = pl.BlockSpec((1, D_RWKV), lambda g, c: (0, 0))
    arrays = [prepared[name].reshape(batch, seq, D_RWKV) for name in RWKV_OPERANDS + RWKV_EPILOGUE]
    out = pl.pallas_call(
        _rwkv_kernel,
        grid=(batch // NB, seq // C),
        in_specs=[blk for _ in arrays]
        + [pl.BlockSpec((NB, 1, 1, D_RWKV), lambda g, c: (g, c, 0, 0)), vec, vec],
        out_specs=blk,
        out_shape=jax.ShapeDtypeStruct((batch, seq, D_RWKV), BF16),
        scratch_shapes=[
            pltpu.VMEM((NB, RWKV_HEAD_DIM, D_RWKV), F32),
        ],
        compiler_params=pltpu.CompilerParams(
            dimension_semantics=("arbitrary", "arbitrary"), vmem_limit_bytes=VMEM_LIMIT_BYTES),
        name="rwkv",
    )(*arrays, g_last.reshape(batch, seq // C, 1, D_RWKV), gn_g, gn_b)
    return out.reshape(batch * seq, D_RWKV)


def _out_attn_front(x_ref, ys_ref, yr_ref, wout_ref, gx_ref, wq_ref, h1_ref, q_ref):
    y = jnp.concatenate([ys_ref[...], yr_ref[...]], axis=1)
    h1 = x_ref[...] + jnp.dot(y, wout_ref[...], preferred_element_type=F32)
    h1_ref[...] = h1
    yield
    hn = _rms_norm(h1, gx_ref[...]).astype(BF16)
    yield
    q_ref[...] = jnp.dot(hn, wq_ref[...], preferred_element_type=F32).astype(BF16)
    yield


def _out_attn_back(h1_ref, q_ref, k_ref, v_ref, wo_ref, gf_ref, o_ref):
    cols = [slice(h * XATTN_HEAD_DIM, (h + 1) * XATTN_HEAD_DIM) for h in range(XATTN_HEADS)]
    scores = [_dot_nt(q_ref[:, sl], k_ref[:, sl]) for sl in cols]
    yield
    probs = []
    for s in scores:
        e = jnp.exp(s - jnp.max(s, axis=-1, keepdims=True))
        probs.append((e / jnp.sum(e, axis=-1, keepdims=True)).astype(BF16))
    yield
    o = jnp.concatenate(
        [jnp.dot(p, v_ref[:, sl], preferred_element_type=F32).astype(BF16)
         for p, sl in zip(probs, cols)], axis=-1)
    yield
    h2 = h1_ref[...] + jnp.dot(o, wo_ref[...], preferred_element_type=F32)
    yield
    o_ref[...] = _rms_norm(h2, gf_ref[...])
    yield


OUT_ATTN_ORDER = "BFBBFBBF"


def _out_attn_kernel(tiles_per_batch, n_tiles, x_ref, ys_ref, yr_ref, wout_ref, gx_ref, wq_ref,
                     mem_ref, gm_ref, wkv_ref, wo_ref, gf_ref, o_ref,
                     h1_a, q_a, h1_b, q_b, k_scr, v_scr):
    step = pl.program_id(0)

    @pl.when(step == 0)
    def _():
        h1_b[...] = jnp.zeros_like(h1_b)
        q_b[...] = jnp.zeros_like(q_b)

    pair_tiles = KV_BATCHES * tiles_per_batch

    @pl.when((step % pair_tiles == 0) & (step < n_tiles))
    def _():
        half = (step // pair_tiles) % 2
        mn = _rms_norm(mem_ref[...], gm_ref[...]).astype(BF16)
        k = jnp.dot(mn, wkv_ref[:, :D_MODEL], preferred_element_type=F32)
        k = (k * (XATTN_HEAD_DIM ** -0.5)).astype(BF16)
        v = jnp.dot(mn, wkv_ref[:, D_MODEL:], preferred_element_type=F32).astype(BF16)
        for b in range(KV_BATCHES):
            k_scr[KV_BATCHES * half + b] = k[b * MEM_LEN:(b + 1) * MEM_LEN]
            v_scr[KV_BATCHES * half + b] = v[b * MEM_LEN:(b + 1) * MEM_LEN]

    back_batch = jnp.maximum(step - 1, 0) // tiles_per_batch
    back_slot = KV_BATCHES * ((back_batch // KV_BATCHES) % 2) + back_batch % KV_BATCHES
    k_ref, v_ref = k_scr.at[back_slot], v_scr.at[back_slot]

    def run(nxt, cur):
        front = _out_attn_front(x_ref, ys_ref, yr_ref, wout_ref, gx_ref, wq_ref, *nxt)
        back = _out_attn_back(*cur, k_ref, v_ref, wo_ref, gf_ref, o_ref)
        for stage in OUT_ATTN_ORDER:
            next(front if stage == "F" else back)

    @pl.when(step % 2 == 0)
    def _():
        run((h1_a, q_a), (h1_b, q_b))

    @pl.when(step % 2 == 1)
    def _():
        run((h1_b, q_b), (h1_a, q_a))


def _out_attn(x2, y_sgu, y_rwkv, w_out, g_x, w_q, mem2, g_mem, w_kv, w_o, g_f, seq):
    m = x2.shape[0]
    tq = ATTN_ROWS
    per_batch = seq // tq
    n_tiles = m // tq
    front_tile = lambda s: jnp.minimum(s, n_tiles - 1)
    front_blk = lambda n: pl.BlockSpec((tq, n), lambda s: (front_tile(s), 0))
    full = lambda a, b: pl.BlockSpec((a, b), lambda s: (0, 0))
    mem_blk = pl.BlockSpec((KV_BATCHES * MEM_LEN, D_MODEL),
                           lambda s: (front_tile(s) // (KV_BATCHES * per_batch), 0))
    return pl.pallas_call(
        functools.partial(_out_attn_kernel, per_batch, n_tiles),
        grid=(n_tiles + 1,),
        in_specs=[
            front_blk(D_MODEL), front_blk(D_SGU), front_blk(D_RWKV),
            full(D_SGU + D_RWKV, D_MODEL), full(1, D_MODEL), full(D_MODEL, D_MODEL),
            mem_blk, full(1, D_MODEL), full(D_MODEL, 2 * D_MODEL),
            full(D_MODEL, D_MODEL), full(1, D_MODEL),
        ],
        out_specs=pl.BlockSpec((tq, D_MODEL), lambda s: (jnp.maximum(s - 1, 0), 0)),
        out_shape=jax.ShapeDtypeStruct((m, D_MODEL), F32),
        scratch_shapes=2 * [pltpu.VMEM((tq, D_MODEL), F32), pltpu.VMEM((tq, D_MODEL), BF16)]
        + 2 * [pltpu.VMEM((2 * KV_BATCHES, MEM_LEN, D_MODEL), BF16)],
        compiler_params=pltpu.CompilerParams(
            dimension_semantics=("arbitrary",), vmem_limit_bytes=VMEM_LIMIT_BYTES),
        name="out_attn",
    )(x2, y_sgu, y_rwkv, w_out, g_x, w_q, mem2, g_mem, w_kv, w_o, g_f)


def kernel(x, mem, ln_mix_g, w_in, sgu_ln_g, sgu_ln_b, sgu_ws, sgu_bs, sgu_out_g, rw_mu, rw_w0, rw_w2, rw_a0, rw_a2, rw_k_k, rw_k_a, rw_r_k, rw_gn_g, rw_gn_b, w_out, ln_x_g, ln_mem_g, w_q, w_kv, w_o, ln_f_g):
    batch, seq, _ = x.shape
    assert w_in.shape[0] == 1, "the final norm is fused into the only layer's last call"
    row = lambda a: a.reshape(1, -1)
    h = x.reshape(batch * seq, D_MODEL)
    mem2 = mem.reshape(batch * MEM_LEN, D_MODEL)
    bs_full = jnp.repeat(sgu_bs[0].T, SGU_HEAD_DIM, axis=1)
    sgu_par = (row(sgu_ln_g[0]), row(sgu_ln_b[0]), sgu_ws[0], bs_full, row(sgu_out_g[0]))
    rwkv_par = (row(rw_w0[0]), rw_w2[0].astype(BF16), row(rw_a0[0]), rw_a2[0].astype(BF16),
                row(rw_k_k[0]), row(rw_k_a[0]), row(rw_r_k[0]))
    y_sgu, prepared, g_last = _in_proj(h, row(ln_mix_g[0]), w_in[0], sgu_par,
                                       row(rw_mu[0]), rwkv_par, seq)
    y_rwkv = _rwkv(prepared, g_last, row(rw_gn_g[0]), row(rw_gn_b[0]), batch, seq)
    h = _out_attn(h, y_sgu, y_rwkv, w_out[0].astype(BF16), row(ln_x_g[0]), w_q[0].astype(BF16),
                  mem2, row(ln_mem_g[0]), w_kv[0].astype(BF16), w_o[0].astype(BF16),
                  row(ln_f_g), seq)
    return h.reshape(batch, seq, D_MODEL)
```

```python
import functools
import math

import jax
import jax.numpy as jnp
from jax import lax
from jax.experimental import pallas as pl
from jax.experimental.pallas import tpu as pltpu

D_MODEL = 1024
MEM_LEN = 256
D_SGU = 512
SGU_HEADS = 4
SGU_HEAD_DIM = D_SGU // SGU_HEADS
SGU_CHUNK = 128
D_RWKV = 512
RWKV_HEAD_DIM = 64
RWKV_HEADS = D_RWKV // RWKV_HEAD_DIM
LORA = 64
C_SGU = 3 * D_SGU
C_RWKV = 4 * D_RWKV + 2 * LORA
XATTN_HEADS = 4
XATTN_HEAD_DIM = D_MODEL // XATTN_HEADS
RMS_EPS = 1e-6
LN_EPS = 1e-5
GN_EPS = 64e-5
KK_NORM_FLOOR = 1e-24

RWKV_CHUNK = 64
RWKV_BATCHES = 8
PAIR = 2 * RWKV_HEAD_DIM
RWKV_PAIRS = RWKV_HEADS // 2
IN_PROJ_ROWS = 512
ATTN_ROWS = 512
KV_BATCHES = 2
VMEM_LIMIT_BYTES = 48 * 1024 * 1024

SUBLANES = 8
LOG2_E = math.log2(math.e)

F32 = jnp.float32
BF16 = jnp.bfloat16


def _dot(a, b):
    return jnp.dot(a.astype(BF16), b.astype(BF16), preferred_element_type=F32)


def _dot_nt(a, b):
    return lax.dot_general(a.astype(BF16), b.astype(BF16), (((1,), (1,)), ((), ())),
                           preferred_element_type=F32)


def _dot_tn(a, b):
    return lax.dot_general(a.astype(BF16), b.astype(BF16), (((0,), (0,)), ((), ())),
                           preferred_element_type=F32)


def _rms_norm(x, g):
    return x * lax.rsqrt(jnp.mean(x * x, axis=-1, keepdims=True) + RMS_EPS) * g


def _gelu(x):
    return 0.5 * x * (1.0 + lax.erf(x * (2.0 ** -0.5)))


def _silu(x):
    return x * jax.nn.sigmoid(x)


def _head_sum(x, lane_a):
    outs = []
    for p in range(RWKV_PAIRS):
        xp = x[:, p * PAIR:(p + 1) * PAIR]
        sum_a = jnp.sum(jnp.where(lane_a, xp, 0.0), axis=-1, keepdims=True)
        sum_b = jnp.sum(jnp.where(lane_a, 0.0, xp), axis=-1, keepdims=True)
        outs.append(jnp.where(lane_a, sum_a, sum_b))
    return jnp.concatenate(outs, axis=-1)


RWKV_OPERANDS = ("a_t", "r_t", "b_t", "k_t", "v")
RWKV_EPILOGUE = ("bonus", "gate")
RWKV_PREPARE_PIECES = 3
RWKV_PREPARE_SPLIT = (3, 13, 8)


def _sgu_chunk(z, lng_ref, lnb_ref, ws_c, bs_ref, og_ref):
    u = _gelu(z[:, 0:D_SGU])
    v = _gelu(z[:, D_SGU:2 * D_SGU])
    gate = z[:, 2 * D_SGU:3 * D_SGU]
    vc = v - jnp.mean(v, axis=-1, keepdims=True)
    var = jnp.mean(vc * vc, axis=-1, keepdims=True)
    vn = (vc * lax.rsqrt(var + LN_EPS) * lng_ref[...] + lnb_ref[...]).astype(BF16)
    sv = jnp.concatenate(
        [jnp.dot(ws_c[h], vn[:, h * SGU_HEAD_DIM:(h + 1) * SGU_HEAD_DIM],
                 preferred_element_type=F32) for h in range(SGU_HEADS)], axis=1) + bs_ref[...]
    return _rms_norm(u * sv, og_ref[...]) * _silu(gate)


def _rwkv_prepare(zr_ref, par, out, glast_ref):
    C, N = RWKV_CHUNK, RWKV_HEAD_DIM
    w0_ref, w2_ref, a0_ref, a2_ref, kk_ref, ka_ref, rk_ref = par
    lane_a = lax.broadcasted_iota(jnp.int32, (C, PAIR), 1) < N
    row_c = lax.broadcasted_iota(jnp.int32, (C, C), 0)
    col_c = lax.broadcasted_iota(jnp.int32, (C, C), 1)
    tril = (col_c <= row_c).astype(F32).astype(BF16)
    tril3 = jnp.concatenate([tril, tril, tril], axis=1)

    def one_chunk(c):
        rows = slice(c * C, (c + 1) * C)
        r = zr_ref[rows, 0:D_RWKV]
        k = zr_ref[rows, D_RWKV:2 * D_RWKV]
        v = zr_ref[rows, 2 * D_RWKV:3 * D_RWKV]
        gate_act = zr_ref[rows, 3 * D_RWKV:4 * D_RWKV]
        wd = zr_ref[rows, 4 * D_RWKV:4 * D_RWKV + LORA]
        ad = zr_ref[rows, 4 * D_RWKV + LORA:4 * D_RWKV + 2 * LORA]

        lw_lora = _dot(jnp.tanh(wd), w2_ref[...])
        icl_lora = _dot(ad, a2_ref[...])
        yield
        lw = w0_ref[...] + lw_lora
        ld = -(math.exp(-0.5) * LOG2_E) * jax.nn.sigmoid(lw)
        icl = jax.nn.sigmoid(a0_ref[...] + icl_lora)
        p1 = ld.astype(BF16)
        r1 = ld - p1.astype(F32)
        p2 = r1.astype(BF16)
        p3 = (r1 - p2.astype(F32)).astype(BF16)
        cum = jnp.dot(tril3, jnp.concatenate([p1, p2, p3], axis=0), preferred_element_type=F32)
        last = cum[C - 1:C, :]
        yield
        k2 = k * ((1.0 - ka_ref[...]) + ka_ref[...] * icl)
        kk = k * kk_ref[...]
        kk = kk * lax.rsqrt(jnp.maximum(_head_sum(kk * kk, lane_a), KK_NORM_FLOOR))
        bvec = kk * icl
        e_neg = jnp.exp2(-cum)
        out["a_t"][rows, :] = (-kk * jnp.exp2(cum - ld)).astype(BF16)
        out["r_t"][rows, :] = (r * jnp.exp2(cum)).astype(BF16)
        out["b_t"][rows, :] = (bvec * e_neg).astype(BF16)
        out["k_t"][rows, :] = (k2 * e_neg).astype(BF16)
        out["v"][rows, :] = v.astype(BF16)
        out["bonus"][rows, :] = _head_sum(r * k2 * rk_ref[...], lane_a) * v
        out["gate"][rows, :] = gate_act
        glast_ref[c] = jnp.exp2(last)
        yield

    chunks = [one_chunk(c) for c in range(IN_PROJ_ROWS // C)]
    for _ in range(RWKV_PREPARE_PIECES):
        for gen in chunks:
            next(gen)
            yield


def _in_proj_kernel(tiles_per_batch, n_tiles, x_ref, g_ref, w_ref, lng_ref, lnb_ref, ws_ref,
                    bs_ref, og_ref, mu_ref, w0_ref, w2_ref, a0_ref, a2_ref, kk_ref, ka_ref, rk_ref,
                    ys_ref, at_ref, rt_ref, bt_ref, kt_ref, v_ref, bonus_ref,
                    gate_ref, glast_ref, zs_ref, zr_ref, prev_ref):
    tile = pl.program_id(0)
    par = (w0_ref, w2_ref, a0_ref, a2_ref, kk_ref, ka_ref, rk_ref)
    out = dict(zip(RWKV_OPERANDS + RWKV_EPILOGUE,
                   (at_ref, rt_ref, bt_ref, kt_ref, v_ref, bonus_ref, gate_ref)))

    @pl.when(tile == 0)
    def _():
        prev_ref[...] = jnp.zeros_like(prev_ref)
        zr_ref[...] = jnp.zeros_like(zr_ref)

    @pl.when(tile < n_tiles)
    def _():
        prepare = _rwkv_prepare(zr_ref, par, out, glast_ref)
        first_tile = tile % tiles_per_batch == 0
        row8 = lax.broadcasted_iota(jnp.int32, (SUBLANES, 1), 0)
        xn = _rms_norm(x_ref[...], g_ref[...]).astype(BF16)
        for j in range(3):
            for _ in range(RWKV_PREPARE_SPLIT[j]):
                next(prepare)
            cols = slice(j * D_SGU, (j + 1) * D_SGU)
            zs_ref[:, cols] = jnp.dot(xn, w_ref[:, cols], preferred_element_type=F32)
        for _ in prepare:
            pass
        row = lax.broadcasted_iota(jnp.int32, (SGU_CHUNK, SGU_CHUNK), 0)
        col = lax.broadcasted_iota(jnp.int32, (SGU_CHUNK, SGU_CHUNK), 1)
        ws_c = [jnp.where(col <= row, ws_ref[h], 0.0).astype(BF16) for h in range(SGU_HEADS)]
        n_chunks = IN_PROJ_ROWS // SGU_CHUNK
        lane_tiles = C_RWKV // PAIR
        bounds = [C_SGU + PAIR * (lane_tiles * c // n_chunks) for c in range(n_chunks + 1)]
        gate_cols = slice(3 * D_RWKV, 4 * D_RWKV)
        for c in range(n_chunks):
            z = jnp.dot(xn, w_ref[:, bounds[c]:bounds[c + 1]], preferred_element_type=F32)
            cols = slice(bounds[c] - C_SGU, bounds[c + 1] - C_SGU)
            prev = jnp.where(first_tile, 0.0, prev_ref[:, cols])
            rolled = pltpu.roll(z, 1, axis=0)
            z_prev = jnp.concatenate(
                [jnp.where(row8 == 0, prev, rolled[:SUBLANES]), rolled[SUBLANES:]], axis=0)
            prev_ref[:, cols] = z[IN_PROJ_ROWS - 1:IN_PROJ_ROWS, :]
            zs = z + (z_prev - z) * mu_ref[:, cols]
            if cols.start <= gate_cols.start < cols.stop:
                assert gate_cols.stop <= cols.stop
                lo, hi = gate_cols.start - cols.start, gate_cols.stop - cols.start
                pieces = [zs[:, :lo], _silu(zs[:, lo:hi]), zs[:, hi:]]
                zs = jnp.concatenate([p for p in pieces if p.shape[1]], axis=1)
            zr_ref[:, cols] = zs
            rows = slice(c * SGU_CHUNK, (c + 1) * SGU_CHUNK)
            ys_ref[rows, :] = _sgu_chunk(zs_ref[rows, :], lng_ref, lnb_ref, ws_c, bs_ref,
                                         og_ref).astype(ys_ref.dtype)

    @pl.when(tile == n_tiles)
    def _():
        for _ in _rwkv_prepare(zr_ref, par, out, glast_ref):
            pass


def _in_proj(x2, g, w_in, sgu_par, mu, rwkv_par, seq):
    m = x2.shape[0]
    tm = IN_PROJ_ROWS
    n_tiles = m // tm
    chunks = tm // RWKV_CHUNK
    assert sum(RWKV_PREPARE_SPLIT) == RWKV_PREPARE_PIECES * chunks
    const = lambda a: pl.BlockSpec(a.shape, lambda i: (0,) * a.ndim)
    cur = lambda i: jnp.minimum(i, n_tiles - 1)
    last = lambda i: jnp.maximum(i - 1, 0)
    tokens = lambda dt: jax.ShapeDtypeStruct((m, D_RWKV), dt)
    outs = pl.pallas_call(
        functools.partial(_in_proj_kernel, seq // tm, n_tiles),
        grid=(n_tiles + 1,),
        in_specs=[pl.BlockSpec((tm, D_MODEL), lambda i: (cur(i), 0)), const(g), const(w_in)]
        + [const(a) for a in sgu_par] + [const(mu)] + [const(a) for a in rwkv_par],
        out_specs=[pl.BlockSpec((tm, D_SGU), lambda i: (cur(i), 0))]
        + [pl.BlockSpec((tm, D_RWKV), lambda i: (last(i), 0))
           for _ in RWKV_OPERANDS + RWKV_EPILOGUE]
        + [pl.BlockSpec((chunks, 1, D_RWKV), lambda i: (last(i), 0, 0))],
        out_shape=[jax.ShapeDtypeStruct((m, D_SGU), BF16)]
        + [tokens(BF16) for _ in RWKV_OPERANDS] + [tokens(F32) for _ in RWKV_EPILOGUE]
        + [jax.ShapeDtypeStruct((m // RWKV_CHUNK, 1, D_RWKV), F32)],
        scratch_shapes=[
            pltpu.VMEM((tm, C_SGU), F32),
            pltpu.VMEM((tm, C_RWKV), F32),
            pltpu.VMEM((1, C_RWKV), F32),
        ],
        compiler_params=pltpu.CompilerParams(
            dimension_semantics=("arbitrary",), vmem_limit_bytes=VMEM_LIMIT_BYTES),
        name="in_proj",
    )(x2, g, w_in, *sgu_par, mu, *rwkv_par)
    y_sgu, prepared, g_last = outs[0], outs[1:-1], outs[-1]
    return y_sgu, dict(zip(RWKV_OPERANDS + RWKV_EPILOGUE, prepared)), g_last


def _block_diag(x, keep_a, keep_b):
    return jnp.concatenate([x * keep_a, x * keep_b], axis=0)


def _dots_paired(lhs, rhs):
    out = []
    for i in range(0, len(lhs), 2):
        m, n = lhs[i].shape[0], rhs[i].shape[1]
        both = jnp.dot(jnp.concatenate([lhs[i], lhs[i + 1]], axis=0),
                       jnp.concatenate([rhs[i], rhs[i + 1]], axis=1),
                       preferred_element_type=F32)
        out += [both[:m, :n], both[m:, n:]]
    return out


def _dots_paired_tn(lhs, rhs):
    out = []
    for i in range(0, len(lhs), 2):
        m, n = lhs[i].shape[1], rhs[i].shape[1]
        both = _dot_tn(jnp.concatenate([lhs[i], lhs[i + 1]], axis=1),
                       jnp.concatenate([rhs[i], rhs[i + 1]], axis=1))
        out += [both[:m, :n], both[m:, n:]]
    return out


def _rwkv_kernel(at_ref, rt_ref, bt_ref, kt_ref, v_ref, bonus_ref, gate_ref,
                 glast_ref, gng_ref, gnb_ref, o_ref, state_ref):
    C, N = RWKV_CHUNK, RWKV_HEAD_DIM

    @pl.when(pl.program_id(1) == 0)
    def _():
        state_ref[...] = jnp.zeros_like(state_ref)

    row = lax.broadcasted_iota(jnp.int32, (C, PAIR), 0)
    lane = lax.broadcasted_iota(jnp.int32, (C, PAIR), 1)
    col = lane & (N - 1)
    lane_a = lane < N
    incl = col <= row
    strict = col < row
    eye = col == row

    units = [(b, p) for b in range(RWKV_BATCHES) for p in range(RWKV_PAIRS)]
    nu = len(units)
    lanes = lambda u: slice(u[1] * PAIR, (u[1] + 1) * PAIR)
    of = lambda ref, u: ref[u[0], :, lanes(u)]
    as_keep = lambda mask: jnp.where(mask, 1.0, 0.0).astype(BF16)
    keep_a, keep_b = as_keep(lane_a), as_keep(jnp.logical_not(lane_a))
    bd = lambda x: _block_diag(x, keep_a, keep_b)

    s_bk = [_dot_nt(jnp.concatenate([of(at_ref, u), of(rt_ref, u)], axis=0),
                    jnp.concatenate([bd(of(bt_ref, u)), bd(of(kt_ref, u))], axis=0))
            for u in units]
    s_b = [s[:, :PAIR] for s in s_bk]
    s_k = [s[:, PAIR:] for s in s_bk]
    n_mat = [jnp.where(strict, s[:C], 0.0) for s in s_b]
    n_bf = [n.astype(BF16) for n in n_mat]
    a_ak = [jnp.where(strict, s[:C], 0.0).astype(BF16) for s in s_k]
    a_rb = [jnp.where(incl, s[C:], 0.0).astype(BF16) for s in s_b]
    a_rk = [jnp.where(incl, s[C:], 0.0).astype(BF16) for s in s_k]

    lvl = ((row >> 1) == (col >> 1)) & ((row & 1) == 1) & ((col & 1) == 0)
    x = [eye.astype(F32) + jnp.where(lvl, n, 0.0) for n in n_mat]
    s = 2
    while s < C:
        shift = s.bit_length()
        lvl = ((row >> shift) == (col >> shift)) & ((row & s) != 0) & ((col & s) == 0)
        lvl_a, lvl_b = as_keep(lvl & lane_a), as_keep(lvl & jnp.logical_not(lane_a))
        x_bf = [xi.astype(BF16) for xi in x]
        if s < SUBLANES:
            xn = _dots_paired(x_bf, [_block_diag(n, lvl_a, lvl_b) for n in n_bf])
            xnx = _dots_paired([t.astype(BF16) for t in xn], [bd(t) for t in x_bf])
            x = [x[i] + xnx[i] for i in range(nu)]
        else:
            groups = [g for g in range(C // SUBLANES) if (g * SUBLANES) & s]
            rows_of = lambda m, g: m[g * SUBLANES:(g + 1) * SUBLANES]
            moving = [jnp.concatenate([rows_of(xi, g) for g in groups], axis=0).astype(BF16)
                      for xi in x]
            xn = _dots_paired(moving, [_block_diag(n, lvl_a, lvl_b) for n in n_bf])
            xnx = _dots_paired([t.astype(BF16) for t in xn], [bd(t) for t in x_bf])
            x = [jnp.concatenate(
                [rows_of(x[i], g) + rows_of(xnx[i], groups.index(g)) if g in groups
                 else rows_of(x[i], g) for g in range(C // SUBLANES)], axis=0)
                 for i in range(nu)]
        s *= 2

    x_bf = [xi.astype(BF16) for xi in x]
    v_bd = [bd(of(v_ref, u)) for u in units]
    av = [t.astype(BF16) for t in _dots_paired(a_ak, v_bd)]
    p_mat = [t.astype(BF16) for t in _dots_paired(x_bf, [bd(of(at_ref, u)) for u in units])]

    t_old = [of(state_ref, u) for u in units]
    t_bd = [bd(t.astype(BF16)) for t in t_old]
    u_mat = _dots_paired(
        [jnp.concatenate([p_mat[i], x_bf[i]], axis=1) for i in range(nu)],
        [jnp.concatenate([t_bd[i], bd(av[i])], axis=0) for i in range(nu)])
    u_bf = [um.astype(BF16) for um in u_mat]
    y = _dots_paired(
        [jnp.concatenate([of(rt_ref, u), a_rb[i], a_rk[i]], axis=1) for i, u in enumerate(units)],
        [jnp.concatenate([t_bd[i], bd(u_bf[i]), v_bd[i]], axis=0) for i in range(nu)])
    full = _dots_paired_tn(
        [jnp.concatenate([of(bt_ref, u), of(kt_ref, u)], axis=0) for u in units],
        [jnp.concatenate([u_bf[i], of(v_ref, u)], axis=0) for i, u in enumerate(units)])
    eye_full = jnp.concatenate([eye] * RWKV_PAIRS, axis=1)
    g_col = [_head_sum(jnp.where(eye_full, glast_ref[b, 0], 0.0), lane_a)
             for b in range(RWKV_BATCHES)]
    for i, u in enumerate(units):
        state_ref[u[0], :, lanes(u)] = g_col[u[0]][:, lanes(u)] * (
            t_old[i] + jnp.where(lane_a, full[i][:N], full[i][N:]))

    for b in range(RWKV_BATCHES):
        yb = jnp.concatenate(y[b * RWKV_PAIRS:(b + 1) * RWKV_PAIRS], axis=1)
        yc = yb - _head_sum(yb, lane_a) * (1.0 / N)
        var = _head_sum(yc * yc, lane_a) * (1.0 / N)
        yn = yc * lax.rsqrt(var + GN_EPS) * gng_ref[...] + gnb_ref[...]
        o_ref[b] = ((yn + bonus_ref[b]) * gate_ref[b]).astype(o_ref.dtype)


def _rwkv(prepared, g_last, gn_g, gn_b, batch, seq):
    C, NB = RWKV_CHUNK, RWKV_BATCHES
    blk = pl.BlockSpec((NB, C, D_RWKV), lambda g, c: (g, c, 0))
    vec = pl.BlockSpec((1, D_RWKV), lambda g, c: (0, 0))
    arrays = [prepared[name].reshape(batch, seq, D_RWKV) for name in RWKV_OPERANDS + RWKV_EPILOGUE]
    out = pl.pallas_call(
        _rwkv_kernel,
        grid=(batch // NB, seq // C),
        in_specs=[blk for _ in arrays]
        + [pl.BlockSpec((NB, 1, 1, D_RWKV), lambda g, c: (g, c, 0, 0)), vec, vec],
        out_specs=blk,
        out_shape=jax.ShapeDtypeStruct((batch, seq, D_RWKV), BF16),
        scratch_shapes=[
            pltpu.VMEM((NB, RWKV_HEAD_DIM, D_RWKV), F32),
        ],
        compiler_params=pltpu.CompilerParams(
            dimension_semantics=("arbitrary", "arbitrary"), vmem_limit_bytes=VMEM_LIMIT_BYTES),
        name="rwkv",
    )(*arrays, g_last.reshape(batch, seq // C, 1, D_RWKV), gn_g, gn_b)
    return out.reshape(batch * seq, D_RWKV)


def _out_attn_front(x_ref, ys_ref, yr_ref, wout_ref, gx_ref, wq_ref, h1_ref, q_ref):
    y = jnp.concatenate([ys_ref[...], yr_ref[...]], axis=1)
    h1 = x_ref[...] + jnp.dot(y, wout_ref[...], preferred_element_type=F32)
    h1_ref[...] = h1
    yield
    hn = _rms_norm(h1, gx_ref[...]).astype(BF16)
    yield
    q_ref[...] = jnp.dot(hn, wq_ref[...], preferred_element_type=F32).astype(BF16)
    yield


def _out_attn_back(h1_ref, q_ref, k_ref, v_ref, wo_ref, gf_ref, o_ref):
    cols = [slice(h * XATTN_HEAD_DIM, (h + 1) * XATTN_HEAD_DIM) for h in range(XATTN_HEADS)]
    scores = [_dot_nt(q_ref[:, sl], k_ref[:, sl]) for sl in cols]
    yield
    probs = []
    for s in scores:
        e = jnp.exp(s - jnp.max(s, axis=-1, keepdims=True))
        probs.append((e / jnp.sum(e, axis=-1, keepdims=True)).astype(BF16))
    yield
    o = jnp.concatenate(
        [jnp.dot(p, v_ref[:, sl], preferred_element_type=F32).astype(BF16)
         for p, sl in zip(probs, cols)], axis=-1)
    yield
    h2 = h1_ref[...] + jnp.dot(o, wo_ref[...], preferred_element_type=F32)
    yield
    o_ref[...] = _rms_norm(h2, gf_ref[...])
    yield


OUT_ATTN_ORDER = "BFBBFBBF"


def _out_attn_kernel(tiles_per_batch, n_tiles, x_ref, ys_ref, yr_ref, wout_ref, gx_ref, wq_ref,
                     mem_ref, gm_ref, wkv_ref, wo_ref, gf_ref, o_ref,
                     h1_a, q_a, h1_b, q_b, k_scr, v_scr):
    step = pl.program_id(0)

    @pl.when(step == 0)
    def _():
        h1_b[...] = jnp.zeros_like(h1_b)
        q_b[...] = jnp.zeros_like(q_b)

    pair_tiles = KV_BATCHES * tiles_per_batch

    @pl.when((step % pair_tiles == 0) & (step < n_tiles))
    def _():
        half = (step // pair_tiles) % 2
        mn = _rms_norm(mem_ref[...], gm_ref[...]).astype(BF16)
        k = jnp.dot(mn, wkv_ref[:, :D_MODEL], preferred_element_type=F32)
        k = (k * (XATTN_HEAD_DIM ** -0.5)).astype(BF16)
        v = jnp.dot(mn, wkv_ref[:, D_MODEL:], preferred_element_type=F32).astype(BF16)
        for b in range(KV_BATCHES):
            k_scr[KV_BATCHES * half + b] = k[b * MEM_LEN:(b + 1) * MEM_LEN]
            v_scr[KV_BATCHES * half + b] = v[b * MEM_LEN:(b + 1) * MEM_LEN]

    back_batch = jnp.maximum(step - 1, 0) // tiles_per_batch
    back_slot = KV_BATCHES * ((back_batch // KV_BATCHES) % 2) + back_batch % KV_BATCHES
    k_ref, v_ref = k_scr.at[back_slot], v_scr.at[back_slot]

    def run(nxt, cur):
        front = _out_attn_front(x_ref, ys_ref, yr_ref, wout_ref, gx_ref, wq_ref, *nxt)
        back = _out_attn_back(*cur, k_ref, v_ref, wo_ref, gf_ref, o_ref)
        for stage in OUT_ATTN_ORDER:
            next(front if stage == "F" else back)

    @pl.when(step % 2 == 0)
    def _():
        run((h1_a, q_a), (h1_b, q_b))

    @pl.when(step % 2 == 1)
    def _():
        run((h1_b, q_b), (h1_a, q_a))


def _out_attn(x2, y_sgu, y_rwkv, w_out, g_x, w_q, mem2, g_mem, w_kv, w_o, g_f, seq):
    m = x2.shape[0]
    tq = ATTN_ROWS
    per_batch = seq // tq
    n_tiles = m // tq
    front_tile = lambda s: jnp.minimum(s, n_tiles - 1)
    front_blk = lambda n: pl.BlockSpec((tq, n), lambda s: (front_tile(s), 0))
    full = lambda a, b: pl.BlockSpec((a, b), lambda s: (0, 0))
    mem_blk = pl.BlockSpec((KV_BATCHES * MEM_LEN, D_MODEL),
                           lambda s: (front_tile(s) // (KV_BATCHES * per_batch), 0))
    return pl.pallas_call(
        functools.partial(_out_attn_kernel, per_batch, n_tiles),
        grid=(n_tiles + 1,),
        in_specs=[
            front_blk(D_MODEL), front_blk(D_SGU), front_blk(D_RWKV),
            full(D_SGU + D_RWKV, D_MODEL), full(1, D_MODEL), full(D_MODEL, D_MODEL),
            mem_blk, full(1, D_MODEL), full(D_MODEL, 2 * D_MODEL),
            full(D_MODEL, D_MODEL), full(1, D_MODEL),
        ],
        out_specs=pl.BlockSpec((tq, D_MODEL), lambda s: (jnp.maximum(s - 1, 0), 0)),
        out_shape=jax.ShapeDtypeStruct((m, D_MODEL), F32),
        scratch_shapes=2 * [pltpu.VMEM((tq, D_MODEL), F32), pltpu.VMEM((tq, D_MODEL), BF16)]
        + 2 * [pltpu.VMEM((2 * KV_BATCHES, MEM_LEN, D_MODEL), BF16)],
        compiler_params=pltpu.CompilerParams(
            dimension_semantics=("arbitrary",), vmem_limit_bytes=VMEM_LIMIT_BYTES),
        name="out_attn",
    )(x2, y_sgu, y_rwkv, w_out, g_x, w_q, mem2, g_mem, w_kv, w_o, g_f)


def kernel(x, mem, ln_mix_g, w_in, sgu_ln_g, sgu_ln_b, sgu_ws, sgu_bs, sgu_out_g, rw_mu, rw_w0, rw_w2, rw_a0, rw_a2, rw_k_k, rw_k_a, rw_r_k, rw_gn_g, rw_gn_b, w_out, ln_x_g, ln_mem_g, w_q, w_kv, w_o, ln_f_g):
    batch, seq, _ = x.shape
    assert w_in.shape[0] == 1, "the final norm is fused into the only layer's last call"
    row = lambda a: a.reshape(1, -1)
    h = x.reshape(batch * seq, D_MODEL)
    mem2 = mem.reshape(batch * MEM_LEN, D_MODEL)
    bs_full = jnp.repeat(sgu_bs[0].T, SGU_HEAD_DIM, axis=1)
    sgu_par = (row(sgu_ln_g[0]), row(sgu_ln_b[0]), sgu_ws[0], bs_full, row(sgu_out_g[0]))
    rwkv_par = (row(rw_w0[0]), rw_w2[0].astype(BF16), row(rw_a0[0]), rw_a2[0].astype(BF16),
                row(rw_k_k[0]), row(rw_k_a[0]), row(rw_r_k[0]))
    y_sgu, prepared, g_last = _in_proj(h, row(ln_mix_g[0]), w_in[0].astype(BF16), sgu_par,
                                       row(rw_mu[0]), rwkv_par, seq)
    y_rwkv = _rwkv(prepared, g_last, row(rw_gn_g[0]), row(rw_gn_b[0]), batch, seq)
    h = _out_attn(h, y_sgu, y_rwkv, w_out[0].astype(BF16), row(ln_x_g[0]), w_q[0].astype(BF16),
                  mem2, row(ln_mem_g[0]), w_kv[0].astype(BF16), w_o[0].astype(BF16),
                  row(ln_f_g), seq)
    return h.reshape(batch, seq, D_MODEL)
```

```python
import functools
import math

import jax
import jax.numpy as jnp
from jax import lax
from jax.experimental import pallas as pl
from jax.experimental.pallas import tpu as pltpu

D_MODEL = 1024
MEM_LEN = 256
D_SGU = 512
SGU_HEADS = 4
SGU_HEAD_DIM = D_SGU // SGU_HEADS
SGU_CHUNK = 128
D_RWKV = 512
RWKV_HEAD_DIM = 64
RWKV_HEADS = D_RWKV // RWKV_HEAD_DIM
LORA = 64
C_SGU = 3 * D_SGU
C_RWKV = 4 * D_RWKV + 2 * LORA
XATTN_HEADS = 4
XATTN_HEAD_DIM = D_MODEL // XATTN_HEADS
RMS_EPS = 1e-6
LN_EPS = 1e-5
GN_EPS = 64e-5
KK_NORM_FLOOR = 1e-24

RWKV_CHUNK = 64
RWKV_BATCHES = 8
PAIR = 2 * RWKV_HEAD_DIM
RWKV_PAIRS = RWKV_HEADS // 2
IN_PROJ_ROWS = 512
ATTN_ROWS = 512
KV_BATCHES = 2
VMEM_LIMIT_BYTES = 48 * 1024 * 1024

SUBLANES = 8
LOG2_E = math.log2(math.e)

F32 = jnp.float32
BF16 = jnp.bfloat16


def _dot(a, b):
    return jnp.dot(a.astype(BF16), b.astype(BF16), preferred_element_type=F32)


def _dot_nt(a, b):
    return lax.dot_general(a.astype(BF16), b.astype(BF16), (((1,), (1,)), ((), ())),
                           preferred_element_type=F32)


def _dot_tn(a, b):
    return lax.dot_general(a.astype(BF16), b.astype(BF16), (((0,), (0,)), ((), ())),
                           preferred_element_type=F32)


def _rms_norm(x, g):
    return x * lax.rsqrt(jnp.mean(x * x, axis=-1, keepdims=True) + RMS_EPS) * g


def _gelu(x):
    return 0.5 * x * (1.0 + lax.erf(x * (2.0 ** -0.5)))


def _silu(x):
    return x * jax.nn.sigmoid(x)


def _head_sum(x, lane_a):
    outs = []
    for p in range(RWKV_PAIRS):
        xp = x[:, p * PAIR:(p + 1) * PAIR]
        sum_a = jnp.sum(jnp.where(lane_a, xp, 0.0), axis=-1, keepdims=True)
        sum_b = jnp.sum(jnp.where(lane_a, 0.0, xp), axis=-1, keepdims=True)
        outs.append(jnp.where(lane_a, sum_a, sum_b))
    return jnp.concatenate(outs, axis=-1)


RWKV_OPERANDS = ("a_t", "r_t", "b_t", "k_t", "v")
RWKV_EPILOGUE = ("bonus", "gate")
RWKV_PREPARE_PIECES = 2
RWKV_PREPARE_SPLIT = (1, 8, 8)


def _sgu_chunk(z, lng_ref, lnb_ref, ws_c, bs_ref, og_ref):
    u = _gelu(z[:, 0:D_SGU])
    v = _gelu(z[:, D_SGU:2 * D_SGU])
    gate = z[:, 2 * D_SGU:3 * D_SGU]
    vc = v - jnp.mean(v, axis=-1, keepdims=True)
    var = jnp.mean(vc * vc, axis=-1, keepdims=True)
    vn = (vc * lax.rsqrt(var + LN_EPS) * lng_ref[...] + lnb_ref[...]).astype(BF16)
    sv = jnp.concatenate(
        [jnp.dot(ws_c[h], vn[:, h * SGU_HEAD_DIM:(h + 1) * SGU_HEAD_DIM],
                 preferred_element_type=F32) for h in range(SGU_HEADS)], axis=1) + bs_ref[...]
    return _rms_norm(u * sv, og_ref[...]) * _silu(gate)


def _rwkv_prepare(zr_ref, par, out, glast_ref):
    C, N = RWKV_CHUNK, RWKV_HEAD_DIM
    w0_ref, w2_ref, a0_ref, a2_ref, kk_ref, ka_ref, rk_ref = par
    lane_a = lax.broadcasted_iota(jnp.int32, (C, PAIR), 1) < N
    row_c = lax.broadcasted_iota(jnp.int32, (C, C), 0)
    col_c = lax.broadcasted_iota(jnp.int32, (C, C), 1)
    tril = (col_c <= row_c).astype(F32).astype(BF16)
    tril3 = jnp.concatenate([tril, tril, tril], axis=1)

    def one_chunk(c):
        rows = slice(c * C, (c + 1) * C)
        r = zr_ref[rows, 0:D_RWKV]
        k = zr_ref[rows, D_RWKV:2 * D_RWKV]
        v = zr_ref[rows, 2 * D_RWKV:3 * D_RWKV]
        gate_act = zr_ref[rows, 3 * D_RWKV:4 * D_RWKV]
        lw = w0_ref[...] + lw_all[rows]
        ld = -(math.exp(-0.5) * LOG2_E) * jax.nn.sigmoid(lw)
        icl = jax.nn.sigmoid(a0_ref[...] + icl_all[rows])
        p1 = ld.astype(BF16)
        r1 = ld - p1.astype(F32)
        p2 = r1.astype(BF16)
        p3 = (r1 - p2.astype(F32)).astype(BF16)
        cum = jnp.dot(tril3, jnp.concatenate([p1, p2, p3], axis=0), preferred_element_type=F32)
        last = cum[C - 1:C, :]
        yield
        k2 = k * ((1.0 - ka_ref[...]) + ka_ref[...] * icl)
        kk = k * kk_ref[...]
        kk = kk * lax.rsqrt(jnp.maximum(_head_sum(kk * kk, lane_a), KK_NORM_FLOOR))
        bvec = kk * icl
        e_neg = jnp.exp2(-cum)
        out["a_t"][rows, :] = (-kk * jnp.exp2(cum - ld)).astype(BF16)
        out["r_t"][rows, :] = (r * jnp.exp2(cum)).astype(BF16)
        out["b_t"][rows, :] = (bvec * e_neg).astype(BF16)
        out["k_t"][rows, :] = (k2 * e_neg).astype(BF16)
        out["v"][rows, :] = v.astype(BF16)
        out["bonus"][rows, :] = _head_sum(r * k2 * rk_ref[...], lane_a) * v
        out["gate"][rows, :] = gate_act
        glast_ref[c] = jnp.exp2(last)
        yield

    lw_all = _dot(jnp.tanh(zr_ref[:, 4 * D_RWKV:4 * D_RWKV + LORA]), w2_ref[...])
    icl_all = _dot(zr_ref[:, 4 * D_RWKV + LORA:4 * D_RWKV + 2 * LORA], a2_ref[...])
    yield
    chunks = [one_chunk(c) for c in range(IN_PROJ_ROWS // C)]
    for _ in range(RWKV_PREPARE_PIECES):
        for gen in chunks:
            next(gen)
            yield


def _in_proj_kernel(tiles_per_batch, n_tiles, x_ref, g_ref, w_ref, lng_ref, lnb_ref, ws_ref,
                    bs_ref, og_ref, mu_ref, w0_ref, w2_ref, a0_ref, a2_ref, kk_ref, ka_ref, rk_ref,
                    ys_ref, at_ref, rt_ref, bt_ref, kt_ref, v_ref, bonus_ref,
                    gate_ref, glast_ref, zs_ref, zr_ref, prev_ref):
    tile = pl.program_id(0)
    par = (w0_ref, w2_ref, a0_ref, a2_ref, kk_ref, ka_ref, rk_ref)
    out = dict(zip(RWKV_OPERANDS + RWKV_EPILOGUE,
                   (at_ref, rt_ref, bt_ref, kt_ref, v_ref, bonus_ref, gate_ref)))

    @pl.when(tile == 0)
    def _():
        prev_ref[...] = jnp.zeros_like(prev_ref)
        zr_ref[...] = jnp.zeros_like(zr_ref)

    @pl.when(tile < n_tiles)
    def _():
        prepare = _rwkv_prepare(zr_ref, par, out, glast_ref)
        first_tile = tile % tiles_per_batch == 0
        row8 = lax.broadcasted_iota(jnp.int32, (SUBLANES, 1), 0)
        xn = _rms_norm(x_ref[...], g_ref[...]).astype(BF16)
        for j in range(3):
            for _ in range(RWKV_PREPARE_SPLIT[j]):
                next(prepare)
            cols = slice(j * D_SGU, (j + 1) * D_SGU)
            zs_ref[:, cols] = jnp.dot(xn, w_ref[:, cols], preferred_element_type=F32)
        for _ in prepare:
            pass
        row = lax.broadcasted_iota(jnp.int32, (SGU_CHUNK, SGU_CHUNK), 0)
        col = lax.broadcasted_iota(jnp.int32, (SGU_CHUNK, SGU_CHUNK), 1)
        ws_c = [jnp.where(col <= row, ws_ref[h], 0.0).astype(BF16) for h in range(SGU_HEADS)]
        n_chunks = IN_PROJ_ROWS // SGU_CHUNK
        lane_tiles = C_RWKV // PAIR
        bounds = [C_SGU + PAIR * (lane_tiles * c // n_chunks) for c in range(n_chunks + 1)]
        gate_cols = slice(3 * D_RWKV, 4 * D_RWKV)
        for c in range(n_chunks):
            z = jnp.dot(xn, w_ref[:, bounds[c]:bounds[c + 1]], preferred_element_type=F32)
            cols = slice(bounds[c] - C_SGU, bounds[c + 1] - C_SGU)
            prev = jnp.where(first_tile, 0.0, prev_ref[:, cols])
            rolled = pltpu.roll(z, 1, axis=0)
            z_prev = jnp.concatenate(
                [jnp.where(row8 == 0, prev, rolled[:SUBLANES]), rolled[SUBLANES:]], axis=0)
            prev_ref[:, cols] = z[IN_PROJ_ROWS - 1:IN_PROJ_ROWS, :]
            zs = z + (z_prev - z) * mu_ref[:, cols]
            if cols.start <= gate_cols.start < cols.stop:
                assert gate_cols.stop <= cols.stop
                lo, hi = gate_cols.start - cols.start, gate_cols.stop - cols.start
                pieces = [zs[:, :lo], _silu(zs[:, lo:hi]), zs[:, hi:]]
                zs = jnp.concatenate([p for p in pieces if p.shape[1]], axis=1)
            zr_ref[:, cols] = zs
            rows = slice(c * SGU_CHUNK, (c + 1) * SGU_CHUNK)
            ys_ref[rows, :] = _sgu_chunk(zs_ref[rows, :], lng_ref, lnb_ref, ws_c, bs_ref,
                                         og_ref).astype(ys_ref.dtype)

    @pl.when(tile == n_tiles)
    def _():
        for _ in _rwkv_prepare(zr_ref, par, out, glast_ref):
            pass


def _in_proj(x2, g, w_in, sgu_par, mu, rwkv_par, seq):
    m = x2.shape[0]
    tm = IN_PROJ_ROWS
    n_tiles = m // tm
    chunks = tm // RWKV_CHUNK
    assert sum(RWKV_PREPARE_SPLIT) == 1 + RWKV_PREPARE_PIECES * chunks
    const = lambda a: pl.BlockSpec(a.shape, lambda i: (0,) * a.ndim)
    cur = lambda i: jnp.minimum(i, n_tiles - 1)
    last = lambda i: jnp.maximum(i - 1, 0)
    tokens = lambda dt: jax.ShapeDtypeStruct((m, D_RWKV), dt)
    outs = pl.pallas_call(
        functools.partial(_in_proj_kernel, seq // tm, n_tiles),
        grid=(n_tiles + 1,),
        in_specs=[pl.BlockSpec((tm, D_MODEL), lambda i: (cur(i), 0)), const(g), const(w_in)]
        + [const(a) for a in sgu_par] + [const(mu)] + [const(a) for a in rwkv_par],
        out_specs=[pl.BlockSpec((tm, D_SGU), lambda i: (cur(i), 0))]
        + [pl.BlockSpec((tm, D_RWKV), lambda i: (last(i), 0))
           for _ in RWKV_OPERANDS + RWKV_EPILOGUE]
        + [pl.BlockSpec((chunks, 1, D_RWKV), lambda i: (last(i), 0, 0))],
        out_shape=[jax.ShapeDtypeStruct((m, D_SGU), BF16)]
        + [tokens(BF16) for _ in RWKV_OPERANDS] + [tokens(F32) for _ in RWKV_EPILOGUE]
        + [jax.ShapeDtypeStruct((m // RWKV_CHUNK, 1, D_RWKV), F32)],
        scratch_shapes=[
            pltpu.VMEM((tm, C_SGU), F32),
            pltpu.VMEM((tm, C_RWKV), F32),
            pltpu.VMEM((1, C_RWKV), F32),
        ],
        compiler_params=pltpu.CompilerParams(
            dimension_semantics=("arbitrary",), vmem_limit_bytes=VMEM_LIMIT_BYTES),
        name="in_proj",
    )(x2, g, w_in, *sgu_par, mu, *rwkv_par)
    y_sgu, prepared, g_last = outs[0], outs[1:-1], outs[-1]
    return y_sgu, dict(zip(RWKV_OPERANDS + RWKV_EPILOGUE, prepared)), g_last


def _block_diag(x, keep_a, keep_b):
    return jnp.concatenate([x * keep_a, x * keep_b], axis=0)


def _dots_paired(lhs, rhs):
    out = []
    for i in range(0, len(lhs), 2):
        m, n = lhs[i].shape[0], rhs[i].shape[1]
        both = jnp.dot(jnp.concatenate([lhs[i], lhs[i + 1]], axis=0),
                       jnp.concatenate([rhs[i], rhs[i + 1]], axis=1),
                       preferred_element_type=F32)
        out += [both[:m, :n], both[m:, n:]]
    return out


def _dots_paired_tn(lhs, rhs):
    out = []
    for i in range(0, len(lhs), 2):
        m, n = lhs[i].shape[1], rhs[i].shape[1]
        both = _dot_tn(jnp.concatenate([lhs[i], lhs[i + 1]], axis=1),
                       jnp.concatenate([rhs[i], rhs[i + 1]], axis=1))
        out += [both[:m, :n], both[m:, n:]]
    return out


def _rwkv_kernel(at_ref, rt_ref, bt_ref, kt_ref, v_ref, bonus_ref, gate_ref,
                 glast_ref, gng_ref, gnb_ref, o_ref, state_ref):
    C, N = RWKV_CHUNK, RWKV_HEAD_DIM

    @pl.when(pl.program_id(1) == 0)
    def _():
        state_ref[...] = jnp.zeros_like(state_ref)

    row = lax.broadcasted_iota(jnp.int32, (C, PAIR), 0)
    lane = lax.broadcasted_iota(jnp.int32, (C, PAIR), 1)
    col = lane & (N - 1)
    lane_a = lane < N
    incl = col <= row
    strict = col < row
    eye = col == row

    units = [(b, p) for b in range(RWKV_BATCHES) for p in range(RWKV_PAIRS)]
    nu = len(units)
    lanes = lambda u: slice(u[1] * PAIR, (u[1] + 1) * PAIR)
    of = lambda ref, u: ref[u[0], :, lanes(u)]
    as_keep = lambda mask: jnp.where(mask, 1.0, 0.0).astype(BF16)
    keep_a, keep_b = as_keep(lane_a), as_keep(jnp.logical_not(lane_a))
    bd = lambda x: _block_diag(x, keep_a, keep_b)

    s_bk = [_dot_nt(jnp.concatenate([of(at_ref, u), of(rt_ref, u)], axis=0),
                    jnp.concatenate([bd(of(bt_ref, u)), bd(of(kt_ref, u))], axis=0))
            for u in units]
    s_b = [s[:, :PAIR] for s in s_bk]
    s_k = [s[:, PAIR:] for s in s_bk]
    n_mat = [jnp.where(strict, s[:C], 0.0) for s in s_b]
    n_bf = [n.astype(BF16) for n in n_mat]
    a_ak = [jnp.where(strict, s[:C], 0.0).astype(BF16) for s in s_k]
    a_rb = [jnp.where(incl, s[C:], 0.0).astype(BF16) for s in s_b]
    a_rk = [jnp.where(incl, s[C:], 0.0).astype(BF16) for s in s_k]

    lvl = ((row >> 1) == (col >> 1)) & ((row & 1) == 1) & ((col & 1) == 0)
    x = [eye.astype(F32) + jnp.where(lvl, n, 0.0) for n in n_mat]
    s = 2
    while s < C:
        shift = s.bit_length()
        lvl = ((row >> shift) == (col >> shift)) & ((row & s) != 0) & ((col & s) == 0)
        lvl_a, lvl_b = as_keep(lvl & lane_a), as_keep(lvl & jnp.logical_not(lane_a))
        x_bf = [xi.astype(BF16) for xi in x]
        if s < SUBLANES:
            xn = _dots_paired(x_bf, [_block_diag(n, lvl_a, lvl_b) for n in n_bf])
            xnx = _dots_paired([t.astype(BF16) for t in xn], [bd(t) for t in x_bf])
            x = [x[i] + xnx[i] for i in range(nu)]
        else:
            groups = [g for g in range(C // SUBLANES) if (g * SUBLANES) & s]
            rows_of = lambda m, g: m[g * SUBLANES:(g + 1) * SUBLANES]
            moving = [jnp.concatenate([rows_of(xi, g) for g in groups], axis=0).astype(BF16)
                      for xi in x]
            xn = _dots_paired(moving, [_block_diag(n, lvl_a, lvl_b) for n in n_bf])
            xnx = _dots_paired([t.astype(BF16) for t in xn], [bd(t) for t in x_bf])
            x = [jnp.concatenate(
                [rows_of(x[i], g) + rows_of(xnx[i], groups.index(g)) if g in groups
                 else rows_of(x[i], g) for g in range(C // SUBLANES)], axis=0)
                 for i in range(nu)]
        s *= 2

    x_bf = [xi.astype(BF16) for xi in x]
    v_bd = [bd(of(v_ref, u)) for u in units]
    av = [t.astype(BF16) for t in _dots_paired(a_ak, v_bd)]
    p_mat = [t.astype(BF16) for t in _dots_paired(x_bf, [bd(of(at_ref, u)) for u in units])]

    t_old = [of(state_ref, u) for u in units]
    t_bd = [bd(t.astype(BF16)) for t in t_old]
    u_mat = _dots_paired(
        [jnp.concatenate([p_mat[i], x_bf[i]], axis=1) for i in range(nu)],
        [jnp.concatenate([t_bd[i], bd(av[i])], axis=0) for i in range(nu)])
    u_bf = [um.astype(BF16) for um in u_mat]
    y = _dots_paired(
        [jnp.concatenate([of(rt_ref, u), a_rb[i], a_rk[i]], axis=1) for i, u in enumerate(units)],
        [jnp.concatenate([t_bd[i], bd(u_bf[i]), v_bd[i]], axis=0) for i in range(nu)])
    full = _dots_paired_tn(
        [jnp.concatenate([of(bt_ref, u), of(kt_ref, u)], axis=0) for u in units],
        [jnp.concatenate([u_bf[i], of(v_ref, u)], axis=0) for i, u in enumerate(units)])
    eye_full = jnp.concatenate([eye] * RWKV_PAIRS, axis=1)
    g_col = [_head_sum(jnp.where(eye_full, glast_ref[b, 0], 0.0), lane_a)
             for b in range(RWKV_BATCHES)]
    for i, u in enumerate(units):
        state_ref[u[0], :, lanes(u)] = g_col[u[0]][:, lanes(u)] * (
            t_old[i] + jnp.where(lane_a, full[i][:N], full[i][N:]))

    for b in range(RWKV_BATCHES):
        yb = jnp.concatenate(y[b * RWKV_PAIRS:(b + 1) * RWKV_PAIRS], axis=1)
        yc = yb - _head_sum(yb, lane_a) * (1.0 / N)
        var = _head_sum(yc * yc, lane_a) * (1.0 / N)
        yn = yc * lax.rsqrt(var + GN_EPS) * gng_ref[...] + gnb_ref[...]
        o_ref[b] = ((yn + bonus_ref[b]) * gate_ref[b]).astype(o_ref.dtype)


def _rwkv(prepared, g_last, gn_g, gn_b, batch, seq):
    C, NB = RWKV_CHUNK, RWKV_BATCHES
    blk = pl.BlockSpec((NB, C, D_RWKV), lambda g, c: (g, c, 0))
    vec = pl.BlockSpec((1, D_RWKV), lambda g, c: (0, 0))
    arrays = [prepared[name].reshape(batch, seq, D_RWKV) for name in RWKV_OPERANDS + RWKV_EPILOGUE]
    out = pl.pallas_call(
        _rwkv_kernel,
        grid=(batch // NB, seq // C),
        in_specs=[blk for _ in arrays]
        + [pl.BlockSpec((NB, 1, 1, D_RWKV), lambda g, c: (g, c, 0, 0)), vec, vec],
        out_specs=blk,
        out_shape=jax.ShapeDtypeStruct((batch, seq, D_RWKV), BF16),
        scratch_shapes=[
            pltpu.VMEM((NB, RWKV_HEAD_DIM, D_RWKV), F32),
        ],
        compiler_params=pltpu.CompilerParams(
            dimension_semantics=("arbitrary", "arbitrary"), vmem_limit_bytes=VMEM_LIMIT_BYTES),
        name="rwkv",
    )(*arrays, g_last.reshape(batch, seq // C, 1, D_RWKV), gn_g, gn_b)
    return out.reshape(batch * seq, D_RWKV)


def _out_attn_front(x_ref, ys_ref, yr_ref, wout_ref, gx_ref, wq_ref, h1_ref, q_ref):
    y = jnp.concatenate([ys_ref[...], yr_ref[...]], axis=1)
    h1 = x_ref[...] + jnp.dot(y, wout_ref[...], preferred_element_type=F32)
    h1_ref[...] = h1
    yield
    hn = _rms_norm(h1, gx_ref[...]).astype(BF16)
    yield
    q_ref[...] = jnp.dot(hn, wq_ref[...], preferred_element_type=F32).astype(BF16)
    yield


def _out_attn_back(h1_ref, q_ref, k_ref, v_ref, wo_ref, gf_ref, o_ref):
    cols = [slice(h * XATTN_HEAD_DIM, (h + 1) * XATTN_HEAD_DIM) for h in range(XATTN_HEADS)]
    scores = [_dot_nt(q_ref[:, sl], k_ref[:, sl]) for sl in cols]
    yield
    probs = []
    for s in scores:
        e = jnp.exp(s - jnp.max(s, axis=-1, keepdims=True))
        probs.append((e / jnp.sum(e, axis=-1, keepdims=True)).astype(BF16))
    yield
    o = jnp.concatenate(
        [jnp.dot(p, v_ref[:, sl], preferred_element_type=F32).astype(BF16)
         for p, sl in zip(probs, cols)], axis=-1)
    yield
    h2 = h1_ref[...] + jnp.dot(o, wo_ref[...], preferred_element_type=F32)
    yield
    o_ref[...] = _rms_norm(h2, gf_ref[...])
    yield


OUT_ATTN_ORDER = "BFBBFBBF"


def _out_attn_kernel(tiles_per_batch, n_tiles, x_ref, ys_ref, yr_ref, wout_ref, gx_ref, wq_ref,
                     mem_ref, gm_ref, wkv_ref, wo_ref, gf_ref, o_ref,
                     h1_a, q_a, h1_b, q_b, k_scr, v_scr):
    step = pl.program_id(0)

    @pl.when(step == 0)
    def _():
        h1_b[...] = jnp.zeros_like(h1_b)
        q_b[...] = jnp.zeros_like(q_b)

    pair_tiles = KV_BATCHES * tiles_per_batch

    @pl.when((step % pair_tiles == 0) & (step < n_tiles))
    def _():
        half = (step // pair_tiles) % 2
        mn = _rms_norm(mem_ref[...], gm_ref[...]).astype(BF16)
        k = jnp.dot(mn, wkv_ref[:, :D_MODEL], preferred_element_type=F32)
        k = (k * (XATTN_HEAD_DIM ** -0.5)).astype(BF16)
        v = jnp.dot(mn, wkv_ref[:, D_MODEL:], preferred_element_type=F32).astype(BF16)
        for b in range(KV_BATCHES):
            k_scr[KV_BATCHES * half + b] = k[b * MEM_LEN:(b + 1) * MEM_LEN]
            v_scr[KV_BATCHES * half + b] = v[b * MEM_LEN:(b + 1) * MEM_LEN]

    back_batch = jnp.maximum(step - 1, 0) // tiles_per_batch
    back_slot = KV_BATCHES * ((back_batch // KV_BATCHES) % 2) + back_batch % KV_BATCHES
    k_ref, v_ref = k_scr.at[back_slot], v_scr.at[back_slot]

    def run(nxt, cur):
        front = _out_attn_front(x_ref, ys_ref, yr_ref, wout_ref, gx_ref, wq_ref, *nxt)
        back = _out_attn_back(*cur, k_ref, v_ref, wo_ref, gf_ref, o_ref)
        for stage in OUT_ATTN_ORDER:
            next(front if stage == "F" else back)

    @pl.when(step % 2 == 0)
    def _():
        run((h1_a, q_a), (h1_b, q_b))

    @pl.when(step % 2 == 1)
    def _():
        run((h1_b, q_b), (h1_a, q_a))


def _out_attn(x2, y_sgu, y_rwkv, w_out, g_x, w_q, mem2, g_mem, w_kv, w_o, g_f, seq):
    m = x2.shape[0]
    tq = ATTN_ROWS
    per_batch = seq // tq
    n_tiles = m // tq
    front_tile = lambda s: jnp.minimum(s, n_tiles - 1)
    front_blk = lambda n: pl.BlockSpec((tq, n), lambda s: (front_tile(s), 0))
    full = lambda a, b: pl.BlockSpec((a, b), lambda s: (0, 0))
    mem_blk = pl.BlockSpec((KV_BATCHES * MEM_LEN, D_MODEL),
                           lambda s: (front_tile(s) // (KV_BATCHES * per_batch), 0))
    return pl.pallas_call(
        functools.partial(_out_attn_kernel, per_batch, n_tiles),
        grid=(n_tiles + 1,),
        in_specs=[
            front_blk(D_MODEL), front_blk(D_SGU), front_blk(D_RWKV),
            full(D_SGU + D_RWKV, D_MODEL), full(1, D_MODEL), full(D_MODEL, D_MODEL),
            mem_blk, full(1, D_MODEL), full(D_MODEL, 2 * D_MODEL),
            full(D_MODEL, D_MODEL), full(1, D_MODEL),
        ],
        out_specs=pl.BlockSpec((tq, D_MODEL), lambda s: (jnp.maximum(s - 1, 0), 0)),
        out_shape=jax.ShapeDtypeStruct((m, D_MODEL), F32),
        scratch_shapes=2 * [pltpu.VMEM((tq, D_MODEL), F32), pltpu.VMEM((tq, D_MODEL), BF16)]
        + 2 * [pltpu.VMEM((2 * KV_BATCHES, MEM_LEN, D_MODEL), BF16)],
        compiler_params=pltpu.CompilerParams(
            dimension_semantics=("arbitrary",), vmem_limit_bytes=VMEM_LIMIT_BYTES),
        name="out_attn",
    )(x2, y_sgu, y_rwkv, w_out, g_x, w_q, mem2, g_mem, w_kv, w_o, g_f)


def kernel(x, mem, ln_mix_g, w_in, sgu_ln_g, sgu_ln_b, sgu_ws, sgu_bs, sgu_out_g, rw_mu, rw_w0, rw_w2, rw_a0, rw_a2, rw_k_k, rw_k_a, rw_r_k, rw_gn_g, rw_gn_b, w_out, ln_x_g, ln_mem_g, w_q, w_kv, w_o, ln_f_g):
    batch, seq, _ = x.shape
    assert w_in.shape[0] == 1, "the final norm is fused into the only layer's last call"
    row = lambda a: a.reshape(1, -1)
    h = x.reshape(batch * seq, D_MODEL)
    mem2 = mem.reshape(batch * MEM_LEN, D_MODEL)
    bs_full = jnp.repeat(sgu_bs[0].T, SGU_HEAD_DIM, axis=1)
    sgu_par = (row(sgu_ln_g[0]), row(sgu_ln_b[0]), sgu_ws[0], bs_full, row(sgu_out_g[0]))
    rwkv_par = (row(rw_w0[0]), rw_w2[0].astype(BF16), row(rw_a0[0]), rw_a2[0].astype(BF16),
                row(rw_k_k[0]), row(rw_k_a[0]), row(rw_r_k[0]))
    y_sgu, prepared, g_last = _in_proj(h, row(ln_mix_g[0]), w_in[0].astype(BF16), sgu_par,
                                       row(rw_mu[0]), rwkv_par, seq)
    y_rwkv = _rwkv(prepared, g_last, row(rw_gn_g[0]), row(rw_gn_b[0]), batch, seq)
    h = _out_attn(h, y_sgu, y_rwkv, w_out[0].astype(BF16), row(ln_x_g[0]), w_q[0].astype(BF16),
                  mem2, row(ln_mem_g[0]), w_kv[0].astype(BF16), w_o[0].astype(BF16),
                  row(ln_f_g), seq)
    return h.reshape(batch, seq, D_MODEL)
```

```python
import functools
import math

import jax
import jax.numpy as jnp
from jax import lax
from jax.experimental import pallas as pl
from jax.experimental.pallas import tpu as pltpu

D_MODEL = 1024
MEM_LEN = 256
D_SGU = 512
SGU_HEADS = 4
SGU_HEAD_DIM = D_SGU // SGU_HEADS
SGU_CHUNK = 128
D_RWKV = 512
RWKV_HEAD_DIM = 64
RWKV_HEADS = D_RWKV // RWKV_HEAD_DIM
LORA = 64
C_SGU = 3 * D_SGU
C_RWKV = 4 * D_RWKV + 2 * LORA
XATTN_HEADS = 4
XATTN_HEAD_DIM = D_MODEL // XATTN_HEADS
RMS_EPS = 1e-6
LN_EPS = 1e-5
GN_EPS = 64e-5
KK_NORM_FLOOR = 1e-24

RWKV_CHUNK = 64
RWKV_BATCHES = 8
PAIR = 2 * RWKV_HEAD_DIM
RWKV_PAIRS = RWKV_HEADS // 2
IN_PROJ_ROWS = 512
ATTN_ROWS = 512
KV_BATCHES = 2
VMEM_LIMIT_BYTES = 48 * 1024 * 1024

SUBLANES = 8
LOG2_E = math.log2(math.e)

F32 = jnp.float32
BF16 = jnp.bfloat16


def _dot(a, b):
    return jnp.dot(a.astype(BF16), b.astype(BF16), preferred_element_type=F32)


def _dot_nt(a, b):
    return lax.dot_general(a.astype(BF16), b.astype(BF16), (((1,), (1,)), ((), ())),
                           preferred_element_type=F32)


def _dot_tn(a, b):
    return lax.dot_general(a.astype(BF16), b.astype(BF16), (((0,), (0,)), ((), ())),
                           preferred_element_type=F32)


def _rms_norm(x, g):
    return x * lax.rsqrt(jnp.mean(x * x, axis=-1, keepdims=True) + RMS_EPS) * g


def _gelu(x):
    return 0.5 * x * (1.0 + lax.erf(x * (2.0 ** -0.5)))


def _silu(x):
    return x * jax.nn.sigmoid(x)


def _head_sum(x, lane_a):
    outs = []
    for p in range(RWKV_PAIRS):
        xp = x[:, p * PAIR:(p + 1) * PAIR]
        sum_a = jnp.sum(jnp.where(lane_a, xp, 0.0), axis=-1, keepdims=True)
        sum_b = jnp.sum(jnp.where(lane_a, 0.0, xp), axis=-1, keepdims=True)
        outs.append(jnp.where(lane_a, sum_a, sum_b))
    return jnp.concatenate(outs, axis=-1)


RWKV_OPERANDS = ("a_t", "r_t", "b_t", "k_t", "v")
RWKV_EPILOGUE = ("bonus", "gate")
RWKV_PREPARE_PIECES = 2
RWKV_PREPARE_SPLIT = (1, 8, 8)
SGU_CHUNKS_PER_DOT = 2


def _sgu_chunk(z, lng_ref, lnb_ref, ws_c, bs_ref, og_ref):
    n = z.shape[0] // SGU_CHUNK
    u = _gelu(z[:, 0:D_SGU])
    v = _gelu(z[:, D_SGU:2 * D_SGU])
    gate = z[:, 2 * D_SGU:3 * D_SGU]
    vc = v - jnp.mean(v, axis=-1, keepdims=True)
    var = jnp.mean(vc * vc, axis=-1, keepdims=True)
    vn = (vc * lax.rsqrt(var + LN_EPS) * lng_ref[...] + lnb_ref[...]).astype(BF16)
    heads = []
    for h in range(SGU_HEADS):
        cols = slice(h * SGU_HEAD_DIM, (h + 1) * SGU_HEAD_DIM)
        side = jnp.concatenate(
            [vn[c * SGU_CHUNK:(c + 1) * SGU_CHUNK, cols] for c in range(n)], axis=1)
        mixed = jnp.dot(ws_c[h], side, preferred_element_type=F32)
        heads.append(jnp.concatenate(
            [mixed[:, c * SGU_HEAD_DIM:(c + 1) * SGU_HEAD_DIM] for c in range(n)], axis=0))
    sv = jnp.concatenate(heads, axis=1) + jnp.concatenate([bs_ref[...]] * n, axis=0)
    return _rms_norm(u * sv, og_ref[...]) * _silu(gate)


def _rwkv_prepare(zr_ref, par, out, glast_ref):
    C, N = RWKV_CHUNK, RWKV_HEAD_DIM
    w0_ref, w2_ref, a0_ref, a2_ref, kk_ref, ka_ref, rk_ref = par
    lane_a = lax.broadcasted_iota(jnp.int32, (C, PAIR), 1) < N
    row_c = lax.broadcasted_iota(jnp.int32, (C, C), 0)
    col_c = lax.broadcasted_iota(jnp.int32, (C, C), 1)
    tril = (col_c <= row_c).astype(F32).astype(BF16)
    tril3 = jnp.concatenate([tril, tril, tril], axis=1)

    def one_chunk(c):
        rows = slice(c * C, (c + 1) * C)
        r = zr_ref[rows, 0:D_RWKV]
        k = zr_ref[rows, D_RWKV:2 * D_RWKV]
        v = zr_ref[rows, 2 * D_RWKV:3 * D_RWKV]
        gate_act = zr_ref[rows, 3 * D_RWKV:4 * D_RWKV]
        lw = w0_ref[...] + lw_all[rows]
        ld = -(math.exp(-0.5) * LOG2_E) * jax.nn.sigmoid(lw)
        icl = jax.nn.sigmoid(a0_ref[...] + icl_all[rows])
        p1 = ld.astype(BF16)
        r1 = ld - p1.astype(F32)
        p2 = r1.astype(BF16)
        p3 = (r1 - p2.astype(F32)).astype(BF16)
        cum = jnp.dot(tril3, jnp.concatenate([p1, p2, p3], axis=0), preferred_element_type=F32)
        last = cum[C - 1:C, :]
        yield
        k2 = k * ((1.0 - ka_ref[...]) + ka_ref[...] * icl)
        kk = k * kk_ref[...]
        kk = kk * lax.rsqrt(jnp.maximum(_head_sum(kk * kk, lane_a), KK_NORM_FLOOR))
        bvec = kk * icl
        e_neg = jnp.exp2(-cum)
        out["a_t"][rows, :] = (-kk * jnp.exp2(cum - ld)).astype(BF16)
        out["r_t"][rows, :] = (r * jnp.exp2(cum)).astype(BF16)
        out["b_t"][rows, :] = (bvec * e_neg).astype(BF16)
        out["k_t"][rows, :] = (k2 * e_neg).astype(BF16)
        out["v"][rows, :] = v.astype(BF16)
        out["bonus"][rows, :] = _head_sum(r * k2 * rk_ref[...], lane_a) * v
        out["gate"][rows, :] = gate_act
        glast_ref[c] = jnp.exp2(last)
        yield

    lw_all = _dot(jnp.tanh(zr_ref[:, 4 * D_RWKV:4 * D_RWKV + LORA]), w2_ref[...])
    icl_all = _dot(zr_ref[:, 4 * D_RWKV + LORA:4 * D_RWKV + 2 * LORA], a2_ref[...])
    yield
    chunks = [one_chunk(c) for c in range(IN_PROJ_ROWS // C)]
    for _ in range(RWKV_PREPARE_PIECES):
        for gen in chunks:
            next(gen)
            yield


def _in_proj_kernel(tiles_per_batch, n_tiles, x_ref, g_ref, w_ref, lng_ref, lnb_ref, ws_ref,
                    bs_ref, og_ref, mu_ref, w0_ref, w2_ref, a0_ref, a2_ref, kk_ref, ka_ref, rk_ref,
                    ys_ref, at_ref, rt_ref, bt_ref, kt_ref, v_ref, bonus_ref,
                    gate_ref, glast_ref, zs_ref, zr_ref, prev_ref):
    tile = pl.program_id(0)
    par = (w0_ref, w2_ref, a0_ref, a2_ref, kk_ref, ka_ref, rk_ref)
    out = dict(zip(RWKV_OPERANDS + RWKV_EPILOGUE,
                   (at_ref, rt_ref, bt_ref, kt_ref, v_ref, bonus_ref, gate_ref)))

    @pl.when(tile == 0)
    def _():
        prev_ref[...] = jnp.zeros_like(prev_ref)
        zr_ref[...] = jnp.zeros_like(zr_ref)

    @pl.when(tile < n_tiles)
    def _():
        prepare = _rwkv_prepare(zr_ref, par, out, glast_ref)
        first_tile = tile % tiles_per_batch == 0
        row8 = lax.broadcasted_iota(jnp.int32, (SUBLANES, 1), 0)
        xn = _rms_norm(x_ref[...], g_ref[...]).astype(BF16)
        for j in range(3):
            for _ in range(RWKV_PREPARE_SPLIT[j]):
                next(prepare)
            cols = slice(j * D_SGU, (j + 1) * D_SGU)
            zs_ref[:, cols] = jnp.dot(xn, w_ref[:, cols], preferred_element_type=F32)
        for _ in prepare:
            pass
        row = lax.broadcasted_iota(jnp.int32, (SGU_CHUNK, SGU_CHUNK), 0)
        col = lax.broadcasted_iota(jnp.int32, (SGU_CHUNK, SGU_CHUNK), 1)
        ws_c = [jnp.where(col <= row, ws_ref[h], 0.0).astype(BF16) for h in range(SGU_HEADS)]
        n_chunks = IN_PROJ_ROWS // SGU_CHUNK
        lane_tiles = C_RWKV // PAIR
        bounds = [C_SGU + PAIR * (lane_tiles * c // n_chunks) for c in range(n_chunks + 1)]
        gate_cols = slice(3 * D_RWKV, 4 * D_RWKV)
        for c in range(n_chunks):
            z = jnp.dot(xn, w_ref[:, bounds[c]:bounds[c + 1]], preferred_element_type=F32)
            cols = slice(bounds[c] - C_SGU, bounds[c + 1] - C_SGU)
            prev = jnp.where(first_tile, 0.0, prev_ref[:, cols])
            rolled = pltpu.roll(z, 1, axis=0)
            z_prev = jnp.concatenate(
                [jnp.where(row8 == 0, prev, rolled[:SUBLANES]), rolled[SUBLANES:]], axis=0)
            prev_ref[:, cols] = z[IN_PROJ_ROWS - 1:IN_PROJ_ROWS, :]
            zs = z + (z_prev - z) * mu_ref[:, cols]
            if cols.start <= gate_cols.start < cols.stop:
                assert gate_cols.stop <= cols.stop
                lo, hi = gate_cols.start - cols.start, gate_cols.stop - cols.start
                pieces = [zs[:, :lo], _silu(zs[:, lo:hi]), zs[:, hi:]]
                zs = jnp.concatenate([p for p in pieces if p.shape[1]], axis=1)
            zr_ref[:, cols] = zs
            if (c + 1) % SGU_CHUNKS_PER_DOT == 0:
                rows = slice((c + 1 - SGU_CHUNKS_PER_DOT) * SGU_CHUNK, (c + 1) * SGU_CHUNK)
                ys_ref[rows, :] = _sgu_chunk(zs_ref[rows, :], lng_ref, lnb_ref, ws_c, bs_ref,
                                             og_ref).astype(ys_ref.dtype)

    @pl.when(tile == n_tiles)
    def _():
        for _ in _rwkv_prepare(zr_ref, par, out, glast_ref):
            pass


def _in_proj(x2, g, w_in, sgu_par, mu, rwkv_par, seq):
    m = x2.shape[0]
    tm = IN_PROJ_ROWS
    n_tiles = m // tm
    chunks = tm // RWKV_CHUNK
    assert sum(RWKV_PREPARE_SPLIT) == 1 + RWKV_PREPARE_PIECES * chunks
    const = lambda a: pl.BlockSpec(a.shape, lambda i: (0,) * a.ndim)
    cur = lambda i: jnp.minimum(i, n_tiles - 1)
    last = lambda i: jnp.maximum(i - 1, 0)
    tokens = lambda dt: jax.ShapeDtypeStruct((m, D_RWKV), dt)
    outs = pl.pallas_call(
        functools.partial(_in_proj_kernel, seq // tm, n_tiles),
        grid=(n_tiles + 1,),
        in_specs=[pl.BlockSpec((tm, D_MODEL), lambda i: (cur(i), 0)), const(g), const(w_in)]
        + [const(a) for a in sgu_par] + [const(mu)] + [const(a) for a in rwkv_par],
        out_specs=[pl.BlockSpec((tm, D_SGU), lambda i: (cur(i), 0))]
        + [pl.BlockSpec((tm, D_RWKV), lambda i: (last(i), 0))
           for _ in RWKV_OPERANDS + RWKV_EPILOGUE]
        + [pl.BlockSpec((chunks, 1, D_RWKV), lambda i: (last(i), 0, 0))],
        out_shape=[jax.ShapeDtypeStruct((m, D_SGU), BF16)]
        + [tokens(BF16) for _ in RWKV_OPERANDS] + [tokens(F32) for _ in RWKV_EPILOGUE]
        + [jax.ShapeDtypeStruct((m // RWKV_CHUNK, 1, D_RWKV), F32)],
        scratch_shapes=[
            pltpu.VMEM((tm, C_SGU), F32),
            pltpu.VMEM((tm, C_RWKV), F32),
            pltpu.VMEM((1, C_RWKV), F32),
        ],
        compiler_params=pltpu.CompilerParams(
            dimension_semantics=("arbitrary",), vmem_limit_bytes=VMEM_LIMIT_BYTES),
        name="in_proj",
    )(x2, g, w_in, *sgu_par, mu, *rwkv_par)
    y_sgu, prepared, g_last = outs[0], outs[1:-1], outs[-1]
    return y_sgu, dict(zip(RWKV_OPERANDS + RWKV_EPILOGUE, prepared)), g_last


def _block_diag(x, keep_a, keep_b):
    return jnp.concatenate([x * keep_a, x * keep_b], axis=0)


def _dots_paired(lhs, rhs):
    out = []
    for i in range(0, len(lhs), 2):
        m, n = lhs[i].shape[0], rhs[i].shape[1]
        both = jnp.dot(jnp.concatenate([lhs[i], lhs[i + 1]], axis=0),
                       jnp.concatenate([rhs[i], rhs[i + 1]], axis=1),
                       preferred_element_type=F32)
        out += [both[:m, :n], both[m:, n:]]
    return out


def _dots_paired_tn(lhs, rhs):
    out = []
    for i in range(0, len(lhs), 2):
        m, n = lhs[i].shape[1], rhs[i].shape[1]
        both = _dot_tn(jnp.concatenate([lhs[i], lhs[i + 1]], axis=1),
                       jnp.concatenate([rhs[i], rhs[i + 1]], axis=1))
        out += [both[:m, :n], both[m:, n:]]
    return out


def _rwkv_kernel(at_ref, rt_ref, bt_ref, kt_ref, v_ref, bonus_ref, gate_ref,
                 glast_ref, gng_ref, gnb_ref, o_ref, state_ref):
    C, N = RWKV_CHUNK, RWKV_HEAD_DIM

    @pl.when(pl.program_id(1) == 0)
    def _():
        state_ref[...] = jnp.zeros_like(state_ref)

    row = lax.broadcasted_iota(jnp.int32, (C, PAIR), 0)
    lane = lax.broadcasted_iota(jnp.int32, (C, PAIR), 1)
    col = lane & (N - 1)
    lane_a = lane < N
    incl = col <= row
    strict = col < row
    eye = col == row

    units = [(b, p) for b in range(RWKV_BATCHES) for p in range(RWKV_PAIRS)]
    nu = len(units)
    lanes = lambda u: slice(u[1] * PAIR, (u[1] + 1) * PAIR)
    of = lambda ref, u: ref[u[0], :, lanes(u)]
    as_keep = lambda mask: jnp.where(mask, 1.0, 0.0).astype(BF16)
    keep_a, keep_b = as_keep(lane_a), as_keep(jnp.logical_not(lane_a))
    bd = lambda x: _block_diag(x, keep_a, keep_b)

    s_bk = [_dot_nt(jnp.concatenate([of(at_ref, u), of(rt_ref, u)], axis=0),
                    jnp.concatenate([bd(of(bt_ref, u)), bd(of(kt_ref, u))], axis=0))
            for u in units]
    s_b = [s[:, :PAIR] for s in s_bk]
    s_k = [s[:, PAIR:] for s in s_bk]
    n_mat = [jnp.where(strict, s[:C], 0.0) for s in s_b]
    n_bf = [n.astype(BF16) for n in n_mat]
    a_ak = [jnp.where(strict, s[:C], 0.0).astype(BF16) for s in s_k]
    a_rb = [jnp.where(incl, s[C:], 0.0).astype(BF16) for s in s_b]
    a_rk = [jnp.where(incl, s[C:], 0.0).astype(BF16) for s in s_k]

    lvl = ((row >> 1) == (col >> 1)) & ((row & 1) == 1) & ((col & 1) == 0)
    x = [eye.astype(F32) + jnp.where(lvl, n, 0.0) for n in n_mat]
    s = 2
    while s < C:
        shift = s.bit_length()
        lvl = ((row >> shift) == (col >> shift)) & ((row & s) != 0) & ((col & s) == 0)
        lvl_a, lvl_b = as_keep(lvl & lane_a), as_keep(lvl & jnp.logical_not(lane_a))
        x_bf = [xi.astype(BF16) for xi in x]
        if s < SUBLANES:
            xn = _dots_paired(x_bf, [_block_diag(n, lvl_a, lvl_b) for n in n_bf])
            xnx = _dots_paired([t.astype(BF16) for t in xn], [bd(t) for t in x_bf])
            x = [x[i] + xnx[i] for i in range(nu)]
        else:
            groups = [g for g in range(C // SUBLANES) if (g * SUBLANES) & s]
            rows_of = lambda m, g: m[g * SUBLANES:(g + 1) * SUBLANES]
            moving = [jnp.concatenate([rows_of(xi, g) for g in groups], axis=0).astype(BF16)
                      for xi in x]
            xn = _dots_paired(moving, [_block_diag(n, lvl_a, lvl_b) for n in n_bf])
            xnx = _dots_paired([t.astype(BF16) for t in xn], [bd(t) for t in x_bf])
            x = [jnp.concatenate(
                [rows_of(x[i], g) + rows_of(xnx[i], groups.index(g)) if g in groups
                 else rows_of(x[i], g) for g in range(C // SUBLANES)], axis=0)
                 for i in range(nu)]
        s *= 2

    x_bf = [xi.astype(BF16) for xi in x]
    v_bd = [bd(of(v_ref, u)) for u in units]
    av = [t.astype(BF16) for t in _dots_paired(a_ak, v_bd)]
    p_mat = [t.astype(BF16) for t in _dots_paired(x_bf, [bd(of(at_ref, u)) for u in units])]

    t_old = [of(state_ref, u) for u in units]
    t_bd = [bd(t.astype(BF16)) for t in t_old]
    u_mat = _dots_paired(
        [jnp.concatenate([p_mat[i], x_bf[i]], axis=1) for i in range(nu)],
        [jnp.concatenate([t_bd[i], bd(av[i])], axis=0) for i in range(nu)])
    u_bf = [um.astype(BF16) for um in u_mat]
    y = _dots_paired(
        [jnp.concatenate([of(rt_ref, u), a_rb[i], a_rk[i]], axis=1) for i, u in enumerate(units)],
        [jnp.concatenate([t_bd[i], bd(u_bf[i]), v_bd[i]], axis=0) for i in range(nu)])
    full = _dots_paired_tn(
        [jnp.concatenate([of(bt_ref, u), of(kt_ref, u)], axis=0) for u in units],
        [jnp.concatenate([u_bf[i], of(v_ref, u)], axis=0) for i, u in enumerate(units)])
    eye_full = jnp.concatenate([eye] * RWKV_PAIRS, axis=1)
    g_col = [_head_sum(jnp.where(eye_full, glast_ref[b, 0], 0.0), lane_a)
             for b in range(RWKV_BATCHES)]
    for i, u in enumerate(units):
        state_ref[u[0], :, lanes(u)] = g_col[u[0]][:, lanes(u)] * (
            t_old[i] + jnp.where(lane_a, full[i][:N], full[i][N:]))

    for b in range(RWKV_BATCHES):
        yb = jnp.concatenate(y[b * RWKV_PAIRS:(b + 1) * RWKV_PAIRS], axis=1)
        yc = yb - _head_sum(yb, lane_a) * (1.0 / N)
        var = _head_sum(yc * yc, lane_a) * (1.0 / N)
        yn = yc * lax.rsqrt(var + GN_EPS) * gng_ref[...] + gnb_ref[...]
        o_ref[b] = ((yn + bonus_ref[b]) * gate_ref[b]).astype(o_ref.dtype)


def _rwkv(prepared, g_last, gn_g, gn_b, batch, seq):
    C, NB = RWKV_CHUNK, RWKV_BATCHES
    blk = pl.BlockSpec((NB, C, D_RWKV), lambda g, c: (g, c, 0))
    vec = pl.BlockSpec((1, D_RWKV), lambda g, c: (0, 0))
    arrays = [prepared[name].reshape(batch, seq, D_RWKV) for name in RWKV_OPERANDS + RWKV_EPILOGUE]
    out = pl.pallas_call(
        _rwkv_kernel,
        grid=(batch // NB, seq // C),
        in_specs=[blk for _ in arrays]
        + [pl.BlockSpec((NB, 1, 1, D_RWKV), lambda g, c: (g, c, 0, 0)), vec, vec],
        out_specs=blk,
        out_shape=jax.ShapeDtypeStruct((batch, seq, D_RWKV), BF16),
        scratch_shapes=[
            pltpu.VMEM((NB, RWKV_HEAD_DIM, D_RWKV), F32),
        ],
        compiler_params=pltpu.CompilerParams(
            dimension_semantics=("arbitrary", "arbitrary"), vmem_limit_bytes=VMEM_LIMIT_BYTES),
        name="rwkv",
    )(*arrays, g_last.reshape(batch, seq // C, 1, D_RWKV), gn_g, gn_b)
    return out.reshape(batch * seq, D_RWKV)


def _out_attn_front(x_ref, ys_ref, yr_ref, wout_ref, gx_ref, wq_ref, h1_ref, q_ref):
    y = jnp.concatenate([ys_ref[...], yr_ref[...]], axis=1)
    h1 = x_ref[...] + jnp.dot(y, wout_ref[...], preferred_element_type=F32)
    h1_ref[...] = h1
    yield
    hn = _rms_norm(h1, gx_ref[...]).astype(BF16)
    yield
    q_ref[...] = jnp.dot(hn, wq_ref[...], preferred_element_type=F32).astype(BF16)
    yield


def _out_attn_back(h1_ref, q_ref, k_ref, v_ref, wo_ref, gf_ref, o_ref):
    cols = [slice(h * XATTN_HEAD_DIM, (h + 1) * XATTN_HEAD_DIM) for h in range(XATTN_HEADS)]
    scores = [_dot_nt(q_ref[:, sl], k_ref[:, sl]) for sl in cols]
    yield
    probs = []
    for s in scores:
        e = jnp.exp(s - jnp.max(s, axis=-1, keepdims=True))
        probs.append((e / jnp.sum(e, axis=-1, keepdims=True)).astype(BF16))
    yield
    o = jnp.concatenate(
        [jnp.dot(p, v_ref[:, sl], preferred_element_type=F32).astype(BF16)
         for p, sl in zip(probs, cols)], axis=-1)
    yield
    h2 = h1_ref[...] + jnp.dot(o, wo_ref[...], preferred_element_type=F32)
    yield
    o_ref[...] = _rms_norm(h2, gf_ref[...])
    yield


OUT_ATTN_ORDER = "BFBBFBBF"


def _out_attn_kernel(tiles_per_batch, n_tiles, x_ref, ys_ref, yr_ref, wout_ref, gx_ref, wq_ref,
                     mem_ref, gm_ref, wkv_ref, wo_ref, gf_ref, o_ref,
                     h1_a, q_a, h1_b, q_b, k_scr, v_scr):
    step = pl.program_id(0)

    @pl.when(step == 0)
    def _():
        h1_b[...] = jnp.zeros_like(h1_b)
        q_b[...] = jnp.zeros_like(q_b)

    pair_tiles = KV_BATCHES * tiles_per_batch

    @pl.when((step % pair_tiles == 0) & (step < n_tiles))
    def _():
        half = (step // pair_tiles) % 2
        mn = _rms_norm(mem_ref[...], gm_ref[...]).astype(BF16)
        k = jnp.dot(mn, wkv_ref[:, :D_MODEL], preferred_element_type=F32)
        k = (k * (XATTN_HEAD_DIM ** -0.5)).astype(BF16)
        v = jnp.dot(mn, wkv_ref[:, D_MODEL:], preferred_element_type=F32).astype(BF16)
        for b in range(KV_BATCHES):
            k_scr[KV_BATCHES * half + b] = k[b * MEM_LEN:(b + 1) * MEM_LEN]
            v_scr[KV_BATCHES * half + b] = v[b * MEM_LEN:(b + 1) * MEM_LEN]

    back_batch = jnp.maximum(step - 1, 0) // tiles_per_batch
    back_slot = KV_BATCHES * ((back_batch // KV_BATCHES) % 2) + back_batch % KV_BATCHES
    k_ref, v_ref = k_scr.at[back_slot], v_scr.at[back_slot]

    def run(nxt, cur):
        front = _out_attn_front(x_ref, ys_ref, yr_ref, wout_ref, gx_ref, wq_ref, *nxt)
        back = _out_attn_back(*cur, k_ref, v_ref, wo_ref, gf_ref, o_ref)
        for stage in OUT_ATTN_ORDER:
            next(front if stage == "F" else back)

    @pl.when(step % 2 == 0)
    def _():
        run((h1_a, q_a), (h1_b, q_b))

    @pl.when(step % 2 == 1)
    def _():
        run((h1_b, q_b), (h1_a, q_a))


def _out_attn(x2, y_sgu, y_rwkv, w_out, g_x, w_q, mem2, g_mem, w_kv, w_o, g_f, seq):
    m = x2.shape[0]
    tq = ATTN_ROWS
    per_batch = seq // tq
    n_tiles = m // tq
    front_tile = lambda s: jnp.minimum(s, n_tiles - 1)
    front_blk = lambda n: pl.BlockSpec((tq, n), lambda s: (front_tile(s), 0))
    full = lambda a, b: pl.BlockSpec((a, b), lambda s: (0, 0))
    mem_blk = pl.BlockSpec((KV_BATCHES * MEM_LEN, D_MODEL),
                           lambda s: (front_tile(s) // (KV_BATCHES * per_batch), 0))
    return pl.pallas_call(
        functools.partial(_out_attn_kernel, per_batch, n_tiles),
        grid=(n_tiles + 1,),
        in_specs=[
            front_blk(D_MODEL), front_blk(D_SGU), front_blk(D_RWKV),
            full(D_SGU + D_RWKV, D_MODEL), full(1, D_MODEL), full(D_MODEL, D_MODEL),
            mem_blk, full(1, D_MODEL), full(D_MODEL, 2 * D_MODEL),
            full(D_MODEL, D_MODEL), full(1, D_MODEL),
        ],
        out_specs=pl.BlockSpec((tq, D_MODEL), lambda s: (jnp.maximum(s - 1, 0), 0)),
        out_shape=jax.ShapeDtypeStruct((m, D_MODEL), F32),
        scratch_shapes=2 * [pltpu.VMEM((tq, D_MODEL), F32), pltpu.VMEM((tq, D_MODEL), BF16)]
        + 2 * [pltpu.VMEM((2 * KV_BATCHES, MEM_LEN, D_MODEL), BF16)],
        compiler_params=pltpu.CompilerParams(
            dimension_semantics=("arbitrary",), vmem_limit_bytes=VMEM_LIMIT_BYTES),
        name="out_attn",
    )(x2, y_sgu, y_rwkv, w_out, g_x, w_q, mem2, g_mem, w_kv, w_o, g_f)


def kernel(x, mem, ln_mix_g, w_in, sgu_ln_g, sgu_ln_b, sgu_ws, sgu_bs, sgu_out_g, rw_mu, rw_w0, rw_w2, rw_a0, rw_a2, rw_k_k, rw_k_a, rw_r_k, rw_gn_g, rw_gn_b, w_out, ln_x_g, ln_mem_g, w_q, w_kv, w_o, ln_f_g):
    batch, seq, _ = x.shape
    assert w_in.shape[0] == 1, "the final norm is fused into the only layer's last call"
    row = lambda a: a.reshape(1, -1)
    h = x.reshape(batch * seq, D_MODEL)
    mem2 = mem.reshape(batch * MEM_LEN, D_MODEL)
    bs_full = jnp.repeat(sgu_bs[0].T, SGU_HEAD_DIM, axis=1)
    sgu_par = (row(sgu_ln_g[0]), row(sgu_ln_b[0]), sgu_ws[0], bs_full, row(sgu_out_g[0]))
    rwkv_par = (row(rw_w0[0]), rw_w2[0].astype(BF16), row(rw_a0[0]), rw_a2[0].astype(BF16),
                row(rw_k_k[0]), row(rw_k_a[0]), row(rw_r_k[0]))
    y_sgu, prepared, g_last = _in_proj(h, row(ln_mix_g[0]), w_in[0].astype(BF16), sgu_par,
                                       row(rw_mu[0]), rwkv_par, seq)
    y_rwkv = _rwkv(prepared, g_last, row(rw_gn_g[0]), row(rw_gn_b[0]), batch, seq)
    h = _out_attn(h, y_sgu, y_rwkv, w_out[0].astype(BF16), row(ln_x_g[0]), w_q[0].astype(BF16),
                  mem2, row(ln_mem_g[0]), w_kv[0].astype(BF16), w_o[0].astype(BF16),
                  row(ln_f_g), seq)
    return h.reshape(batch, seq, D_MODEL)
```
